```python
import jax, jax.numpy as jnp
from jax import lax
import numpy as np

D_MODEL = 2048
BATCH = 4
SEQ = 2048
DEPTH = 4
DEC_BATCH = 8
DEC_SEQ = 32
PAST_LEN = 1024

CHUNK = 64
N_EVEN = (DEPTH + 1) // 2
N_ODD = DEPTH // 2
LRU_WIDTH = D_MODEL // 2
LRU_HEADS = 8
LRU_BLOCK = LRU_WIDTH // LRU_HEADS
CONV_WIDTH = 4
LRU_C = 8.0
MLA_HEADS = 8
QK_NOPE = 128
QK_ROPE = 64
V_DIM = 128
Q_LORA = 512
KV_LORA = 256
ROPE_THETA = 10000.0
Q_BLOCK = 128
IN_COLS = 2 * LRU_WIDTH + Q_LORA + KV_LORA + QK_ROPE
MIX_OUT = LRU_WIDTH + MLA_HEADS * V_DIM
POOL_WINDOWS = (2, 4, 8, 16)
POOL_GROUPS = 4
POOL_GW = D_MODEL // POOL_GROUPS
POOL_BUF = 15
N_MEM = 256
MEM_HEADS = 4
MEM_HEAD_DIM = D_MODEL // MEM_HEADS
D_FF = 4 * D_MODEL
DN_ALPHA = (2.0 * DEPTH) ** 0.25
DN_BETA = (8.0 * DEPTH) ** -0.25
LN_EPS = 1e-5
RMS_EPS = 1e-6

kernel_name = 'hybrid_stream_rglru_mla_pool_step'

F32 = jnp.float32


def layer_norm(x, g, b):
    xf = x.astype(F32)
    mu = jnp.mean(xf, -1, keepdims=True)
    var = jnp.mean(jnp.square(xf - mu), -1, keepdims=True)
    return ((xf - mu) * lax.rsqrt(var + LN_EPS) * g.astype(F32) + b.astype(F32)).astype(x.dtype)


def rms_norm(x, g):
    xf = x.astype(F32)
    return (xf * lax.rsqrt(jnp.mean(jnp.square(xf), -1, keepdims=True) + RMS_EPS) * g.astype(F32)).astype(x.dtype)


def rope(x, pos):
    half = QK_ROPE // 2
    inv = ROPE_THETA ** (-jnp.arange(half, dtype=F32) / half)
    ang = pos.astype(F32)[:, None] * inv[None, :]
    shape = (pos.shape[0],) + (1,) * (x.ndim - 3) + (half,)
    cos, sin = jnp.cos(ang).reshape(shape), jnp.sin(ang).reshape(shape)
    xf = x.astype(F32)
    x1, x2 = xf[..., :half], xf[..., half:]
    return jnp.concatenate([x1 * cos - x2 * sin, x1 * sin + x2 * cos], -1).astype(x.dtype)


def causal_conv(u, buf, w, b):
    full = jnp.concatenate([buf, u], axis=1)
    T = u.shape[1]
    y = b
    for k in range(CONV_WIDTH):
        y = y + w[k] * full[:, k:k + T]
    return y, full[:, full.shape[1] - (CONV_WIDTH - 1):]


def rg_lru(u, h0, ga_w, ga_b, gx_w, gx_b, lam):
    B, T, _ = u.shape
    ub = u.reshape(B, T, LRU_HEADS, LRU_BLOCK)
    r = jax.nn.sigmoid(jnp.einsum('bthi,hij->bthj', ub, ga_w) + ga_b).reshape(B, T, LRU_WIDTH)
    ig = jax.nn.sigmoid(jnp.einsum('bthi,hij->bthj', ub, gx_w) + gx_b).reshape(B, T, LRU_WIDTH)
    log_a = -LRU_C * r.astype(F32) * jax.nn.softplus(-lam.astype(F32))
    a = jnp.exp(log_a)
    bterm = jnp.sqrt(-jnp.expm1(2.0 * log_a)) * (ig * u).astype(F32)
    bterm = bterm.at[:, 0].add(a[:, 0] * h0.astype(F32))

    def combine(left, right):
        a1, b1 = left
        a2, b2 = right
        return a1 * a2, a2 * b1 + b2

    _, h = lax.associative_scan(combine, (a, bterm), axis=1)
    return h.astype(u.dtype), h[:, -1].astype(u.dtype)


def mla_attention(q_nope, q_pe, ckv, kpe, q_pos, w_uk, w_uv):
    B, Tq = q_nope.shape[:2]
    q_lat = jnp.einsum('bqhn,chn->bqhc', q_nope, w_uk)
    scale = (QK_NOPE + QK_ROPE) ** -0.5
    k_chunk = jnp.arange(ckv.shape[1]) // CHUNK

    def block(args):
        ql, qp, qpos = args
        s = jnp.einsum('bqhc,bkc->bhqk', ql, ckv) + jnp.einsum('bqhr,bkr->bhqk', qp, kpe)
        s = s.astype(F32) * scale
        mask = k_chunk[None, :] <= (qpos // CHUNK)[:, None]
        s = jnp.where(mask[None, None], s, -jnp.inf)
        p = jax.nn.softmax(s, axis=-1).astype(ckv.dtype)
        return jnp.einsum('bhqk,bkc->bqhc', p, ckv)

    if Tq > Q_BLOCK and Tq % Q_BLOCK == 0:
        nb = Tq // Q_BLOCK
        split = lambda t: jnp.moveaxis(t.reshape((B, nb, Q_BLOCK) + t.shape[2:]), 1, 0)
        o = lax.map(block, (split(q_lat), split(q_pe), q_pos.reshape(nb, Q_BLOCK)))
        o_lat = jnp.moveaxis(o, 0, 1).reshape(B, Tq, MLA_HEADS, KV_LORA)
    else:
        o_lat = block((q_lat, q_pe, q_pos))
    return jnp.einsum('bqhc,chv->bqhv', o_lat, w_uv).reshape(B, Tq, MLA_HEADS * V_DIM)


def rec_attn_mixer(x, pos, conv_buf, h0, ckv_past, kpe_past,
                   w_in, q_norm_g, w_uq, kv_norm_g, w_uk, w_uv,
                   conv_w, conv_b, ga_w, ga_b, gx_w, gx_b, lam, w_out):
    B, T, _ = x.shape
    z = x @ w_in
    o1 = LRU_WIDTH
    o2 = o1 + LRU_WIDTH
    o3 = o2 + Q_LORA
    o4 = o3 + KV_LORA
    u, gate, c_q, c_kv, k_pe = z[..., :o1], z[..., o1:o2], z[..., o2:o3], z[..., o3:o4], z[..., o4:]
    u_c, new_conv = causal_conv(u, conv_buf, conv_w, conv_b)
    h, h_last = rg_lru(u_c, h0, ga_w, ga_b, gx_w, gx_b, lam)
    rec_out = h * jax.nn.gelu(gate)
    q = (rms_norm(c_q, q_norm_g) @ w_uq).reshape(B, T, MLA_HEADS, QK_NOPE + QK_ROPE)
    q_nope, q_pe = q[..., :QK_NOPE], rope(q[..., QK_NOPE:], pos)
    ckv_new = rms_norm(c_kv, kv_norm_g)
    kpe_new = rope(k_pe, pos)
    ckv_all = jnp.concatenate([ckv_past, ckv_new], axis=1)
    kpe_all = jnp.concatenate([kpe_past, kpe_new], axis=1)
    attn_out = mla_attention(q_nope, q_pe, ckv_all, kpe_all, pos, w_uk, w_uv)
    out = jnp.concatenate([rec_out, attn_out], axis=-1) @ w_out
    return out, new_conv, h_last, ckv_new, kpe_new


def pool_mixer(x, pos, pool_buf, pool_w, pool_scale):
    B, T, D = x.shape
    full_x = jnp.concatenate([pool_buf, x], axis=1)
    full = full_x.astype(F32)
    cs = jnp.concatenate([jnp.zeros((B, 1, D), F32), jnp.cumsum(full, axis=1)], axis=1)
    end = cs[:, POOL_BUF + 1:]
    outs = []
    for g, w in enumerate(POOL_WINDOWS):
        sl = slice(g * POOL_GW, (g + 1) * POOL_GW)
        start = cs[:, POOL_BUF + 1 - w:POOL_BUF + 1 - w + T, sl]
        cnt = jnp.minimum(pos + 1, w).astype(F32)[None, :, None]
        outs.append((end[..., sl] - start) / cnt - full[:, POOL_BUF:, sl])
    d = jnp.stack(outs, axis=2).astype(x.dtype)
    y = jnp.einsum('btgi,gij->btgj', d, pool_w).reshape(B, T, D) * pool_scale
    return y, full_x[:, full_x.shape[1] - POOL_BUF:]


def cross_attention(x, mem_k, mem_v, wq, wo):
    B, T, _ = x.shape
    q = (x @ wq).reshape(B, T, MEM_HEADS, MEM_HEAD_DIM)
    k = mem_k.reshape(B, -1, MEM_HEADS, MEM_HEAD_DIM)
    v = mem_v.reshape(B, -1, MEM_HEADS, MEM_HEAD_DIM)
    s = jnp.einsum('bqhd,bkhd->bhqk', q, k).astype(F32) * (MEM_HEAD_DIM ** -0.5)
    p = jax.nn.softmax(s, axis=-1).astype(x.dtype)
    o = jnp.einsum('bhqk,bkhd->bqhd', p, v).reshape(B, T, D_MODEL)
    return o @ wo


def sq_relu_mlp(x, w_up, w_down):
    return jnp.square(jax.nn.relu(x @ w_up)) @ w_down


def trunk(x, t_past, mem_k, mem_v, conv_buf, h0, ckv_past, kpe_past, pool_buf, W):
    T = x.shape[1]
    pos = t_past + jnp.arange(T)
    convs, hs, ckvs, kpes, pools = [], [], [], [], []
    for l in range(DEPTH):
        j = l // 2
        if l % 2 == 0:
            mix, c, hh, ck, kp = rec_attn_mixer(
                x, pos, conv_buf[j], h0[j], ckv_past[j], kpe_past[j],
                W['w_in'][j], W['q_norm_g'][j], W['w_uq'][j], W['kv_norm_g'][j], W['w_uk'][j], W['w_uv'][j],
                W['conv_w'][j], W['conv_b'][j], W['gate_a_w'][j], W['gate_a_b'][j],
                W['gate_x_w'][j], W['gate_x_b'][j], W['lru_lambda'][j], W['w_out'][j])
            convs.append(c); hs.append(hh); ckvs.append(ck); kpes.append(kp)
        else:
            mix, pb = pool_mixer(x, pos, pool_buf[j], W['pool_w'][j], W['pool_scale'][j])
            pools.append(pb)
        x = layer_norm(DN_ALPHA * x + mix, W['ln_g'][l, 0], W['ln_b'][l, 0])
        x = layer_norm(DN_ALPHA * x + cross_attention(x, mem_k[l], mem_v[l], W['xa_wq'][l], W['xa_wo'][l]),
                       W['ln_g'][l, 1], W['ln_b'][l, 1])
        x = layer_norm(DN_ALPHA * x + sq_relu_mlp(x, W['mlp_up'][l], W['mlp_down'][l]),
                       W['ln_g'][l, 2], W['ln_b'][l, 2])
    return x, jnp.stack(convs), jnp.stack(hs), jnp.stack(ckvs), jnp.stack(kpes), jnp.stack(pools)


def setup_inputs(seed: int = 0) -> dict:
    key = jax.random.key(seed)
    ks = iter(jax.random.split(key, 40))
    nrm = lambda shape, scale: scale * jax.random.normal(next(ks), shape, F32)
    u = jax.random.uniform(next(ks), (N_EVEN, LRU_WIDTH), F32, 0.9, 0.999)
    s = u ** (1.0 / LRU_C)
    lru_lambda = jnp.log(s) - jnp.log1p(-s)
    return {
        'x_prompt': nrm((BATCH, SEQ, D_MODEL), 1.0),
        'x_sample': nrm((DEC_BATCH, DEC_SEQ, D_MODEL), 1.0),
        'mem_prompt': nrm((BATCH, N_MEM, D_MODEL), 1.0),
        'cache_mem_k': nrm((DEPTH, DEC_BATCH, N_MEM, D_MODEL), 1.0),
        'cache_mem_v': nrm((DEPTH, DEC_BATCH, N_MEM, D_MODEL), 1.0),
        'cache_mla_ckv': nrm((N_EVEN, DEC_BATCH, PAST_LEN, KV_LORA), 1.0),
        'cache_mla_kpe': nrm((N_EVEN, DEC_BATCH, PAST_LEN, QK_ROPE), 1.0),
        'state_rglru_h': nrm((N_EVEN, DEC_BATCH, LRU_WIDTH), 1.0),
        'state_rglru_conv': nrm((N_EVEN, DEC_BATCH, CONV_WIDTH - 1, LRU_WIDTH), 1.0),
        'state_pool': nrm((N_ODD, DEC_BATCH, POOL_BUF, D_MODEL), 1.0),
        'w_in': nrm((N_EVEN, D_MODEL, IN_COLS), D_MODEL ** -0.5),
        'q_norm_g': 1.0 + nrm((N_EVEN, Q_LORA), 0.02),
        'w_uq': nrm((N_EVEN, Q_LORA, MLA_HEADS * (QK_NOPE + QK_ROPE)), Q_LORA ** -0.5),
        'kv_norm_g': 1.0 + nrm((N_EVEN, KV_LORA), 0.02),
        'w_uk': nrm((N_EVEN, KV_LORA, MLA_HEADS, QK_NOPE), KV_LORA ** -0.5),
        'w_uv': nrm((N_EVEN, KV_LORA, MLA_HEADS, V_DIM), KV_LORA ** -0.5),
        'conv_w': nrm((N_EVEN, CONV_WIDTH, LRU_WIDTH), CONV_WIDTH ** -0.5),
        'conv_b': nrm((N_EVEN, LRU_WIDTH), 0.01),
        'gate_a_w': nrm((N_EVEN, LRU_HEADS, LRU_BLOCK, LRU_BLOCK), LRU_BLOCK ** -0.5),
        'gate_a_b': nrm((N_EVEN, LRU_HEADS, LRU_BLOCK), 0.01),
        'gate_x_w': nrm((N_EVEN, LRU_HEADS, LRU_BLOCK, LRU_BLOCK), LRU_BLOCK ** -0.5),
        'gate_x_b': nrm((N_EVEN, LRU_HEADS, LRU_BLOCK), 0.01),
        'lru_lambda': lru_lambda,
        'w_out': nrm((N_EVEN, MIX_OUT, D_MODEL), DN_BETA * MIX_OUT ** -0.5),
        'pool_w': nrm((N_ODD, POOL_GROUPS, POOL_GW, POOL_GW), DN_BETA * POOL_GW ** -0.5),
        'pool_scale': 1.0 + nrm((N_ODD, D_MODEL), 0.1),
        'xa_wq': nrm((DEPTH, D_MODEL, D_MODEL), D_MODEL ** -0.5),
        'xa_wk': nrm((DEPTH, D_MODEL, D_MODEL), D_MODEL ** -0.5),
        'xa_wv': nrm((DEPTH, D_MODEL, D_MODEL), D_MODEL ** -0.5),
        'xa_wo': nrm((DEPTH, D_MODEL, D_MODEL), DN_BETA * D_MODEL ** -0.5),
        'mlp_up': nrm((DEPTH, D_MODEL, D_FF), D_MODEL ** -0.5),
        'mlp_down': nrm((DEPTH, D_FF, D_MODEL), DN_BETA * D_FF ** -0.5),
        'ln_g': 1.0 + nrm((DEPTH, 3, D_MODEL), 0.02),
        'ln_b': nrm((DEPTH, 3, D_MODEL), 0.02),
    }


def reference(x_prompt, x_sample, mem_prompt, cache_mem_k, cache_mem_v, cache_mla_ckv, cache_mla_kpe,
              state_rglru_h, state_rglru_conv, state_pool,
              w_in, q_norm_g, w_uq, kv_norm_g, w_uk, w_uv, conv_w, conv_b,
              gate_a_w, gate_a_b, gate_x_w, gate_x_b, lru_lambda, w_out,
              pool_w, pool_scale, xa_wq, xa_wk, xa_wv, xa_wo, mlp_up, mlp_down, ln_g, ln_b):
    W = dict(w_in=w_in, q_norm_g=q_norm_g, w_uq=w_uq, kv_norm_g=kv_norm_g, w_uk=w_uk, w_uv=w_uv,
             conv_w=conv_w, conv_b=conv_b, gate_a_w=gate_a_w, gate_a_b=gate_a_b,
             gate_x_w=gate_x_w, gate_x_b=gate_x_b, lru_lambda=lru_lambda, w_out=w_out,
             pool_w=pool_w, pool_scale=pool_scale, xa_wq=xa_wq, xa_wo=xa_wo,
             mlp_up=mlp_up, mlp_down=mlp_down, ln_g=ln_g, ln_b=ln_b)
    bp = x_prompt.shape[0]
    dt = x_prompt.dtype
    p_mem_k = jnp.einsum('bmd,lde->lbme', mem_prompt, xa_wk)
    p_mem_v = jnp.einsum('bmd,lde->lbme', mem_prompt, xa_wv)
    y_prompt, p_conv, p_h, p_ckv, p_kpe, p_pool = trunk(
        x_prompt, 0, p_mem_k, p_mem_v,
        jnp.zeros((N_EVEN, bp, CONV_WIDTH - 1, LRU_WIDTH), dt),
        jnp.zeros((N_EVEN, bp, LRU_WIDTH), dt),
        jnp.zeros((N_EVEN, bp, 0, KV_LORA), dt),
        jnp.zeros((N_EVEN, bp, 0, QK_ROPE), dt),
        jnp.zeros((N_ODD, bp, POOL_BUF, D_MODEL), dt), W)
    y_sample, s_conv, s_h, s_ckv, s_kpe, s_pool = trunk(
        x_sample, cache_mla_ckv.shape[2], cache_mem_k, cache_mem_v,
        state_rglru_conv, state_rglru_h, cache_mla_ckv, cache_mla_kpe, state_pool, W)
    return (y_prompt, y_sample, p_conv, p_h, p_ckv, p_kpe, p_pool, p_mem_k, p_mem_v,
            s_conv, s_h, s_ckv, s_kpe, s_pool)
```

```python
import functools
import math

import jax
import jax.numpy as jnp
from jax import lax
from jax.experimental import pallas as pl
from jax.experimental.pallas import tpu as pltpu

F32 = jnp.float32
BF16 = jnp.bfloat16

CHUNK = 64
CHUNK_SHIFT = 6
assert 1 << CHUNK_SHIFT == CHUNK
MEM_HEADS = 4
POOL_WINDOWS = (2, 4, 8, 16)
LRU_C = 8.0
ROPE_THETA = 10000.0
LN_EPS = 1e-5
RMS_EPS = 1e-6
NEG_BIG = -1e30

V7X_VMEM_BYTES = 64 * 1024 * 1024
VMEM_LIMIT = V7X_VMEM_BYTES - 8 * 1024 * 1024
LANES = 128
SUBLANES = 8
IN_PROJ_ROWS = 256


def _params(*sem):
    return pltpu.CompilerParams(dimension_semantics=sem, vmem_limit_bytes=VMEM_LIMIT)


def _tile(n, pref):
    if n <= pref:
        return n
    t = pref
    while n % t:
        t -= SUBLANES
    assert t > 0
    return t


def _const_spec(shape):
    nd = len(shape)
    return pl.BlockSpec(shape, lambda *_: (0,) * nd)


def _layer_norm(y, g, b):
    mu = jnp.mean(y, axis=-1, keepdims=True)
    d = y - mu
    var = jnp.mean(d * d, axis=-1, keepdims=True)
    return d * lax.rsqrt(var + LN_EPS) * g + b


def _rms_norm(x, g):
    return x * lax.rsqrt(jnp.mean(x * x, axis=-1, keepdims=True) + RMS_EPS) * g


def _rope_lanes(x, cos_t, sin_t):
    n = x.shape[-1]
    half = 32
    lane = lax.broadcasted_iota(jnp.int32, x.shape, x.ndim - 1)
    first = (lane % (2 * half)) < half
    swapped = jnp.where(first, pltpu.roll(x, n - half, x.ndim - 1), pltpu.roll(x, half, x.ndim - 1))
    return x * cos_t + swapped * sin_t


def _rows_matmul_kernel(a_ref, w_ref, o_ref):
    a = a_ref[...].astype(BF16)
    o_ref[0] = jnp.dot(a, w_ref[0], preferred_element_type=F32).astype(o_ref.dtype)


def _rows_matmul(a, w, out_dtype, tm_pref=512):
    M, K = a.shape
    G, _, N = w.shape
    tm = _tile(M, tm_pref)
    return pl.pallas_call(
        _rows_matmul_kernel,
        out_shape=jax.ShapeDtypeStruct((G, M, N), out_dtype),
        grid=(G, M // tm),
        in_specs=[pl.BlockSpec((tm, K), lambda g, i: (i, 0)),
                  pl.BlockSpec((1, K, N), lambda g, i: (g, 0, 0))],
        out_specs=pl.BlockSpec((1, tm, N), lambda g, i: (g, i, 0)),
        compiler_params=_params("arbitrary", "arbitrary"),
        name="rows_matmul",
    )(a, w)


def _mm_ln_kernel(*refs, n_a, alpha):
    a_refs = refs[:n_a]
    w_ref, res_ref, g_ref, b_ref, o_ref = refs[n_a:]
    y = None
    off = 0
    for a_ref in a_refs:
        k = a_ref.shape[1]
        part = jnp.dot(a_ref[...].astype(BF16), w_ref[off:off + k, :], preferred_element_type=F32)
        y = part if y is None else y + part
        off += k
    o_ref[...] = _layer_norm(alpha * res_ref[...] + y, g_ref[...], b_ref[...])


def _mm_ln(a_list, w, res, g, b, alpha, tm_pref=512):
    M, N = res.shape
    tm = _tile(M, tm_pref)
    in_specs = [pl.BlockSpec((tm, a.shape[1]), lambda i: (i, 0)) for a in a_list]
    in_specs += [_const_spec(w.shape),
                 pl.BlockSpec((tm, N), lambda i: (i, 0)),
                 _const_spec((1, N)), _const_spec((1, N))]
    return pl.pallas_call(
        functools.partial(_mm_ln_kernel, n_a=len(a_list), alpha=alpha),
        out_shape=jax.ShapeDtypeStruct((M, N), F32),
        grid=(M // tm,),
        in_specs=in_specs,
        out_specs=pl.BlockSpec((tm, N), lambda i: (i, 0)),
        compiler_params=_params("arbitrary"),
        name="matmul_res_ln",
    )(*a_list, w, res, g, b)


def _mlp_ln_kernel(x_ref, wu_ref, wd_ref, g_ref, b_ref, o_ref, xb_ref, acc_ref, *, alpha):
    j = pl.program_id(1)

    @pl.when(j == 0)
    def _():
        xb_ref[...] = x_ref[...].astype(BF16)

    h = jnp.dot(xb_ref[...], wu_ref[...], preferred_element_type=F32)
    h = jnp.square(jnp.maximum(h, 0.0)).astype(BF16)
    part = jnp.dot(h, wd_ref[...], preferred_element_type=F32)

    @pl.when(j == 0)
    def _():
        acc_ref[...] = part

    @pl.when(j > 0)
    def _():
        acc_ref[...] += part

    @pl.when(j == pl.num_programs(1) - 1)
    def _():
        o_ref[...] = _layer_norm(alpha * x_ref[...] + acc_ref[...], g_ref[...], b_ref[...])


def _mlp_ln(x, w_up, w_down, g, b, alpha, tm_pref=512, tf_pref=512):
    M, D = x.shape
    FF = w_up.shape[1]
    tm = _tile(M, tm_pref)
    tf = _tile(FF, tf_pref)
    return pl.pallas_call(
        functools.partial(_mlp_ln_kernel, alpha=alpha),
        out_shape=jax.ShapeDtypeStruct((M, D), F32),
        grid=(M // tm, FF // tf),
        in_specs=[pl.BlockSpec((tm, D), lambda i, j: (i, 0)),
                  pl.BlockSpec((D, tf), lambda i, j: (0, j)),
                  pl.BlockSpec((tf, D), lambda i, j: (j, 0)),
                  _const_spec((1, D)), _const_spec((1, D))],
        out_specs=pl.BlockSpec((tm, D), lambda i, j: (i, 0)),
        scratch_shapes=[pltpu.VMEM((tm, D), BF16), pltpu.VMEM((tm, D), F32)],
        compiler_params=_params("arbitrary", "arbitrary"),
        name="mlp_res_ln",
    )(x, w_up, w_down, g, b)


def _in_proj_kernel(x_ref, wa_ref, wb_ref, qg_ref, kg_ref, cos_ref, sin_ref,
                    ug_ref, cqn_ref, ckv_ref, kpe_ref, *, q_lora, kv_lora, qk_rope):
    xb = x_ref[...].astype(BF16)
    ug_ref[...] = jnp.dot(xb, wa_ref[...], preferred_element_type=F32)
    z = jnp.dot(xb, wb_ref[...], preferred_element_type=F32)
    cqn_ref[...] = _rms_norm(z[:, :q_lora], qg_ref[...]).astype(BF16)
    ckv_ref[...] = _rms_norm(z[:, q_lora:q_lora + kv_lora], kg_ref[...])
    pe = z[:, q_lora + kv_lora:]
    kpe_ref[...] = _rope_lanes(pe, cos_ref[...], sin_ref[...])[:, :qk_rope]


def _in_proj(x, wa, wb, qg, kg, cos_t, sin_t, q_lora, kv_lora, qk_rope, tm):
    M, D = x.shape
    assert M % tm == 0 and cos_t.shape[0] % tm == 0
    n_pos_blocks = cos_t.shape[0] // tm
    pe_w = wb.shape[1] - q_lora - kv_lora
    row = lambda i: (i, 0)
    pos = lambda i: (i % n_pos_blocks, 0)
    return pl.pallas_call(
        functools.partial(_in_proj_kernel, q_lora=q_lora, kv_lora=kv_lora, qk_rope=qk_rope),
        out_shape=(jax.ShapeDtypeStruct((M, wa.shape[1]), F32),
                   jax.ShapeDtypeStruct((M, q_lora), BF16),
                   jax.ShapeDtypeStruct((M, kv_lora), F32),
                   jax.ShapeDtypeStruct((M, qk_rope), F32)),
        grid=(M // tm,),
        in_specs=[pl.BlockSpec((tm, D), row), _const_spec(wa.shape), _const_spec(wb.shape),
                  _const_spec((1, q_lora)), _const_spec((1, kv_lora)),
                  pl.BlockSpec((tm, pe_w), pos), pl.BlockSpec((tm, pe_w), pos)],
        out_specs=(pl.BlockSpec((tm, wa.shape[1]), row), pl.BlockSpec((tm, q_lora), row),
                   pl.BlockSpec((tm, kv_lora), row), pl.BlockSpec((tm, qk_rope), row)),
        compiler_params=_params("arbitrary"),
        name="in_proj",
    )(x, wa, wb, qg, kg, cos_t, sin_t)


def _lru_kernel(u_ref, gate_ref, cbuf_ref, h0_ref, cw_ref, cb_ref, gaw_ref, gab_ref,
                gxw_ref, gxb_ref, lam_ref, out_ref, hlast_ref,
                ubuf_ref, hcar_ref, a_ref, b_ref, *, tt, cw, heads):
    t = pl.program_id(1)
    pad = SUBLANES
    tail = cw - 1

    @pl.when(t == 0)
    def _():
        ubuf_ref[pad - tail:pad, :] = cbuf_ref[0]
        hcar_ref[...] = h0_ref[0]

    ubuf_ref[pad:pad + tt, :] = u_ref[...]
    uc = cb_ref[...]
    for k in range(cw):
        uc = uc + cw_ref[k:k + 1, :] * ubuf_ref[pad - tail + k:pad - tail + k + tt, :]
    ubuf_ref[pad - tail:pad, :] = ubuf_ref[pad + tt - tail:pad + tt, :]

    width = uc.shape[1]
    blk = width // heads
    ucb = uc.astype(BF16)
    rs, igs = [], []
    for h in range(heads):
        uh = ucb[:, h * blk:(h + 1) * blk]
        rs.append(jnp.dot(uh, gaw_ref[h], preferred_element_type=F32))
        igs.append(jnp.dot(uh, gxw_ref[h], preferred_element_type=F32))
    r = jax.nn.sigmoid(jnp.concatenate(rs, axis=1) + gab_ref[...])
    ig = jax.nn.sigmoid(jnp.concatenate(igs, axis=1) + gxb_ref[...])
    nlam = -lam_ref[...]
    softplus = jnp.maximum(nlam, 0.0) + jnp.log1p(jnp.exp(-jnp.abs(nlam)))
    log_a = -LRU_C * r * softplus
    a = jnp.exp(log_a)
    a_ref[...] = a
    b_ref[...] = jnp.sqrt(-jnp.tanh(log_a) * (a * a + 1.0)) * (ig * uc)

    def step(i, h):
        h = a_ref[pl.ds(i, 1), :] * h + b_ref[pl.ds(i, 1), :]
        b_ref[pl.ds(i, 1), :] = h
        return h

    h_end = lax.fori_loop(0, tt, step, hcar_ref[...], unroll=8)
    hcar_ref[...] = h_end
    out_ref[...] = (b_ref[...] * jax.nn.gelu(gate_ref[...])).astype(out_ref.dtype)

    @pl.when(t == pl.num_programs(1) - 1)
    def _():
        hlast_ref[0] = h_end


def _lru(ug, conv_buf, h0, cw, cb, gaw, gab, gxw, gxb, lam, B, T, tt_pref=256):
    M = ug.shape[0]
    W = ug.shape[1] // 2
    heads = gaw.shape[0]
    width = cw.shape[0]
    tt = _tile(T, tt_pref)
    nt = T // tt
    row_u = lambda b, t: (b * nt + t, 0)
    row_g = lambda b, t: (b * nt + t, 1)
    return pl.pallas_call(
        functools.partial(_lru_kernel, tt=tt, cw=width, heads=heads),
        out_shape=(jax.ShapeDtypeStruct((M, W), BF16), jax.ShapeDtypeStruct((B, 1, W), F32)),
        grid=(B, nt),
        in_specs=[pl.BlockSpec((tt, W), row_u), pl.BlockSpec((tt, W), row_g),
                  pl.BlockSpec((1, width - 1, W), lambda b, t: (b, 0, 0)),
                  pl.BlockSpec((1, 1, W), lambda b, t: (b, 0, 0)),
                  _const_spec(cw.shape), _const_spec((1, W)),
                  _const_spec(gaw.shape), _const_spec((1, W)),
                  _const_spec(gxw.shape), _const_spec((1, W)), _const_spec((1, W))],
        out_specs=(pl.BlockSpec((tt, W), row_u), pl.BlockSpec((1, 1, W), lambda b, t: (b, 0, 0))),
        scratch_shapes=[pltpu.VMEM((SUBLANES + tt, W), F32), pltpu.VMEM((1, W), F32),
                        pltpu.VMEM((tt, W), F32), pltpu.VMEM((tt, W), F32)],
        compiler_params=_params("arbitrary", "arbitrary"),
        name="rg_lru",
    )(ug, ug, conv_buf, h0, cw, cb, gaw, gab, gxw, gxb, lam)


def _q_prep_kernel(cqn_ref, wuq_ref, wukt_ref, cos_ref, sin_ref, qlat_ref, qpe_ref,
                   *, heads, nope, rope_d):
    q = jnp.dot(cqn_ref[...], wuq_ref[...], preferred_element_type=F32)
    qn = q[:, :heads * nope].astype(BF16)
    pe = _rope_lanes(q[:, heads * nope:], cos_ref[...], sin_ref[...])
    for h in range(heads):
        qlat_ref[0, h] = jnp.dot(qn[:, h * nope:(h + 1) * nope], wukt_ref[h],
                                 preferred_element_type=F32).astype(BF16)
        qpe_ref[0, h] = pe[:, h * rope_d:(h + 1) * rope_d].astype(BF16)


def _q_prep(cqn, wuq, wukt, cos_t, sin_t, B, T, tq_pref=512):
    heads, nope, lat = wukt.shape
    rope_d = (wuq.shape[1] - heads * nope) // heads
    q_lora = cqn.shape[1]
    tq = _tile(T, tq_pref)
    nt = T // tq
    n_pos_blocks = cos_t.shape[0] // tq
    pos = lambda b, t: ((b * nt + t) % n_pos_blocks, 0)
    return pl.pallas_call(
        functools.partial(_q_prep_kernel, heads=heads, nope=nope, rope_d=rope_d),
        out_shape=(jax.ShapeDtypeStruct((B, heads, T, lat), BF16),
                   jax.ShapeDtypeStruct((B, heads, T, rope_d), BF16)),
        grid=(B, nt),
        in_specs=[pl.BlockSpec((tq, q_lora), lambda b, t: (b * nt + t, 0)),
                  _const_spec(wuq.shape), _const_spec(wukt.shape),
                  pl.BlockSpec((tq, heads * rope_d), pos), pl.BlockSpec((tq, heads * rope_d), pos)],
        out_specs=(pl.BlockSpec((1, heads, tq, lat), lambda b, t: (b, 0, t, 0)),
                   pl.BlockSpec((1, heads, tq, rope_d), lambda b, t: (b, 0, t, 0))),
        compiler_params=_params("arbitrary", "arbitrary"),
        name="mla_q_prep",
    )(cqn, wuq, wukt, cos_t, sin_t)


def _mla_kernel(qlat_ref, qpe_ref, ckv_ref, kpe_ref, wuv_ref, o_ref,
                *, tq, tk, q_pos0, n_keys, scale):
    heads = qlat_ref.shape[1]
    lat = qlat_ref.shape[3]
    vdim = wuv_ref.shape[2]
    rows = heads * tq
    qi = pl.program_id(1)
    q_lo = q_pos0 + qi * tq
    vis_all = jnp.minimum(((q_lo >> CHUNK_SHIFT) + 1) * CHUNK, n_keys)
    vis_any = jnp.minimum((((q_lo + tq - 1) >> CHUNK_SHIFT) + 1) * CHUNK, n_keys)
    n_full = vis_all // tk
    n_tot = (vis_any + tk - 1) // tk

    ql = qlat_ref[0].reshape(rows, lat)
    qp = qpe_ref[0].reshape(rows, qpe_ref.shape[3])
    nt_dims = (((1,), (1,)), ((), ()))

    def block(kj, carry, masked):
        m, l, acc = carry
        k0 = pl.multiple_of(kj * tk, tk)
        kc = ckv_ref[0, pl.ds(k0, tk), :]
        kp = kpe_ref[0, pl.ds(k0, tk), :]
        s = lax.dot_general(ql, kc, nt_dims, preferred_element_type=F32)
        s = s + lax.dot_general(qp, kp, nt_dims, preferred_element_type=F32)
        s = s * scale
        if masked:
            kpos = k0 + lax.broadcasted_iota(jnp.int32, (rows, tk), 1)
            qpos = q_lo + (lax.broadcasted_iota(jnp.int32, (rows, tk), 0) & (tq - 1))
            ok = ((kpos >> CHUNK_SHIFT) <= (qpos >> CHUNK_SHIFT)) & (kpos < n_keys)
            s = jnp.where(ok, s, NEG_BIG)
        m_new = jnp.maximum(m, jnp.max(s, axis=1, keepdims=True))
        corr = jnp.exp(m - m_new)
        p = jnp.exp(s - m_new)
        l = corr * l + jnp.sum(p, axis=1, keepdims=True)
        acc = corr * acc + jnp.dot(p.astype(BF16), kc, preferred_element_type=F32)
        return m_new, l, acc

    init = (jnp.full((rows, 1), NEG_BIG, F32), jnp.zeros((rows, 1), F32), jnp.zeros((rows, lat), F32))
    carry = lax.fori_loop(0, n_full, functools.partial(block, masked=False), init)
    m, l, acc = lax.fori_loop(n_full, n_tot, functools.partial(block, masked=True), carry)
    o = (acc / l).astype(BF16)
    for h in range(heads):
        o_ref[:, h * vdim:(h + 1) * vdim] = jnp.dot(
            o[h * tq:(h + 1) * tq], wuv_ref[h], preferred_element_type=F32).astype(o_ref.dtype)


def _mla_attention(qlat, qpe, ckv_all, kpe_all, wuv, q_pos0, n_keys, scale, tq_pref=256, tk_pref=512):
    B, heads, T, lat = qlat.shape
    rope_d = qpe.shape[3]
    vdim = wuv.shape[2]
    Tk = ckv_all.shape[1]
    tq = _tile(T, tq_pref)
    tk = _tile(Tk, tk_pref)
    nt = T // tq
    assert tq & (tq - 1) == 0, "query tile must be a power of two"
    return pl.pallas_call(
        functools.partial(_mla_kernel, tq=tq, tk=tk, q_pos0=q_pos0, n_keys=n_keys, scale=scale),
        out_shape=jax.ShapeDtypeStruct((B * T, heads * vdim), BF16),
        grid=(B, nt),
        in_specs=[pl.BlockSpec((1, heads, tq, lat), lambda b, t: (b, 0, t, 0)),
                  pl.BlockSpec((1, heads, tq, rope_d), lambda b, t: (b, 0, t, 0)),
                  pl.BlockSpec((1, Tk, lat), lambda b, t: (b, 0, 0)),
                  pl.BlockSpec((1, Tk, rope_d), lambda b, t: (b, 0, 0)),
                  _const_spec(wuv.shape)],
        out_specs=pl.BlockSpec((tq, heads * vdim), lambda b, t: (b * nt + t, 0)),
        compiler_params=_params("arbitrary", "arbitrary"),
        name="mla_attention",
    )(qlat, qpe, ckv_all, kpe_all, wuv)


def _xattn_kernel(q_ref, k_ref, v_ref, o_ref, *, heads, scale):
    hd = q_ref.shape[1] // heads
    nt_dims = (((1,), (1,)), ((), ()))
    for h in range(heads):
        sl = slice(h * hd, (h + 1) * hd)
        q = q_ref[:, sl]
        k = k_ref[0, :, sl].astype(BF16)
        v = v_ref[0, :, sl].astype(BF16)
        s = lax.dot_general(q, k, nt_dims, preferred_element_type=F32) * scale
        p = jnp.exp(s - jnp.max(s, axis=1, keepdims=True))
        l = jnp.sum(p, axis=1, keepdims=True)
        o = jnp.dot(p.astype(BF16), v, preferred_element_type=F32) / l
        o_ref[:, sl] = o.astype(o_ref.dtype)


def _xattn(q, mem_k, mem_v, B, T, tq_pref=512):
    M, D = q.shape
    n_mem = mem_k.shape[1]
    tq = _tile(T, tq_pref)
    nt = T // tq
    scale = (D // MEM_HEADS) ** -0.5
    return pl.pallas_call(
        functools.partial(_xattn_kernel, heads=MEM_HEADS, scale=scale),
        out_shape=jax.ShapeDtypeStruct((M, D), BF16),
        grid=(B, nt),
        in_specs=[pl.BlockSpec((tq, D), lambda b, t: (b * nt + t, 0)),
                  pl.BlockSpec((1, n_mem, D), lambda b, t: (b, 0, 0)),
                  pl.BlockSpec((1, n_mem, D), lambda b, t: (b, 0, 0))],
        out_specs=pl.BlockSpec((tq, D), lambda b, t: (b * nt + t, 0)),
        compiler_params=_params("arbitrary", "arbitrary"),
        name="mem_xattn",
    )(q, mem_k, mem_v)


def _pool_ln_kernel(x_ref, pbuf_ref, pw_ref, ps_ref, g_ref, b_ref, o_ref, full_ref, y_ref,
                    *, tt, pos0, alpha):
    t = pl.program_id(1)
    halo = 2 * SUBLANES
    groups = pw_ref.shape[0]
    gw = pw_ref.shape[1]

    @pl.when(t == 0)
    def _():
        full_ref[0:halo, :] = pbuf_ref[0]

    full_ref[halo:halo + tt, :] = x_ref[...]
    pos = pos0 + t * tt + lax.broadcasted_iota(jnp.int32, (tt, 1), 0)
    for g in range(groups):
        w = POOL_WINDOWS[g]
        sl = slice(g * gw, (g + 1) * gw)
        f = full_ref[:, sl]
        s = f
        d = 1
        while d < w:
            s = s + pltpu.roll(s, d, 0)
            d *= 2
        cnt = jnp.minimum(pos + 1, w).astype(F32)
        dlt = (s[halo:] / cnt - f[halo:]).astype(BF16)
        y_ref[:, sl] = jnp.dot(dlt, pw_ref[g], preferred_element_type=F32)
    full_ref[0:halo, :] = full_ref[tt:tt + halo, :]
    o_ref[...] = _layer_norm(alpha * x_ref[...] + y_ref[...] * ps_ref[...], g_ref[...], b_ref[...])


def _pool_ln(x, pbuf16, pw, ps, g, b, alpha, B, T, pos0, tt_pref=512):
    M, D = x.shape
    tt = _tile(T, tt_pref)
    nt = T // tt
    halo = 2 * SUBLANES
    assert max(POOL_WINDOWS) <= halo and tt >= halo
    row = lambda bb, t: (bb * nt + t, 0)
    return pl.pallas_call(
        functools.partial(_pool_ln_kernel, tt=tt, pos0=pos0, alpha=alpha),
        out_shape=jax.ShapeDtypeStruct((M, D), F32),
        grid=(B, nt),
        in_specs=[pl.BlockSpec((tt, D), row),
                  pl.BlockSpec((1, halo, D), lambda bb, t: (bb, 0, 0)),
                  _const_spec(pw.shape), _const_spec((1, D)), _const_spec((1, D)), _const_spec((1, D))],
        out_specs=pl.BlockSpec((tt, D), row),
        scratch_shapes=[pltpu.VMEM((halo + tt, D), F32), pltpu.VMEM((tt, D), F32)],
        compiler_params=_params("arbitrary", "arbitrary"),
        name="pool_res_ln",
    )(x, pbuf16, pw, ps, g, b)


def _rope_tables(pos, rope_d, heads):
    half = rope_d // 2
    inv = ROPE_THETA ** (-jnp.arange(half, dtype=F32) / half)
    ang = pos.astype(F32)[:, None] * inv[None, :]
    cos, sin = jnp.cos(ang), jnp.sin(ang)
    cos_t = jnp.tile(jnp.concatenate([cos, cos], axis=1), (1, heads))
    sin_t = jnp.tile(jnp.concatenate([-sin, sin], axis=1), (1, heads))
    return cos_t, sin_t


def _prep_weights(w_in, q_norm_g, w_uq, kv_norm_g, w_uk, w_uv, conv_w, conv_b,
                  gate_a_w, gate_a_b, gate_x_w, gate_x_b, lru_lambda, w_out,
                  pool_w, pool_scale, xa_wq, xa_wo, mlp_up, mlp_down, ln_g, ln_b):
    n_even = w_in.shape[0]
    lru_w = conv_w.shape[2]
    q_lora = q_norm_g.shape[1]
    kv_lora = kv_norm_g.shape[1]
    heads, nope = w_uk.shape[2], w_uk.shape[3]
    rope_d = w_uq.shape[2] // heads - nope
    ug_cols = 2 * lru_w
    pe_pad = (-rope_d) % LANES
    wa = w_in[:, :, :ug_cols].astype(BF16)
    wb = jnp.pad(w_in[:, :, ug_cols:], ((0, 0), (0, 0), (0, pe_pad))).astype(BF16)
    wuq = w_uq.reshape(n_even, q_lora, heads, nope + rope_d)
    wuq = jnp.concatenate([wuq[..., :nope].reshape(n_even, q_lora, heads * nope),
                           wuq[..., nope:].reshape(n_even, q_lora, heads * rope_d)], axis=2).astype(BF16)
    wukt = jnp.transpose(w_uk, (0, 2, 3, 1)).astype(BF16)
    wuv = jnp.transpose(w_uv, (0, 2, 1, 3)).astype(BF16)
    return dict(
        wa=wa, wb=wb, qg=q_norm_g[:, None, :], kg=kv_norm_g[:, None, :], wuq=wuq, wukt=wukt, wuv=wuv,
        cw=conv_w, cb=conv_b[:, None, :], gaw=gate_a_w.astype(BF16), gab=gate_a_b.reshape(n_even, 1, lru_w),
        gxw=gate_x_w.astype(BF16), gxb=gate_x_b.reshape(n_even, 1, lru_w), lam=lru_lambda[:, None, :],
        w_out=w_out.astype(BF16), pw=pool_w.astype(BF16), ps=pool_scale[:, None, :],
        wq=xa_wq.astype(BF16), wo=xa_wo.astype(BF16), up=mlp_up.astype(BF16), down=mlp_down.astype(BF16),
        ln_g=ln_g, ln_b=ln_b, q_lora=q_lora, kv_lora=kv_lora, heads=heads, nope=nope, rope_d=rope_d)


def _trunk(x3, t_past, mem_k, mem_v, conv_buf, h0, ckv_past, kpe_past, pool_buf, P):
    B, T, D = x3.shape
    depth = P['ln_g'].shape[0]
    alpha = (2.0 * depth) ** 0.25
    heads, rope_d = P['heads'], P['rope_d']
    x = x3.reshape(B * T, D)
    pos = t_past + jnp.arange(T)
    cos_t, sin_t = _rope_tables(pos, rope_d, heads)
    tm_in = _tile(T, IN_PROJ_ROWS) if T >= IN_PROJ_ROWS else B * T
    reps = max(1, tm_in // T)
    cos_rows, sin_rows = jnp.tile(cos_t, (reps, 1)), jnp.tile(sin_t, (reps, 1))
    scale = (P['nope'] + rope_d) ** -0.5
    ln = lambda l, k: (P['ln_g'][l, k][None, :], P['ln_b'][l, k][None, :])
    convs, hs, ckvs, kpes, pools = [], [], [], [], []
    for l in range(depth):
        j = l // 2
        if l % 2 == 0:
            ug, cqn, ckv, kpe = _in_proj(x, P['wa'][j], P['wb'][j], P['qg'][j], P['kg'][j],
                                         cos_rows[:, :LANES], sin_rows[:, :LANES],
                                         P['q_lora'], P['kv_lora'], rope_d, tm_in)
            lru_w = ug.shape[1] // 2
            rec, h_last = _lru(ug, conv_buf[j], h0[j][:, None, :], P['cw'][j], P['cb'][j],
                               P['gaw'][j], P['gab'][j], P['gxw'][j], P['gxb'][j], P['lam'][j], B, T)
            qlat, qpe = _q_prep(cqn, P['wuq'][j], P['wukt'][j], cos_t, sin_t, B, T)
            ckv3 = ckv.reshape(B, T, -1)
            kpe3 = kpe.reshape(B, T, -1)
            n_keys = ckv_past[j].shape[1] + T
            tk = _tile(n_keys, 512) if n_keys % 512 == 0 else 256
            padk = (-n_keys) % tk
            ckv_all = jnp.concatenate([ckv_past[j].astype(BF16), ckv3.astype(BF16),
                                       jnp.zeros((B, padk, ckv3.shape[2]), BF16)], axis=1)
            kpe_all = jnp.concatenate([kpe_past[j].astype(BF16), kpe3.astype(BF16),
                                       jnp.zeros((B, padk, kpe3.shape[2]), BF16)], axis=1)
            attn = _mla_attention(qlat, qpe, ckv_all, kpe_all, P['wuv'][j], t_past, n_keys, scale,
                                  tk_pref=tk)
            x = _mm_ln([rec, attn], P['w_out'][j], x, *ln(l, 0), alpha)
            tail = P['cw'].shape[1] - 1
            convs.append(ug.reshape(B, T, -1)[:, T - tail:, :lru_w])
            hs.append(h_last[:, 0, :])
            ckvs.append(ckv3)
            kpes.append(kpe3)
        else:
            nbuf = pool_buf[j].shape[1]
            pools.append(x.reshape(B, T, D)[:, T - nbuf:])
            pbuf16 = jnp.pad(pool_buf[j], ((0, 0), (2 * SUBLANES - nbuf, 0), (0, 0)))
            x = _pool_ln(x, pbuf16, P['pw'][j], P['ps'][j], *ln(l, 0), alpha, B, T, t_past)
        q = _rows_matmul(x, P['wq'][l][None], BF16)[0]
        o = _xattn(q, mem_k[l], mem_v[l], B, T)
        x = _mm_ln([o], P['wo'][l], x, *ln(l, 1), alpha)
        x = _mlp_ln(x, P['up'][l], P['down'][l], *ln(l, 2), alpha)
    return (x.reshape(B, T, D), jnp.stack(convs), jnp.stack(hs), jnp.stack(ckvs), jnp.stack(kpes),
            jnp.stack(pools))


def kernel(x_prompt, x_sample, mem_prompt, cache_mem_k, cache_mem_v, cache_mla_ckv, cache_mla_kpe,
           state_rglru_h, state_rglru_conv, state_pool,
           w_in, q_norm_g, w_uq, kv_norm_g, w_uk, w_uv, conv_w, conv_b,
           gate_a_w, gate_a_b, gate_x_w, gate_x_b, lru_lambda, w_out,
           pool_w, pool_scale, xa_wq, xa_wk, xa_wv, xa_wo, mlp_up, mlp_down, ln_g, ln_b):
    P = _prep_weights(w_in, q_norm_g, w_uq, kv_norm_g, w_uk, w_uv, conv_w, conv_b,
                      gate_a_w, gate_a_b, gate_x_w, gate_x_b, lru_lambda, w_out,
                      pool_w, pool_scale, xa_wq, xa_wo, mlp_up, mlp_down, ln_g, ln_b)
    depth = ln_g.shape[0]
    n_even, n_odd = w_in.shape[0], pool_w.shape[0]
    bp, _, d_model = x_prompt.shape
    n_mem = mem_prompt.shape[1]
    dt = x_prompt.dtype
    lru_w = conv_w.shape[2]
    wkv = jnp.concatenate([xa_wk, xa_wv], axis=0).astype(BF16)
    kv = _rows_matmul(mem_prompt.reshape(bp * n_mem, d_model), wkv, F32)
    kv = kv.reshape(2, depth, bp, n_mem, d_model)
    p_mem_k, p_mem_v = kv[0], kv[1]
    y_prompt, p_conv, p_h, p_ckv, p_kpe, p_pool = _trunk(
        x_prompt, 0, p_mem_k, p_mem_v,
        jnp.zeros((n_even, bp, conv_w.shape[1] - 1, lru_w), dt),
        jnp.zeros((n_even, bp, lru_w), dt),
        jnp.zeros((n_even, bp, 0, kv_norm_g.shape[1]), dt),
        jnp.zeros((n_even, bp, 0, cache_mla_kpe.shape[3]), dt),
        jnp.zeros((n_odd, bp, state_pool.shape[2], d_model), dt), P)
    y_sample, s_conv, s_h, s_ckv, s_kpe, s_pool = _trunk(
        x_sample, cache_mla_ckv.shape[2], cache_mem_k, cache_mem_v,
        state_rglru_conv, state_rglru_h, cache_mla_ckv, cache_mla_kpe, state_pool, P)
    return (y_prompt, y_sample, p_conv, p_h, p_ckv, p_kpe, p_pool, p_mem_k, p_mem_v,
            s_conv, s_h, s_ckv, s_kpe, s_pool)
```

```python
import functools

import jax
import jax.numpy as jnp
from jax import lax
from jax.experimental import pallas as pl
from jax.experimental.pallas import tpu as pltpu

F32 = jnp.float32
BF16 = jnp.bfloat16

CHUNK = 64
CHUNK_SHIFT = 6
assert 1 << CHUNK_SHIFT == CHUNK
MEM_HEADS = 4
POOL_WINDOWS = (2, 4, 8, 16)
LRU_C = 8.0
ROPE_THETA = 10000.0
LN_EPS = 1e-5
RMS_EPS = 1e-6
NEG_BIG = -1e30

V7X_VMEM_BYTES = 64 * 1024 * 1024
VMEM_LIMIT = V7X_VMEM_BYTES - 8 * 1024 * 1024
LANES = 128
SUBLANES = 8
IN_PROJ_ROWS = 256
NT_DIMS = (((1,), (1,)), ((), ()))


def _params(*sem):
    return pltpu.CompilerParams(dimension_semantics=sem, vmem_limit_bytes=VMEM_LIMIT)


def _tile(n, pref):
    if n <= pref:
        return n
    t = pref
    while n % t:
        t -= SUBLANES
    assert t > 0
    return t


def _sel(arr, *lead):
    n_lead = len(lead)
    rest = arr.shape[n_lead:]
    idx = tuple(lead) + (0,) * len(rest)
    return pl.BlockSpec((None,) * n_lead + tuple(rest), lambda *_: idx)


def _layer_norm(y, g, b):
    mu = jnp.mean(y, axis=-1, keepdims=True)
    d = y - mu
    var = jnp.mean(d * d, axis=-1, keepdims=True)
    return d * lax.rsqrt(var + LN_EPS) * g + b


def _rms_norm(x, g):
    return x * lax.rsqrt(jnp.mean(x * x, axis=-1, keepdims=True) + RMS_EPS) * g


def _rope_lanes(x, cos_t, sin_t):
    n = x.shape[-1]
    half = 32
    lane = lax.broadcasted_iota(jnp.int32, x.shape, x.ndim - 1)
    first = (lane % (2 * half)) < half
    swapped = jnp.where(first, pltpu.roll(x, n - half, x.ndim - 1), pltpu.roll(x, half, x.ndim - 1))
    return x * cos_t + swapped * sin_t


def _rows_matmul_kernel(a_ref, w_ref, o_ref):
    a = a_ref[...].astype(BF16)
    o_ref[...] = jnp.dot(a, w_ref[...], preferred_element_type=F32).astype(o_ref.dtype)


def _rows_matmul(a, w, out_dtype, g0=0, n_g=None, tm_pref=512):
    M, K = a.shape
    N = w.shape[2]
    n_g = w.shape[0] if n_g is None else n_g
    tm = _tile(M, tm_pref)
    return pl.pallas_call(
        _rows_matmul_kernel,
        out_shape=jax.ShapeDtypeStruct((n_g, M, N), out_dtype),
        grid=(n_g, M // tm),
        in_specs=[pl.BlockSpec((tm, K), lambda g, i: (i, 0)),
                  pl.BlockSpec((None, K, N), lambda g, i: (g0 + g, 0, 0))],
        out_specs=pl.BlockSpec((None, tm, N), lambda g, i: (g, i, 0)),
        compiler_params=_params("arbitrary", "arbitrary"),
        name="rows_matmul",
    )(a, w)


def _mm_ln_kernel(*refs, n_a, alpha):
    a_refs = refs[:n_a]
    w_ref, res_ref, g_ref, b_ref, o_ref = refs[n_a:]
    y = None
    off = 0
    for a_ref in a_refs:
        k = a_ref.shape[1]
        part = jnp.dot(a_ref[...].astype(BF16), w_ref[off:off + k, :], preferred_element_type=F32)
        y = part if y is None else y + part
        off += k
    o_ref[...] = _layer_norm(alpha * res_ref[...] + y, g_ref[...], b_ref[...])


def _mm_ln(a_list, w, wl, res, ln_g, ln_b, ln_i, alpha, tm_pref=512):
    M, N = res.shape
    tm = _tile(M, tm_pref)
    in_specs = [pl.BlockSpec((tm, a.shape[1]), lambda i: (i, 0)) for a in a_list]
    in_specs += [_sel(w, wl), pl.BlockSpec((tm, N), lambda i: (i, 0)), _sel(ln_g, ln_i), _sel(ln_b, ln_i)]
    return pl.pallas_call(
        functools.partial(_mm_ln_kernel, n_a=len(a_list), alpha=alpha),
        out_shape=jax.ShapeDtypeStruct((M, N), F32),
        grid=(M // tm,),
        in_specs=in_specs,
        out_specs=pl.BlockSpec((tm, N), lambda i: (i, 0)),
        compiler_params=_params("arbitrary"),
        name="matmul_res_ln",
    )(*a_list, w, res, ln_g, ln_b)


def _mlp_ln_kernel(x_ref, wu_ref, wd_ref, g_ref, b_ref, o_ref, xb_ref, acc_ref, *, alpha):
    j = pl.program_id(1)

    @pl.when(j == 0)
    def _():
        xb_ref[...] = x_ref[...].astype(BF16)
        acc_ref[...] = jnp.zeros_like(acc_ref)

    h = jnp.dot(xb_ref[...], wu_ref[...], preferred_element_type=F32)
    h = jnp.square(jnp.maximum(h, 0.0)).astype(BF16)
    acc_ref[...] += jnp.dot(h, wd_ref[...], preferred_element_type=F32)

    @pl.when(j == pl.num_programs(1) - 1)
    def _():
        o_ref[...] = _layer_norm(alpha * x_ref[...] + acc_ref[...], g_ref[...], b_ref[...])


def _mlp_ln(x, w_up, w_down, l, ln_g, ln_b, ln_i, alpha, tm_pref=512, tf_pref=1024):
    M, D = x.shape
    FF = w_up.shape[2]
    tm = _tile(M, tm_pref)
    tf = _tile(FF, tf_pref)
    return pl.pallas_call(
        functools.partial(_mlp_ln_kernel, alpha=alpha),
        out_shape=jax.ShapeDtypeStruct((M, D), F32),
        grid=(M // tm, FF // tf),
        in_specs=[pl.BlockSpec((tm, D), lambda i, j: (i, 0)),
                  pl.BlockSpec((None, D, tf), lambda i, j: (l, 0, j)),
                  pl.BlockSpec((None, tf, D), lambda i, j: (l, j, 0)),
                  _sel(ln_g, ln_i), _sel(ln_b, ln_i)],
        out_specs=pl.BlockSpec((tm, D), lambda i, j: (i, 0)),
        scratch_shapes=[pltpu.VMEM((tm, D), BF16), pltpu.VMEM((tm, D), F32)],
        compiler_params=_params("arbitrary", "arbitrary"),
        name="mlp_res_ln",
    )(x, w_up, w_down, ln_g, ln_b)


def _in_proj_kernel(x_ref, wa_ref, wb_ref, qg_ref, kg_ref, cos_ref, sin_ref,
                    ug_ref, cqn_ref, ckv_ref, kpe_ref, *, q_lora, kv_lora, qk_rope):
    xb = x_ref[...].astype(BF16)
    ug_ref[...] = jnp.dot(xb, wa_ref[...], preferred_element_type=F32)
    z = jnp.dot(xb, wb_ref[...], preferred_element_type=F32)
    cqn_ref[...] = _rms_norm(z[:, :q_lora], qg_ref[...]).astype(BF16)
    ckv_ref[...] = _rms_norm(z[:, q_lora:q_lora + kv_lora], kg_ref[...])
    pe = z[:, q_lora + kv_lora:]
    kpe_ref[...] = _rope_lanes(pe, cos_ref[...], sin_ref[...])[:, :qk_rope]


def _in_proj(x, P, j, cos_t, sin_t, tm):
    M, D = x.shape
    wa, wb = P['wa'], P['wb']
    q_lora, kv_lora, qk_rope = P['q_lora'], P['kv_lora'], P['rope_d']
    assert M % tm == 0 and cos_t.shape[0] % tm == 0
    n_pos_blocks = cos_t.shape[0] // tm
    pe_w = wb.shape[2] - q_lora - kv_lora
    row = lambda i: (i, 0)
    pos = lambda i: (i % n_pos_blocks, 0)
    return pl.pallas_call(
        functools.partial(_in_proj_kernel, q_lora=q_lora, kv_lora=kv_lora, qk_rope=qk_rope),
        out_shape=(jax.ShapeDtypeStruct((M, wa.shape[2]), F32),
                   jax.ShapeDtypeStruct((M, q_lora), BF16),
                   jax.ShapeDtypeStruct((M, kv_lora), F32),
                   jax.ShapeDtypeStruct((M, qk_rope), F32)),
        grid=(M // tm,),
        in_specs=[pl.BlockSpec((tm, D), row), _sel(wa, j), _sel(wb, j),
                  _sel(P['qg'], j), _sel(P['kg'], j),
                  pl.BlockSpec((tm, pe_w), pos), pl.BlockSpec((tm, pe_w), pos)],
        out_specs=(pl.BlockSpec((tm, wa.shape[2]), row), pl.BlockSpec((tm, q_lora), row),
                   pl.BlockSpec((tm, kv_lora), row), pl.BlockSpec((tm, qk_rope), row)),
        compiler_params=_params("arbitrary"),
        name="in_proj",
    )(x, wa, wb, P['qg'], P['kg'], cos_t, sin_t)


def _lru_kernel(u_ref, gate_ref, cbuf_ref, h0_ref, cw_ref, cb_ref, gaw_ref, gab_ref,
                gxw_ref, gxb_ref, lam_ref, out_ref, hlast_ref,
                ubuf_ref, hcar_ref, a_ref, b_ref, *, tt, cw, heads):
    t = pl.program_id(1)
    pad = SUBLANES
    tail = cw - 1

    @pl.when(t == 0)
    def _():
        ubuf_ref[pad - tail:pad, :] = cbuf_ref[0]
        hcar_ref[...] = h0_ref[0]

    ubuf_ref[pad:pad + tt, :] = u_ref[...]
    uc = cb_ref[...]
    for k in range(cw):
        uc = uc + cw_ref[k:k + 1, :] * ubuf_ref[pad - tail + k:pad - tail + k + tt, :]
    ubuf_ref[pad - tail:pad, :] = ubuf_ref[pad + tt - tail:pad + tt, :]

    width = uc.shape[1]
    blk = width // heads
    ucb = uc.astype(BF16)
    rs, igs = [], []
    for h in range(heads):
        uh = ucb[:, h * blk:(h + 1) * blk]
        rs.append(jnp.dot(uh, gaw_ref[h], preferred_element_type=F32))
        igs.append(jnp.dot(uh, gxw_ref[h], preferred_element_type=F32))
    r = jax.nn.sigmoid(jnp.concatenate(rs, axis=1) + gab_ref[...])
    ig = jax.nn.sigmoid(jnp.concatenate(igs, axis=1) + gxb_ref[...])
    nlam = -lam_ref[...]
    softplus = jnp.maximum(nlam, 0.0) + jnp.log1p(jnp.exp(-jnp.abs(nlam)))
    log_a = -LRU_C * r * softplus
    a = jnp.exp(log_a)
    a_ref[...] = a
    b_ref[...] = jnp.sqrt(-jnp.tanh(log_a) * (a * a + 1.0)) * (ig * uc)

    def step(i, h):
        h = a_ref[pl.ds(i, 1), :] * h + b_ref[pl.ds(i, 1), :]
        b_ref[pl.ds(i, 1), :] = h
        return h

    h_end = lax.fori_loop(0, tt, step, hcar_ref[...], unroll=8)
    hcar_ref[...] = h_end
    out_ref[...] = (b_ref[...] * jax.nn.gelu(gate_ref[...])).astype(out_ref.dtype)

    @pl.when(t == pl.num_programs(1) - 1)
    def _():
        hlast_ref[0] = h_end


def _lru(ug, conv_buf, h0, P, j, B, T, tt_pref=256):
    M = ug.shape[0]
    W = ug.shape[1] // 2
    heads = P['gaw'].shape[1]
    width = P['cw'].shape[1]
    tt = _tile(T, tt_pref)
    nt = T // tt
    row_u = lambda b, t: (b * nt + t, 0)
    row_g = lambda b, t: (b * nt + t, 1)
    names = ('cw', 'cb', 'gaw', 'gab', 'gxw', 'gxb', 'lam')
    return pl.pallas_call(
        functools.partial(_lru_kernel, tt=tt, cw=width, heads=heads),
        out_shape=(jax.ShapeDtypeStruct((M, W), BF16), jax.ShapeDtypeStruct((B, 1, W), F32)),
        grid=(B, nt),
        in_specs=[pl.BlockSpec((tt, W), row_u), pl.BlockSpec((tt, W), row_g),
                  pl.BlockSpec((1, width - 1, W), lambda b, t: (b, 0, 0)),
                  pl.BlockSpec((1, 1, W), lambda b, t: (b, 0, 0))] + [_sel(P[n], j) for n in names],
        out_specs=(pl.BlockSpec((tt, W), row_u), pl.BlockSpec((1, 1, W), lambda b, t: (b, 0, 0))),
        scratch_shapes=[pltpu.VMEM((SUBLANES + tt, W), F32), pltpu.VMEM((1, W), F32),
                        pltpu.VMEM((tt, W), F32), pltpu.VMEM((tt, W), F32)],
        compiler_params=_params("arbitrary", "arbitrary"),
        name="rg_lru",
    )(ug, ug, conv_buf, h0, *[P[n] for n in names])


def _q_prep_kernel(cqn_ref, wuq_ref, wukt_ref, cos_ref, sin_ref, qlat_ref, qpe_ref,
                   *, heads, nope, rope_d):
    q = jnp.dot(cqn_ref[...], wuq_ref[...], preferred_element_type=F32)
    qn = q[:, :heads * nope].astype(BF16)
    pe = _rope_lanes(q[:, heads * nope:], cos_ref[...], sin_ref[...])
    for h in range(heads):
        qlat_ref[0, h] = jnp.dot(qn[:, h * nope:(h + 1) * nope], wukt_ref[h],
                                 preferred_element_type=F32).astype(BF16)
        qpe_ref[0, h] = pe[:, h * rope_d:(h + 1) * rope_d].astype(BF16)


def _q_prep(cqn, P, j, cos_t, sin_t, B, T, tq_pref=512):
    wuq, wukt = P['wuq'], P['wukt']
    _, heads, nope, lat = wukt.shape
    rope_d = P['rope_d']
    q_lora = cqn.shape[1]
    tq = _tile(T, tq_pref)
    nt = T // tq
    n_pos_blocks = cos_t.shape[0] // tq
    pos = lambda b, t: ((b * nt + t) % n_pos_blocks, 0)
    return pl.pallas_call(
        functools.partial(_q_prep_kernel, heads=heads, nope=nope, rope_d=rope_d),
        out_shape=(jax.ShapeDtypeStruct((B, heads, T, lat), BF16),
                   jax.ShapeDtypeStruct((B, heads, T, rope_d), BF16)),
        grid=(B, nt),
        in_specs=[pl.BlockSpec((tq, q_lora), lambda b, t: (b * nt + t, 0)),
                  _sel(wuq, j), _sel(wukt, j),
                  pl.BlockSpec((tq, heads * rope_d), pos), pl.BlockSpec((tq, heads * rope_d), pos)],
        out_specs=(pl.BlockSpec((1, heads, tq, lat), lambda b, t: (b, 0, t, 0)),
                   pl.BlockSpec((1, heads, tq, rope_d), lambda b, t: (b, 0, t, 0))),
        compiler_params=_params("arbitrary", "arbitrary"),
        name="mla_q_prep",
    )(cqn, wuq, wukt, cos_t, sin_t)


def _mla_kernel(qlat_ref, qpe_ref, ckv_ref, kpe_ref, wuv_ref, o_ref, m_ref, l_ref, acc_ref,
                *, tq, tk, q_pos0, n_keys, scale):
    heads = qlat_ref.shape[1]
    vdim = wuv_ref.shape[2]
    qi = pl.program_id(1)
    q_lo = q_pos0 + qi * tq
    vis_all = jnp.minimum(((q_lo >> CHUNK_SHIFT) + 1) * CHUNK, n_keys)
    vis_any = jnp.minimum((((q_lo + tq - 1) >> CHUNK_SHIFT) + 1) * CHUNK, n_keys)
    n_full = vis_all // tk
    n_tot = (vis_any + tk - 1) // tk

    m_ref[...] = jnp.full_like(m_ref, NEG_BIG)
    l_ref[...] = jnp.zeros_like(l_ref)
    acc_ref[...] = jnp.zeros_like(acc_ref)

    def block(kj, carry, masked):
        k0 = pl.multiple_of(kj * tk, tk)
        kc = ckv_ref[0, pl.ds(k0, tk), :]
        kp = kpe_ref[0, pl.ds(k0, tk), :]
        if masked:
            kpos = k0 + lax.broadcasted_iota(jnp.int32, (tq, tk), 1)
            qpos = q_lo + lax.broadcasted_iota(jnp.int32, (tq, tk), 0)
            ok = ((kpos >> CHUNK_SHIFT) <= (qpos >> CHUNK_SHIFT)) & (kpos < n_keys)
        for h in range(heads):
            s = lax.dot_general(qlat_ref[0, h], kc, NT_DIMS, preferred_element_type=F32)
            s = s + lax.dot_general(qpe_ref[0, h], kp, NT_DIMS, preferred_element_type=F32)
            s = s * scale
            if masked:
                s = jnp.where(ok, s, NEG_BIG)
            m_old = m_ref[h]
            m_new = jnp.maximum(m_old, jnp.max(s, axis=1, keepdims=True))
            corr = jnp.exp(m_old - m_new)
            p = jnp.exp(s - m_new)
            l_ref[h] = corr * l_ref[h] + jnp.sum(p, axis=1, keepdims=True)
            acc_ref[h] = corr * acc_ref[h] + jnp.dot(p.astype(BF16), kc, preferred_element_type=F32)
            m_ref[h] = m_new
        return carry

    lax.fori_loop(0, n_full, functools.partial(block, masked=False), 0)
    lax.fori_loop(n_full, n_tot, functools.partial(block, masked=True), 0)
    for h in range(heads):
        o = (acc_ref[h] / l_ref[h]).astype(BF16)
        o_ref[:, h * vdim:(h + 1) * vdim] = jnp.dot(
            o, wuv_ref[h], preferred_element_type=F32).astype(o_ref.dtype)


def _mla_attention(qlat, qpe, ckv_all, kpe_all, wuv, j, q_pos0, n_keys, scale, tq_pref=512, tk_pref=512):
    B, heads, T, lat = qlat.shape
    rope_d = qpe.shape[3]
    vdim = wuv.shape[3]
    Tk = ckv_all.shape[1]
    tq = _tile(T, tq_pref)
    tk = _tile(Tk, tk_pref)
    nt = T // tq
    return pl.pallas_call(
        functools.partial(_mla_kernel, tq=tq, tk=tk, q_pos0=q_pos0, n_keys=n_keys, scale=scale),
        out_shape=jax.ShapeDtypeStruct((B * T, heads * vdim), BF16),
        grid=(B, nt),
        in_specs=[pl.BlockSpec((1, heads, tq, lat), lambda b, t: (b, 0, t, 0)),
                  pl.BlockSpec((1, heads, tq, rope_d), lambda b, t: (b, 0, t, 0)),
                  pl.BlockSpec((1, Tk, lat), lambda b, t: (b, 0, 0)),
                  pl.BlockSpec((1, Tk, rope_d), lambda b, t: (b, 0, 0)),
                  _sel(wuv, j)],
        out_specs=pl.BlockSpec((tq, heads * vdim), lambda b, t: (b * nt + t, 0)),
        scratch_shapes=[pltpu.VMEM((heads, tq, 1), F32), pltpu.VMEM((heads, tq, 1), F32),
                        pltpu.VMEM((heads, tq, lat), F32)],
        compiler_params=_params("arbitrary", "arbitrary"),
        name="mla_attention",
    )(qlat, qpe, ckv_all, kpe_all, wuv)


def _xattn_kernel(q_ref, k_ref, v_ref, o_ref, *, heads, scale):
    hd = q_ref.shape[1] // heads
    for h in range(heads):
        sl = slice(h * hd, (h + 1) * hd)
        q = q_ref[:, sl]
        k = k_ref[:, sl].astype(BF16)
        v = v_ref[:, sl].astype(BF16)
        s = lax.dot_general(q, k, NT_DIMS, preferred_element_type=F32) * scale
        p = jnp.exp(s - jnp.max(s, axis=1, keepdims=True))
        l = jnp.sum(p, axis=1, keepdims=True)
        o = jnp.dot(p.astype(BF16), v, preferred_element_type=F32) / l
        o_ref[:, sl] = o.astype(o_ref.dtype)


def _xattn(q, mem_k, mem_v, l, B, T, tq_pref=512):
    M, D = q.shape
    n_mem = mem_k.shape[2]
    tq = _tile(T, tq_pref)
    nt = T // tq
    scale = (D // MEM_HEADS) ** -0.5
    mem_spec = pl.BlockSpec((None, None, n_mem, D), lambda b, t: (l, b, 0, 0))
    return pl.pallas_call(
        functools.partial(_xattn_kernel, heads=MEM_HEADS, scale=scale),
        out_shape=jax.ShapeDtypeStruct((M, D), BF16),
        grid=(B, nt),
        in_specs=[pl.BlockSpec((tq, D), lambda b, t: (b * nt + t, 0)), mem_spec, mem_spec],
        out_specs=pl.BlockSpec((tq, D), lambda b, t: (b * nt + t, 0)),
        compiler_params=_params("arbitrary", "arbitrary"),
        name="mem_xattn",
    )(q, mem_k, mem_v)


def _pool_ln_kernel(x_ref, pbuf_ref, pw_ref, ps_ref, g_ref, b_ref, o_ref, full_ref, y_ref,
                    *, tt, pos0, alpha):
    t = pl.program_id(1)
    halo = 2 * SUBLANES
    groups = pw_ref.shape[0]
    gw = pw_ref.shape[1]

    @pl.when(t == 0)
    def _():
        full_ref[0:halo, :] = pbuf_ref[0]

    full_ref[halo:halo + tt, :] = x_ref[...]
    pos = pos0 + t * tt + lax.broadcasted_iota(jnp.int32, (tt, 1), 0)
    for g in range(groups):
        w = POOL_WINDOWS[g]
        sl = slice(g * gw, (g + 1) * gw)
        f = full_ref[:, sl]
        s = f
        d = 1
        while d < w:
            s = s + pltpu.roll(s, d, 0)
            d *= 2
        cnt = jnp.minimum(pos + 1, w).astype(F32)
        dlt = (s[halo:] / cnt - f[halo:]).astype(BF16)
        y_ref[:, sl] = jnp.dot(dlt, pw_ref[g], preferred_element_type=F32)
    full_ref[0:halo, :] = full_ref[tt:tt + halo, :]
    o_ref[...] = _layer_norm(alpha * x_ref[...] + y_ref[...] * ps_ref[...], g_ref[...], b_ref[...])


def _pool_ln(x, pbuf16, P, j, ln_i, alpha, B, T, pos0, tt_pref=512):
    M, D = x.shape
    tt = _tile(T, tt_pref)
    nt = T // tt
    halo = 2 * SUBLANES
    assert max(POOL_WINDOWS) <= halo and tt >= halo
    row = lambda bb, t: (bb * nt + t, 0)
    return pl.pallas_call(
        functools.partial(_pool_ln_kernel, tt=tt, pos0=pos0, alpha=alpha),
        out_shape=jax.ShapeDtypeStruct((M, D), F32),
        grid=(B, nt),
        in_specs=[pl.BlockSpec((tt, D), row),
                  pl.BlockSpec((1, halo, D), lambda bb, t: (bb, 0, 0)),
                  _sel(P['pw'], j), _sel(P['ps'], j), _sel(P['ln_g'], ln_i), _sel(P['ln_b'], ln_i)],
        out_specs=pl.BlockSpec((tt, D), row),
        scratch_shapes=[pltpu.VMEM((halo + tt, D), F32), pltpu.VMEM((tt, D), F32)],
        compiler_params=_params("arbitrary", "arbitrary"),
        name="pool_res_ln",
    )(x, pbuf16, P['pw'], P['ps'], P['ln_g'], P['ln_b'])


def _rope_tables(pos, rope_d, heads):
    half = rope_d // 2
    inv = ROPE_THETA ** (-jnp.arange(half, dtype=F32) / half)
    ang = pos.astype(F32)[:, None] * inv[None, :]
    cos, sin = jnp.cos(ang), jnp.sin(ang)
    cos_t = jnp.tile(jnp.concatenate([cos, cos], axis=1), (1, heads))
    sin_t = jnp.tile(jnp.concatenate([-sin, sin], axis=1), (1, heads))
    return cos_t, sin_t


def _prep_weights(w_in, q_norm_g, w_uq, kv_norm_g, w_uk, w_uv, conv_w, conv_b,
                  gate_a_w, gate_a_b, gate_x_w, gate_x_b, lru_lambda, w_out,
                  pool_w, pool_scale, xa_wq, xa_wo, mlp_up, mlp_down, ln_g, ln_b):
    n_even = w_in.shape[0]
    lru_w = conv_w.shape[2]
    d_model = ln_g.shape[2]
    q_lora = q_norm_g.shape[1]
    kv_lora = kv_norm_g.shape[1]
    heads, nope = w_uk.shape[2], w_uk.shape[3]
    rope_d = w_uq.shape[2] // heads - nope
    ug_cols = 2 * lru_w
    pe_pad = (-rope_d) % LANES
    wa = w_in[:, :, :ug_cols].astype(BF16)
    wb = jnp.pad(w_in[:, :, ug_cols:], ((0, 0), (0, 0), (0, pe_pad))).astype(BF16)
    wuq = w_uq.reshape(n_even, q_lora, heads, nope + rope_d)
    wuq = jnp.concatenate([wuq[..., :nope].reshape(n_even, q_lora, heads * nope),
                           wuq[..., nope:].reshape(n_even, q_lora, heads * rope_d)], axis=2).astype(BF16)
    wukt = jnp.transpose(w_uk, (0, 2, 3, 1)).astype(BF16)
    wuv = jnp.transpose(w_uv, (0, 2, 1, 3)).astype(BF16)
    row = lambda a: a.reshape(a.shape[0], 1, -1)
    return dict(
        wa=wa, wb=wb, qg=row(q_norm_g), kg=row(kv_norm_g), wuq=wuq, wukt=wukt, wuv=wuv,
        cw=conv_w, cb=row(conv_b), gaw=gate_a_w.astype(BF16), gab=row(gate_a_b),
        gxw=gate_x_w.astype(BF16), gxb=row(gate_x_b), lam=row(lru_lambda),
        w_out=w_out.astype(BF16), pw=pool_w.astype(BF16), ps=row(pool_scale),
        wq=xa_wq.astype(BF16), wo=xa_wo.astype(BF16), up=mlp_up.astype(BF16), down=mlp_down.astype(BF16),
        ln_g=ln_g.reshape(-1, 1, d_model), ln_b=ln_b.reshape(-1, 1, d_model),
        q_lora=q_lora, kv_lora=kv_lora, heads=heads, nope=nope, rope_d=rope_d)


def _trunk(x3, t_past, mem_k, mem_v, conv_buf, h0, ckv_past, kpe_past, pool_buf, P):
    B, T, D = x3.shape
    n_norms = 3
    depth = P['ln_g'].shape[0] // n_norms
    alpha = (2.0 * depth) ** 0.25
    heads, rope_d = P['heads'], P['rope_d']
    x = x3.reshape(B * T, D)
    pos = t_past + jnp.arange(T)
    cos_t, sin_t = _rope_tables(pos, rope_d, heads)
    tm_in = _tile(T, IN_PROJ_ROWS) if T >= IN_PROJ_ROWS else B * T
    reps = max(1, tm_in // T)
    cos_rows, sin_rows = jnp.tile(cos_t, (reps, 1)), jnp.tile(sin_t, (reps, 1))
    scale = (P['nope'] + rope_d) ** -0.5
    ln_g, ln_b = P['ln_g'], P['ln_b']
    convs, hs, ckvs, kpes, pools = [], [], [], [], []
    for l in range(depth):
        j = l // 2
        if l % 2 == 0:
            ug, cqn, ckv, kpe = _in_proj(x, P, j, cos_rows[:, :LANES], sin_rows[:, :LANES], tm_in)
            lru_w = ug.shape[1] // 2
            rec, h_last = _lru(ug, conv_buf[j], h0[j][:, None, :], P, j, B, T)
            qlat, qpe = _q_prep(cqn, P, j, cos_t, sin_t, B, T)
            ckv3 = ckv.reshape(B, T, -1)
            kpe3 = kpe.reshape(B, T, -1)
            n_keys = ckv_past[j].shape[1] + T
            tk = 512 if n_keys % 512 == 0 else 256
            padk = (-n_keys) % tk
            ckv_all = jnp.concatenate([ckv_past[j].astype(BF16), ckv3.astype(BF16),
                                       jnp.zeros((B, padk, ckv3.shape[2]), BF16)], axis=1)
            kpe_all = jnp.concatenate([kpe_past[j].astype(BF16), kpe3.astype(BF16),
                                       jnp.zeros((B, padk, kpe3.shape[2]), BF16)], axis=1)
            attn = _mla_attention(qlat, qpe, ckv_all, kpe_all, P['wuv'], j, t_past, n_keys, scale,
                                  tk_pref=tk)
            x = _mm_ln([rec, attn], P['w_out'], j, x, ln_g, ln_b, n_norms * l, alpha)
            tail = P['cw'].shape[1] - 1
            convs.append(ug.reshape(B, T, -1)[:, T - tail:, :lru_w])
            hs.append(h_last[:, 0, :])
            ckvs.append(ckv3)
            kpes.append(kpe3)
        else:
            nbuf = pool_buf[j].shape[1]
            pools.append(x.reshape(B, T, D)[:, T - nbuf:])
            pbuf16 = jnp.pad(pool_buf[j], ((0, 0), (2 * SUBLANES - nbuf, 0), (0, 0)))
            x = _pool_ln(x, pbuf16, P, j, n_norms * l, alpha, B, T, t_past)
        q = _rows_matmul(x, P['wq'], BF16, g0=l, n_g=1)[0]
        o = _xattn(q, mem_k, mem_v, l, B, T)
        x = _mm_ln([o], P['wo'], l, x, ln_g, ln_b, n_norms * l + 1, alpha)
        x = _mlp_ln(x, P['up'], P['down'], l, ln_g, ln_b, n_norms * l + 2, alpha)
    return (x.reshape(B, T, D), jnp.stack(convs), jnp.stack(hs), jnp.stack(ckvs), jnp.stack(kpes),
            jnp.stack(pools))


def kernel(x_prompt, x_sample, mem_prompt, cache_mem_k, cache_mem_v, cache_mla_ckv, cache_mla_kpe,
           state_rglru_h, state_rglru_conv, state_pool,
           w_in, q_norm_g, w_uq, kv_norm_g, w_uk, w_uv, conv_w, conv_b,
           gate_a_w, gate_a_b, gate_x_w, gate_x_b, lru_lambda, w_out,
           pool_w, pool_scale, xa_wq, xa_wk, xa_wv, xa_wo, mlp_up, mlp_down, ln_g, ln_b):
    P = _prep_weights(w_in, q_norm_g, w_uq, kv_norm_g, w_uk, w_uv, conv_w, conv_b,
                      gate_a_w, gate_a_b, gate_x_w, gate_x_b, lru_lambda, w_out,
                      pool_w, pool_scale, xa_wq, xa_wo, mlp_up, mlp_down, ln_g, ln_b)
    depth = ln_g.shape[0]
    n_even, n_odd = w_in.shape[0], pool_w.shape[0]
    bp, _, d_model = x_prompt.shape
    n_mem = mem_prompt.shape[1]
    dt = x_prompt.dtype
    lru_w = conv_w.shape[2]
    mem_rows = mem_prompt.reshape(bp * n_mem, d_model)
    p_mem_k = _rows_matmul(mem_rows, xa_wk.astype(BF16), F32).reshape(depth, bp, n_mem, d_model)
    p_mem_v = _rows_matmul(mem_rows, xa_wv.astype(BF16), F32).reshape(depth, bp, n_mem, d_model)
    y_prompt, p_conv, p_h, p_ckv, p_kpe, p_pool = _trunk(
        x_prompt, 0, p_mem_k, p_mem_v,
        jnp.zeros((n_even, bp, conv_w.shape[1] - 1, lru_w), dt),
        jnp.zeros((n_even, bp, lru_w), dt),
        jnp.zeros((n_even, bp, 0, kv_norm_g.shape[1]), dt),
        jnp.zeros((n_even, bp, 0, cache_mla_kpe.shape[3]), dt),
        jnp.zeros((n_odd, bp, state_pool.shape[2], d_model), dt), P)
    y_sample, s_conv, s_h, s_ckv, s_kpe, s_pool = _trunk(
        x_sample, cache_mla_ckv.shape[2], cache_mem_k, cache_mem_v,
        state_rglru_conv, state_rglru_h, cache_mla_ckv, cache_mla_kpe, state_pool, P)
    return (y_prompt, y_sample, p_conv, p_h, p_ckv, p_kpe, p_pool, p_mem_k, p_mem_v,
            s_conv, s_h, s_ckv, s_kpe, s_pool)
```

```python
import functools

import jax
import jax.numpy as jnp
from jax import lax
from jax.experimental import pallas as pl
from jax.experimental.pallas import tpu as pltpu

F32 = jnp.float32
BF16 = jnp.bfloat16

CHUNK = 64
CHUNK_SHIFT = 6
assert 1 << CHUNK_SHIFT == CHUNK
MEM_HEADS = 4
POOL_WINDOWS = (2, 4, 8, 16)
LRU_C = 8.0
ROPE_THETA = 10000.0
LN_EPS = 1e-5
RMS_EPS = 1e-6
NEG_BIG = -1e30

V7X_VMEM_BYTES = 64 * 1024 * 1024
VMEM_LIMIT = V7X_VMEM_BYTES - 8 * 1024 * 1024
LANES = 128
SUBLANES = 8
IN_PROJ_ROWS = 512
MLA_STACK_ROWS = 512
XATTN_FUSE_ROWS = 256
NT_DIMS = (((1,), (1,)), ((), ()))


def _params(*sem):
    return pltpu.CompilerParams(dimension_semantics=sem, vmem_limit_bytes=VMEM_LIMIT)


def _tile(n, pref):
    if n <= pref:
        return n
    t = pref
    while n % t:
        t -= SUBLANES
    assert t > 0
    return t


def _sel(arr, *lead, single=False):
    n_lead = len(lead)
    rest = arr.shape[n_lead:]
    idx = tuple(lead) + (0,) * len(rest)
    mode = dict(pipeline_mode=pl.Buffered(1)) if single else {}
    return pl.BlockSpec((None,) * n_lead + tuple(rest), lambda *_: idx, **mode)


def _layer_norm(y, g, b):
    mu = jnp.mean(y, axis=-1, keepdims=True)
    d = y - mu
    var = jnp.mean(d * d, axis=-1, keepdims=True)
    return d * lax.rsqrt(var + LN_EPS) * g + b


def _rms_norm(x, g):
    return x * lax.rsqrt(jnp.mean(x * x, axis=-1, keepdims=True) + RMS_EPS) * g


def _rope_lanes(x, cos_t, sin_t):
    n = x.shape[-1]
    half = 32
    lane = lax.broadcasted_iota(jnp.int32, x.shape, x.ndim - 1)
    first = (lane % (2 * half)) < half
    swapped = jnp.where(first, pltpu.roll(x, n - half, x.ndim - 1), pltpu.roll(x, half, x.ndim - 1))
    return x * cos_t + swapped * sin_t


def _rows_matmul_kernel(a_ref, w_ref, o_ref):
    a = a_ref[...].astype(BF16)
    w = w_ref[...].astype(BF16)
    o_ref[...] = jnp.dot(a, w, preferred_element_type=F32).astype(o_ref.dtype)


def _rows_matmul(a, w, out_dtype, g0=0, n_g=None, tm_pref=512, tn_pref=None):
    M, K = a.shape
    N = w.shape[2]
    n_g = w.shape[0] if n_g is None else n_g
    tm = _tile(M, tm_pref)
    tn = N if tn_pref is None else _tile(N, tn_pref)
    return pl.pallas_call(
        _rows_matmul_kernel,
        out_shape=jax.ShapeDtypeStruct((n_g, M, N), out_dtype),
        grid=(n_g, N // tn, M // tm),
        in_specs=[pl.BlockSpec((tm, K), lambda g, n, i: (i, 0)),
                  pl.BlockSpec((None, K, tn), lambda g, n, i: (g0 + g, 0, n))],
        out_specs=pl.BlockSpec((None, tm, tn), lambda g, n, i: (g, i, n)),
        compiler_params=_params("arbitrary", "arbitrary", "arbitrary"),
        name="rows_matmul",
    )(a, w)


def _mm_ln_kernel(*refs, n_a, alpha, sub):
    a_refs = refs[:n_a]
    w_ref, res_ref, g_ref, b_ref, o_ref = refs[n_a:]
    for r0 in range(0, o_ref.shape[0], sub):
        rows = slice(r0, r0 + sub)
        y = None
        off = 0
        for a_ref in a_refs:
            k = a_ref.shape[1]
            part = jnp.dot(a_ref[rows, :].astype(BF16), w_ref[off:off + k, :], preferred_element_type=F32)
            y = part if y is None else y + part
            off += k
        o_ref[rows, :] = _layer_norm(alpha * res_ref[rows, :] + y, g_ref[...], b_ref[...])


def _mm_ln(a_list, w, wl, res, ln_g, ln_b, ln_i, alpha, tm_pref=512, sub_pref=256):
    M, N = res.shape
    tm = _tile(M, tm_pref)
    sub = _tile(tm, sub_pref)
    in_specs = [pl.BlockSpec((tm, a.shape[1]), lambda i: (i, 0)) for a in a_list]
    in_specs += [_sel(w, wl, single=True), pl.BlockSpec((tm, N), lambda i: (i, 0)),
                 _sel(ln_g, ln_i), _sel(ln_b, ln_i)]
    return pl.pallas_call(
        functools.partial(_mm_ln_kernel, n_a=len(a_list), alpha=alpha, sub=sub),
        out_shape=jax.ShapeDtypeStruct((M, N), F32),
        grid=(M // tm,),
        in_specs=in_specs,
        out_specs=pl.BlockSpec((tm, N), lambda i: (i, 0)),
        compiler_params=_params("arbitrary"),
        name="matmul_res_ln",
    )(*a_list, w, res, ln_g, ln_b)


def _mlp_ln_kernel(x_ref, wu_ref, wd_ref, g_ref, b_ref, o_ref, xb_ref, acc_ref, *, alpha, sub):
    j = pl.program_id(1)

    @pl.when(j == 0)
    def _():
        xb_ref[...] = x_ref[...].astype(BF16)
        acc_ref[...] = jnp.zeros_like(acc_ref)

    last = pl.num_programs(1) - 1

    def hidden():
        h = jnp.dot(xb_ref[...], wu_ref[...], preferred_element_type=F32)
        return jnp.square(jnp.maximum(h, 0.0)).astype(BF16)

    @pl.when(j < last)
    def _():
        acc_ref[...] += jnp.dot(hidden(), wd_ref[...], preferred_element_type=F32)

    @pl.when(j == last)
    def _():
        h = hidden()
        for r0 in range(0, o_ref.shape[0], sub):
            rows = slice(r0, r0 + sub)
            y = acc_ref[rows, :] + jnp.dot(h[rows, :], wd_ref[...], preferred_element_type=F32)
            o_ref[rows, :] = _layer_norm(alpha * x_ref[rows, :] + y, g_ref[...], b_ref[...])


def _mlp_ln(x, w_up, w_down, l, ln_g, ln_b, ln_i, alpha, tm_pref=512, tf_pref=1024, sub_pref=256):
    M, D = x.shape
    FF = w_up.shape[2]
    tm = _tile(M, tm_pref)
    tf = _tile(FF, tf_pref)
    sub = _tile(tm, sub_pref)
    return pl.pallas_call(
        functools.partial(_mlp_ln_kernel, alpha=alpha, sub=sub),
        out_shape=jax.ShapeDtypeStruct((M, D), F32),
        grid=(M // tm, FF // tf),
        in_specs=[pl.BlockSpec((tm, D), lambda i, j: (i, 0)),
                  pl.BlockSpec((None, D, tf), lambda i, j: (l, 0, j)),
                  pl.BlockSpec((None, tf, D), lambda i, j: (l, j, 0)),
                  _sel(ln_g, ln_i), _sel(ln_b, ln_i)],
        out_specs=pl.BlockSpec((tm, D), lambda i, j: (i, 0)),
        scratch_shapes=[pltpu.VMEM((tm, D), BF16), pltpu.VMEM((tm, D), F32)],
        compiler_params=_params("arbitrary", "arbitrary"),
        name="mlp_res_ln",
    )(x, w_up, w_down, ln_g, ln_b)


def _in_proj_kernel(x_ref, wa_ref, wb_ref, qg_ref, kg_ref, cos_ref, sin_ref,
                    ug_ref, cqn_ref, ckv_ref, kpe_ref, *, q_lora, kv_lora, qk_rope):
    xb = x_ref[...].astype(BF16)
    ug_ref[...] = jnp.dot(xb, wa_ref[...], preferred_element_type=F32)
    z = jnp.dot(xb, wb_ref[...], preferred_element_type=F32)
    cqn_ref[...] = _rms_norm(z[:, :q_lora], qg_ref[...]).astype(BF16)
    ckv_ref[...] = _rms_norm(z[:, q_lora:q_lora + kv_lora], kg_ref[...])
    pe = z[:, q_lora + kv_lora:]
    kpe_ref[...] = _rope_lanes(pe, cos_ref[...], sin_ref[...])[:, :qk_rope]


def _in_proj(x, P, j, cos_t, sin_t, tm):
    M, D = x.shape
    wa, wb = P['wa'], P['wb']
    q_lora, kv_lora, qk_rope = P['q_lora'], P['kv_lora'], P['rope_d']
    assert M % tm == 0 and cos_t.shape[0] % tm == 0
    n_pos_blocks = cos_t.shape[0] // tm
    pe_w = wb.shape[2] - q_lora - kv_lora
    row = lambda i: (i, 0)
    pos = lambda i: (i % n_pos_blocks, 0)
    return pl.pallas_call(
        functools.partial(_in_proj_kernel, q_lora=q_lora, kv_lora=kv_lora, qk_rope=qk_rope),
        out_shape=(jax.ShapeDtypeStruct((M, wa.shape[2]), F32),
                   jax.ShapeDtypeStruct((M, q_lora), BF16),
                   jax.ShapeDtypeStruct((M, kv_lora), F32),
                   jax.ShapeDtypeStruct((M, qk_rope), F32)),
        grid=(M // tm,),
        in_specs=[pl.BlockSpec((tm, D), row), _sel(wa, j, single=True), _sel(wb, j, single=True),
                  _sel(P['qg'], j), _sel(P['kg'], j),
                  pl.BlockSpec((tm, pe_w), pos), pl.BlockSpec((tm, pe_w), pos)],
        out_specs=(pl.BlockSpec((tm, wa.shape[2]), row), pl.BlockSpec((tm, q_lora), row),
                   pl.BlockSpec((tm, kv_lora), row), pl.BlockSpec((tm, qk_rope), row)),
        compiler_params=_params("arbitrary"),
        name="in_proj",
    )(x, wa, wb, P['qg'], P['kg'], cos_t, sin_t)


def _lru_kernel(u_ref, gate_ref, cbuf_ref, h0_ref, cw_ref, cb_ref, gaw_ref, gab_ref,
                gxw_ref, gxb_ref, lam_ref, out_ref, hlast_ref,
                ubuf_ref, hcar_ref, a_ref, b_ref, *, tt, cw, heads):
    t = pl.program_id(1)
    pad = SUBLANES
    tail = cw - 1

    @pl.when(t == 0)
    def _():
        ubuf_ref[pad - tail:pad, :] = cbuf_ref[0]
        hcar_ref[...] = h0_ref[0]

    ubuf_ref[pad:pad + tt, :] = u_ref[...]
    uc = cb_ref[...]
    for k in range(cw):
        uc = uc + cw_ref[k:k + 1, :] * ubuf_ref[pad - tail + k:pad - tail + k + tt, :]
    ubuf_ref[pad - tail:pad, :] = ubuf_ref[pad + tt - tail:pad + tt, :]

    width = uc.shape[1]
    blk = width // heads
    ucb = uc.astype(BF16)
    rs, igs = [], []
    for h in range(heads):
        uh = ucb[:, h * blk:(h + 1) * blk]
        rs.append(jnp.dot(uh, gaw_ref[h], preferred_element_type=F32))
        igs.append(jnp.dot(uh, gxw_ref[h], preferred_element_type=F32))
    r = jax.nn.sigmoid(jnp.concatenate(rs, axis=1) + gab_ref[...])
    ig = jax.nn.sigmoid(jnp.concatenate(igs, axis=1) + gxb_ref[...])
    nlam = -lam_ref[...]
    softplus = jnp.maximum(nlam, 0.0) + jnp.log1p(jnp.exp(-jnp.abs(nlam)))
    log_a = -LRU_C * r * softplus
    a = jnp.exp(log_a)
    a_ref[...] = a
    b_ref[...] = jnp.sqrt(-jnp.tanh(log_a) * (a * a + 1.0)) * (ig * uc)

    def step(i, h):
        h = a_ref[pl.ds(i, 1), :] * h + b_ref[pl.ds(i, 1), :]
        b_ref[pl.ds(i, 1), :] = h
        return h

    h_end = lax.fori_loop(0, tt, step, hcar_ref[...], unroll=8)
    hcar_ref[...] = h_end
    out_ref[...] = (b_ref[...] * jax.nn.gelu(gate_ref[...])).astype(out_ref.dtype)

    @pl.when(t == pl.num_programs(1) - 1)
    def _():
        hlast_ref[0] = h_end


def _lru(ug, conv_buf, h0, P, j, B, T, tt_pref=256):
    M = ug.shape[0]
    W = ug.shape[1] // 2
    heads = P['gaw'].shape[1]
    width = P['cw'].shape[1]
    tt = _tile(T, tt_pref)
    assert tt % SUBLANES == 0
    nt = T // tt
    row_u = lambda b, t: (b * nt + t, 0)
    row_g = lambda b, t: (b * nt + t, 1)
    names = ('cw', 'cb', 'gaw', 'gab', 'gxw', 'gxb', 'lam')
    return pl.pallas_call(
        functools.partial(_lru_kernel, tt=tt, cw=width, heads=heads),
        out_shape=(jax.ShapeDtypeStruct((M, W), BF16), jax.ShapeDtypeStruct((B, 1, W), F32)),
        grid=(B, nt),
        in_specs=[pl.BlockSpec((tt, W), row_u), pl.BlockSpec((tt, W), row_g),
                  pl.BlockSpec((1, width - 1, W), lambda b, t: (b, 0, 0)),
                  pl.BlockSpec((1, 1, W), lambda b, t: (b, 0, 0))] + [_sel(P[n], j) for n in names],
        out_specs=(pl.BlockSpec((tt, W), row_u), pl.BlockSpec((1, 1, W), lambda b, t: (b, 0, 0))),
        scratch_shapes=[pltpu.VMEM((SUBLANES + tt, W), F32), pltpu.VMEM((1, W), F32),
                        pltpu.VMEM((tt, W), F32), pltpu.VMEM((tt, W), F32)],
        compiler_params=_params("arbitrary", "arbitrary"),
        name="rg_lru",
    )(ug, ug, conv_buf, h0, *[P[n] for n in names])


def _q_prep_kernel(cqn_ref, wuq_ref, wukt_ref, cos_ref, sin_ref, qlat_ref, qpe_ref,
                   *, heads, nope, rope_d):
    q = jnp.dot(cqn_ref[...], wuq_ref[...], preferred_element_type=F32)
    qn = q[:, :heads * nope].astype(BF16)
    pe = _rope_lanes(q[:, heads * nope:], cos_ref[...], sin_ref[...])
    for h in range(heads):
        qlat_ref[0, h] = jnp.dot(qn[:, h * nope:(h + 1) * nope], wukt_ref[h],
                                 preferred_element_type=F32).astype(BF16)
        qpe_ref[0, h] = pe[:, h * rope_d:(h + 1) * rope_d].astype(BF16)


def _q_prep(cqn, P, j, cos_t, sin_t, B, T, tq_pref=512):
    wuq, wukt = P['wuq'], P['wukt']
    _, heads, nope, lat = wukt.shape
    rope_d = P['rope_d']
    q_lora = cqn.shape[1]
    tq = _tile(T, tq_pref)
    nt = T // tq
    n_pos_blocks = cos_t.shape[0] // tq
    pos = lambda b, t: ((b * nt + t) % n_pos_blocks, 0)
    return pl.pallas_call(
        functools.partial(_q_prep_kernel, heads=heads, nope=nope, rope_d=rope_d),
        out_shape=(jax.ShapeDtypeStruct((B, heads, T, lat), BF16),
                   jax.ShapeDtypeStruct((B, heads, T, rope_d), BF16)),
        grid=(B, nt),
        in_specs=[pl.BlockSpec((tq, q_lora), lambda b, t: (b * nt + t, 0)),
                  _sel(wuq, j), _sel(wukt, j),
                  pl.BlockSpec((tq, heads * rope_d), pos), pl.BlockSpec((tq, heads * rope_d), pos)],
        out_specs=(pl.BlockSpec((1, heads, tq, lat), lambda b, t: (b, 0, t, 0)),
                   pl.BlockSpec((1, heads, tq, rope_d), lambda b, t: (b, 0, t, 0))),
        compiler_params=_params("arbitrary", "arbitrary"),
        name="mla_q_prep",
    )(cqn, wuq, wukt, cos_t, sin_t)


def _mla_kernel(qlat_ref, qpe_ref, ckv_ref, kpe_ref, wuv_ref, o_ref, m_ref, l_ref, acc_ref,
                *, tq, tk, hg, q_pos0, n_keys, scale):
    heads = qlat_ref.shape[1]
    lat = qlat_ref.shape[3]
    vdim = wuv_ref.shape[2]
    rows = hg * tq
    qi = pl.program_id(1)
    q_lo = q_pos0 + qi * tq
    vis_all = jnp.minimum(((q_lo >> CHUNK_SHIFT) + 1) * CHUNK, n_keys)
    vis_any = jnp.minimum((((q_lo + tq - 1) >> CHUNK_SHIFT) + 1) * CHUNK, n_keys)
    n_full = vis_all // tk
    n_tot = (vis_any + tk - 1) // tk

    m_ref[...] = jnp.full_like(m_ref, NEG_BIG)
    l_ref[...] = jnp.zeros_like(l_ref)
    acc_ref[...] = jnp.zeros_like(acc_ref)

    def block(kj, carry, masked):
        k0 = pl.multiple_of(kj * tk, tk)
        kc = ckv_ref[0, pl.ds(k0, tk), :]
        kp = kpe_ref[0, pl.ds(k0, tk), :]
        if masked:
            kpos = k0 + lax.broadcasted_iota(jnp.int32, (rows, tk), 1)
            qpos = q_lo + (lax.broadcasted_iota(jnp.int32, (rows, tk), 0) & (tq - 1))
            ok = ((kpos >> CHUNK_SHIFT) <= (qpos >> CHUNK_SHIFT)) & (kpos < n_keys)
        for g in range(heads // hg):
            hs = slice(g * hg, (g + 1) * hg)
            ql = qlat_ref[0, hs].reshape(rows, lat)
            qp = qpe_ref[0, hs].reshape(rows, qpe_ref.shape[3])
            s = lax.dot_general(ql, kc, NT_DIMS, preferred_element_type=F32)
            s = s + lax.dot_general(qp, kp, NT_DIMS, preferred_element_type=F32)
            s = s * scale
            if masked:
                s = jnp.where(ok, s, NEG_BIG)
            m_old = m_ref[g]
            m_new = jnp.maximum(m_old, jnp.max(s, axis=1, keepdims=True))
            corr = jnp.exp(m_old - m_new)
            p = jnp.exp(s - m_new)
            l_ref[g] = corr * l_ref[g] + jnp.sum(p, axis=1, keepdims=True)
            acc_ref[g] = corr * acc_ref[g] + jnp.dot(p.astype(BF16), kc, preferred_element_type=F32)
            m_ref[g] = m_new
        return carry

    lax.fori_loop(0, n_full, functools.partial(block, masked=False), 0)
    lax.fori_loop(n_full, n_tot, functools.partial(block, masked=True), 0)
    for h in range(heads):
        g, r0 = h // hg, (h % hg) * tq
        o = (acc_ref[g, r0:r0 + tq, :] / l_ref[g, r0:r0 + tq, :]).astype(BF16)
        o_ref[:, h * vdim:(h + 1) * vdim] = jnp.dot(
            o, wuv_ref[h], preferred_element_type=F32).astype(o_ref.dtype)


def _mla_attention(qlat, qpe, ckv_all, kpe_all, wuv, j, q_pos0, n_keys, scale, tq_pref=512, tk_pref=512):
    B, heads, T, lat = qlat.shape
    rope_d = qpe.shape[3]
    vdim = wuv.shape[3]
    Tk = ckv_all.shape[1]
    tq = _tile(T, tq_pref)
    tk = _tile(Tk, tk_pref)
    nt = T // tq
    hg = max(1, min(heads, MLA_STACK_ROWS // tq))
    assert heads % hg == 0 and tq & (tq - 1) == 0
    return pl.pallas_call(
        functools.partial(_mla_kernel, tq=tq, tk=tk, hg=hg, q_pos0=q_pos0, n_keys=n_keys, scale=scale),
        out_shape=jax.ShapeDtypeStruct((B * T, heads * vdim), BF16),
        grid=(B, nt),
        in_specs=[pl.BlockSpec((1, heads, tq, lat), lambda b, t: (b, 0, t, 0)),
                  pl.BlockSpec((1, heads, tq, rope_d), lambda b, t: (b, 0, t, 0)),
                  pl.BlockSpec((1, Tk, lat), lambda b, t: (b, 0, 0)),
                  pl.BlockSpec((1, Tk, rope_d), lambda b, t: (b, 0, 0)),
                  _sel(wuv, j)],
        out_specs=pl.BlockSpec((tq, heads * vdim), lambda b, t: (b * nt + t, 0)),
        scratch_shapes=[pltpu.VMEM((heads // hg, hg * tq, 1), F32), pltpu.VMEM((heads // hg, hg * tq, 1), F32),
                        pltpu.VMEM((heads // hg, hg * tq, lat), F32)],
        compiler_params=_params("arbitrary", "arbitrary"),
        name="mla_attention",
    )(qlat, qpe, ckv_all, kpe_all, wuv)


def _xattn_kernel(q_ref, k_ref, v_ref, o_ref, *, heads, scale):
    hd = q_ref.shape[1] // heads
    for h in range(heads):
        sl = slice(h * hd, (h + 1) * hd)
        q = q_ref[:, sl]
        k = k_ref[:, sl].astype(BF16)
        v = v_ref[:, sl].astype(BF16)
        s = lax.dot_general(q, k, NT_DIMS, preferred_element_type=F32) * scale
        p = jnp.exp(s - jnp.max(s, axis=1, keepdims=True))
        l = jnp.sum(p, axis=1, keepdims=True)
        o = jnp.dot(p.astype(BF16), v, preferred_element_type=F32) / l
        o_ref[:, sl] = o.astype(o_ref.dtype)


def _xattn(q, mem_k, mem_v, l, B, T, tq_pref=512):
    M, D = q.shape
    n_mem = mem_k.shape[2]
    tq = _tile(T, tq_pref)
    nt = T // tq
    scale = (D // MEM_HEADS) ** -0.5
    mem_spec = pl.BlockSpec((None, None, n_mem, D), lambda b, t: (l, b, 0, 0))
    return pl.pallas_call(
        functools.partial(_xattn_kernel, heads=MEM_HEADS, scale=scale),
        out_shape=jax.ShapeDtypeStruct((M, D), BF16),
        grid=(B, nt),
        in_specs=[pl.BlockSpec((tq, D), lambda b, t: (b * nt + t, 0)), mem_spec, mem_spec],
        out_specs=pl.BlockSpec((tq, D), lambda b, t: (b * nt + t, 0)),
        compiler_params=_params("arbitrary", "arbitrary"),
        name="mem_xattn",
    )(q, mem_k, mem_v)


def _xattn_ln_kernel(q_ref, k_ref, v_ref, wo_ref, res_ref, g_ref, b_ref, o_ref, *, heads, scale, alpha, sub):
    hd = q_ref.shape[1] // heads
    kb = k_ref[...].astype(BF16)
    vb = v_ref[...].astype(BF16)
    for r0 in range(0, o_ref.shape[0], sub):
        rows = slice(r0, r0 + sub)
        outs = []
        for h in range(heads):
            sl = slice(h * hd, (h + 1) * hd)
            s = lax.dot_general(q_ref[rows, sl], kb[:, sl], NT_DIMS, preferred_element_type=F32) * scale
            p = jnp.exp(s - jnp.max(s, axis=1, keepdims=True))
            l = jnp.sum(p, axis=1, keepdims=True)
            outs.append((jnp.dot(p.astype(BF16), vb[:, sl], preferred_element_type=F32) / l).astype(BF16))
        y = jnp.dot(jnp.concatenate(outs, axis=1), wo_ref[...], preferred_element_type=F32)
        o_ref[rows, :] = _layer_norm(alpha * res_ref[rows, :] + y, g_ref[...], b_ref[...])


def _xattn_ln(q, mem_k, mem_v, l, wo, res, ln_g, ln_b, ln_i, alpha, B, T, tq_pref=512, sub_pref=256):
    M, D = q.shape
    n_mem = mem_k.shape[2]
    tq = _tile(T, tq_pref)
    sub = _tile(tq, sub_pref)
    nt = T // tq
    scale = (D // MEM_HEADS) ** -0.5
    row = lambda b, t: (b * nt + t, 0)
    mem_spec = pl.BlockSpec((None, None, n_mem, D), lambda b, t: (l, b, 0, 0))
    return pl.pallas_call(
        functools.partial(_xattn_ln_kernel, heads=MEM_HEADS, scale=scale, alpha=alpha, sub=sub),
        out_shape=jax.ShapeDtypeStruct((M, D), F32),
        grid=(B, nt),
        in_specs=[pl.BlockSpec((tq, D), row), mem_spec, mem_spec, _sel(wo, l, single=True),
                  pl.BlockSpec((tq, D), row), _sel(ln_g, ln_i), _sel(ln_b, ln_i)],
        out_specs=pl.BlockSpec((tq, D), row),
        compiler_params=_params("arbitrary", "arbitrary"),
        name="mem_xattn_out_ln",
    )(q, mem_k, mem_v, wo, res, ln_g, ln_b)


def _pool_ln_kernel(x_ref, pbuf_ref, pw_ref, ps_ref, g_ref, b_ref, o_ref, full_ref, y_ref,
                    *, tt, pos0, alpha):
    t = pl.program_id(1)
    halo = 2 * SUBLANES
    groups = pw_ref.shape[0]
    gw = pw_ref.shape[1]

    @pl.when(t == 0)
    def _():
        full_ref[0:halo, :] = pbuf_ref[0]

    full_ref[halo:halo + tt, :] = x_ref[...]
    pos = pos0 + t * tt + lax.broadcasted_iota(jnp.int32, (tt, 1), 0)
    for g in range(groups):
        w = POOL_WINDOWS[g]
        sl = slice(g * gw, (g + 1) * gw)
        f = full_ref[:, sl]
        s = f
        d = 1
        while d < w:
            s = s + pltpu.roll(s, d, 0)
            d *= 2
        cnt = jnp.minimum(pos + 1, w).astype(F32)
        dlt = (s[halo:] / cnt - f[halo:]).astype(BF16)
        y_ref[:, sl] = jnp.dot(dlt, pw_ref[g], preferred_element_type=F32)
    full_ref[0:halo, :] = full_ref[tt:tt + halo, :]
    o_ref[...] = _layer_norm(alpha * x_ref[...] + y_ref[...] * ps_ref[...], g_ref[...], b_ref[...])


def _pool_ln(x, pbuf16, P, j, ln_i, alpha, B, T, pos0, tt_pref=512):
    M, D = x.shape
    tt = _tile(T, tt_pref)
    nt = T // tt
    halo = 2 * SUBLANES
    assert max(POOL_WINDOWS) <= halo and tt >= halo
    row = lambda bb, t: (bb * nt + t, 0)
    return pl.pallas_call(
        functools.partial(_pool_ln_kernel, tt=tt, pos0=pos0, alpha=alpha),
        out_shape=jax.ShapeDtypeStruct((M, D), F32),
        grid=(B, nt),
        in_specs=[pl.BlockSpec((tt, D), row),
                  pl.BlockSpec((1, halo, D), lambda bb, t: (bb, 0, 0)),
                  _sel(P['pw'], j), _sel(P['ps'], j), _sel(P['ln_g'], ln_i), _sel(P['ln_b'], ln_i)],
        out_specs=pl.BlockSpec((tt, D), row),
        scratch_shapes=[pltpu.VMEM((halo + tt, D), F32), pltpu.VMEM((tt, D), F32)],
        compiler_params=_params("arbitrary", "arbitrary"),
        name="pool_res_ln",
    )(x, pbuf16, P['pw'], P['ps'], P['ln_g'], P['ln_b'])


def _rope_tables(pos, rope_d, heads):
    half = rope_d // 2
    inv = ROPE_THETA ** (-jnp.arange(half, dtype=F32) / half)
    ang = pos.astype(F32)[:, None] * inv[None, :]
    cos, sin = jnp.cos(ang), jnp.sin(ang)
    cos_t = jnp.tile(jnp.concatenate([cos, cos], axis=1), (1, heads))
    sin_t = jnp.tile(jnp.concatenate([-sin, sin], axis=1), (1, heads))
    return cos_t, sin_t


def _prep_weights(w_in, q_norm_g, w_uq, kv_norm_g, w_uk, w_uv, conv_w, conv_b,
                  gate_a_w, gate_a_b, gate_x_w, gate_x_b, lru_lambda, w_out,
                  pool_w, pool_scale, xa_wq, xa_wo, mlp_up, mlp_down, ln_g, ln_b):
    n_even = w_in.shape[0]
    lru_w = conv_w.shape[2]
    d_model = ln_g.shape[2]
    q_lora = q_norm_g.shape[1]
    kv_lora = kv_norm_g.shape[1]
    heads, nope = w_uk.shape[2], w_uk.shape[3]
    rope_d = w_uq.shape[2] // heads - nope
    ug_cols = 2 * lru_w
    pe_pad = (-rope_d) % LANES
    wa = w_in[:, :, :ug_cols].astype(BF16)
    wb = jnp.pad(w_in[:, :, ug_cols:], ((0, 0), (0, 0), (0, pe_pad))).astype(BF16)
    wuq = w_uq.reshape(n_even, q_lora, heads, nope + rope_d)
    wuq = jnp.concatenate([wuq[..., :nope].reshape(n_even, q_lora, heads * nope),
                           wuq[..., nope:].reshape(n_even, q_lora, heads * rope_d)], axis=2).astype(BF16)
    wukt = jnp.transpose(w_uk, (0, 2, 3, 1)).astype(BF16)
    wuv = jnp.transpose(w_uv, (0, 2, 1, 3)).astype(BF16)
    row = lambda a: a.reshape(a.shape[0], 1, -1)
    return dict(
        wa=wa, wb=wb, qg=row(q_norm_g), kg=row(kv_norm_g), wuq=wuq, wukt=wukt, wuv=wuv,
        cw=conv_w, cb=row(conv_b), gaw=gate_a_w.astype(BF16), gab=row(gate_a_b),
        gxw=gate_x_w.astype(BF16), gxb=row(gate_x_b), lam=row(lru_lambda),
        w_out=w_out.astype(BF16), pw=pool_w.astype(BF16), ps=row(pool_scale),
        wq=xa_wq.astype(BF16), wo=xa_wo.astype(BF16), up=mlp_up.astype(BF16), down=mlp_down.astype(BF16),
        ln_g=ln_g.reshape(-1, 1, d_model), ln_b=ln_b.reshape(-1, 1, d_model),
        q_lora=q_lora, kv_lora=kv_lora, heads=heads, nope=nope, rope_d=rope_d)


def _trunk(x3, t_past, mem_k, mem_v, conv_buf, h0, ckv_past, kpe_past, pool_buf, P):
    B, T, D = x3.shape
    n_norms = 3
    depth = P['ln_g'].shape[0] // n_norms
    alpha = (2.0 * depth) ** 0.25
    heads, rope_d = P['heads'], P['rope_d']
    x = x3.reshape(B * T, D)
    pos = t_past + jnp.arange(T)
    cos_t, sin_t = _rope_tables(pos, rope_d, heads)
    tm_in = _tile(T, IN_PROJ_ROWS) if T >= IN_PROJ_ROWS else B * T
    reps = max(1, tm_in // T)
    cos_rows, sin_rows = jnp.tile(cos_t, (reps, 1)), jnp.tile(sin_t, (reps, 1))
    scale = (P['nope'] + rope_d) ** -0.5
    ln_g, ln_b = P['ln_g'], P['ln_b']
    convs, hs, ckvs, kpes, pools = [], [], [], [], []
    for l in range(depth):
        j = l // 2
        if l % 2 == 0:
            ug, cqn, ckv, kpe = _in_proj(x, P, j, cos_rows[:, :LANES], sin_rows[:, :LANES], tm_in)
            lru_w = ug.shape[1] // 2
            rec, h_last = _lru(ug, conv_buf[j], h0[j][:, None, :], P, j, B, T)
            qlat, qpe = _q_prep(cqn, P, j, cos_t, sin_t, B, T)
            ckv3 = ckv.reshape(B, T, -1)
            kpe3 = kpe.reshape(B, T, -1)
            n_keys = ckv_past[j].shape[1] + T
            tk = 512 if n_keys % 512 == 0 else 256
            padk = (-n_keys) % tk
            ckv_all = jnp.concatenate([ckv_past[j].astype(BF16), ckv3.astype(BF16),
                                       jnp.zeros((B, padk, ckv3.shape[2]), BF16)], axis=1)
            kpe_all = jnp.concatenate([kpe_past[j].astype(BF16), kpe3.astype(BF16),
                                       jnp.zeros((B, padk, kpe3.shape[2]), BF16)], axis=1)
            attn = _mla_attention(qlat, qpe, ckv_all, kpe_all, P['wuv'], j, t_past, n_keys, scale,
                                  tk_pref=tk)
            x = _mm_ln([rec, attn], P['w_out'], j, x, ln_g, ln_b, n_norms * l, alpha)
            tail = P['cw'].shape[1] - 1
            convs.append(ug.reshape(B, T, -1)[:, T - tail:, :lru_w])
            hs.append(h_last[:, 0, :])
            ckvs.append(ckv3)
            kpes.append(kpe3)
        else:
            nbuf = pool_buf[j].shape[1]
            pools.append(x.reshape(B, T, D)[:, T - nbuf:])
            pbuf16 = jnp.pad(pool_buf[j], ((0, 0), (2 * SUBLANES - nbuf, 0), (0, 0)))
            x = _pool_ln(x, pbuf16, P, j, n_norms * l, alpha, B, T, t_past)
        q = _rows_matmul(x, P['wq'], BF16, g0=l, n_g=1)[0]
        if T >= XATTN_FUSE_ROWS:
            x = _xattn_ln(q, mem_k, mem_v, l, P['wo'], x, ln_g, ln_b, n_norms * l + 1, alpha, B, T)
        else:
            o = _xattn(q, mem_k, mem_v, l, B, T)
            x = _mm_ln([o], P['wo'], l, x, ln_g, ln_b, n_norms * l + 1, alpha)
        x = _mlp_ln(x, P['up'], P['down'], l, ln_g, ln_b, n_norms * l + 2, alpha)
    return (x.reshape(B, T, D), jnp.stack(convs), jnp.stack(hs), jnp.stack(ckvs), jnp.stack(kpes),
            jnp.stack(pools))


def kernel(x_prompt, x_sample, mem_prompt, cache_mem_k, cache_mem_v, cache_mla_ckv, cache_mla_kpe,
           state_rglru_h, state_rglru_conv, state_pool,
           w_in, q_norm_g, w_uq, kv_norm_g, w_uk, w_uv, conv_w, conv_b,
           gate_a_w, gate_a_b, gate_x_w, gate_x_b, lru_lambda, w_out,
           pool_w, pool_scale, xa_wq, xa_wk, xa_wv, xa_wo, mlp_up, mlp_down, ln_g, ln_b):
    P = _prep_weights(w_in, q_norm_g, w_uq, kv_norm_g, w_uk, w_uv, conv_w, conv_b,
                      gate_a_w, gate_a_b, gate_x_w, gate_x_b, lru_lambda, w_out,
                      pool_w, pool_scale, xa_wq, xa_wo, mlp_up, mlp_down, ln_g, ln_b)
    depth = ln_g.shape[0]
    n_even, n_odd = w_in.shape[0], pool_w.shape[0]
    bp, _, d_model = x_prompt.shape
    n_mem = mem_prompt.shape[1]
    dt = x_prompt.dtype
    lru_w = conv_w.shape[2]
    mem_rows = mem_prompt.reshape(bp * n_mem, d_model)
    p_mem_k = _rows_matmul(mem_rows, xa_wk, F32, tn_pref=1024).reshape(depth, bp, n_mem, d_model)
    p_mem_v = _rows_matmul(mem_rows, xa_wv, F32, tn_pref=1024).reshape(depth, bp, n_mem, d_model)
    y_prompt, p_conv, p_h, p_ckv, p_kpe, p_pool = _trunk(
        x_prompt, 0, p_mem_k, p_mem_v,
        jnp.zeros((n_even, bp, conv_w.shape[1] - 1, lru_w), dt),
        jnp.zeros((n_even, bp, lru_w), dt),
        jnp.zeros((n_even, bp, 0, kv_norm_g.shape[1]), dt),
        jnp.zeros((n_even, bp, 0, cache_mla_kpe.shape[3]), dt),
        jnp.zeros((n_odd, bp, state_pool.shape[2], d_model), dt), P)
    y_sample, s_conv, s_h, s_ckv, s_kpe, s_pool = _trunk(
        x_sample, cache_mla_ckv.shape[2], cache_mem_k, cache_mem_v,
        state_rglru_conv, state_rglru_h, cache_mla_ckv, cache_mla_kpe, state_pool, P)
    return (y_prompt, y_sample, p_conv, p_h, p_ckv, p_kpe, p_pool, p_mem_k, p_mem_v,
            s_conv, s_h, s_ckv, s_kpe, s_pool)
```

```python
import functools

import jax
import jax.numpy as jnp
from jax import lax
from jax.experimental import pallas as pl
from jax.experimental.pallas import tpu as pltpu

F32 = jnp.float32
BF16 = jnp.bfloat16

CHUNK = 64
CHUNK_SHIFT = 6
assert 1 << CHUNK_SHIFT == CHUNK
MEM_HEADS = 4
POOL_WINDOWS = (2, 4, 8, 16)
LRU_C = 8.0
ROPE_THETA = 10000.0
LN_EPS = 1e-5
RMS_EPS = 1e-6
NEG_BIG = -1e30
LOG2_E = 1.4426950408889634

V7X_VMEM_BYTES = 64 * 1024 * 1024
VMEM_LIMIT = V7X_VMEM_BYTES - 8 * 1024 * 1024
LANES = 128
SUBLANES = 8
IN_PROJ_ROWS = 512
MLA_STACK_ROWS = 512
XATTN_FUSE_ROWS = 256
NT_DIMS = (((1,), (1,)), ((), ()))


def _params(*sem):
    return pltpu.CompilerParams(dimension_semantics=sem, vmem_limit_bytes=VMEM_LIMIT)


def _tile(n, pref):
    if n <= pref:
        return n
    t = pref
    while n % t:
        t -= SUBLANES
    assert t > 0
    return t


def _sel(arr, *lead, single=False):
    n_lead = len(lead)
    rest = arr.shape[n_lead:]
    idx = tuple(lead) + (0,) * len(rest)
    mode = dict(pipeline_mode=pl.Buffered(1)) if single else {}
    return pl.BlockSpec((None,) * n_lead + tuple(rest), lambda *_: idx, **mode)


def _layer_norm(y, g, b):
    mu = jnp.mean(y, axis=-1, keepdims=True)
    d = y - mu
    var = jnp.mean(d * d, axis=-1, keepdims=True)
    return d * lax.rsqrt(var + LN_EPS) * g + b


def _rms_norm(x, g):
    return x * lax.rsqrt(jnp.mean(x * x, axis=-1, keepdims=True) + RMS_EPS) * g


def _rope_lanes(x, cos_t, sin_t):
    n = x.shape[-1]
    half = 32
    lane = lax.broadcasted_iota(jnp.int32, x.shape, x.ndim - 1)
    first = (lane % (2 * half)) < half
    swapped = jnp.where(first, pltpu.roll(x, n - half, x.ndim - 1), pltpu.roll(x, half, x.ndim - 1))
    return x * cos_t + swapped * sin_t


def _rows_matmul_kernel(a_ref, w_ref, o_ref):
    a = a_ref[...].astype(BF16)
    w = w_ref[...].astype(BF16)
    o_ref[...] = jnp.dot(a, w, preferred_element_type=F32).astype(o_ref.dtype)


def _rows_matmul(a, w, out_dtype, g0=0, n_g=None, tm_pref=512, tn_pref=None):
    M, K = a.shape
    N = w.shape[2]
    n_g = w.shape[0] if n_g is None else n_g
    tm = _tile(M, tm_pref)
    tn = N if tn_pref is None else _tile(N, tn_pref)
    return pl.pallas_call(
        _rows_matmul_kernel,
        out_shape=jax.ShapeDtypeStruct((n_g, M, N), out_dtype),
        grid=(n_g, N // tn, M // tm),
        in_specs=[pl.BlockSpec((tm, K), lambda g, n, i: (i, 0)),
                  pl.BlockSpec((None, K, tn), lambda g, n, i: (g0 + g, 0, n))],
        out_specs=pl.BlockSpec((None, tm, tn), lambda g, n, i: (g, i, n)),
        compiler_params=_params("arbitrary", "arbitrary", "arbitrary"),
        name="rows_matmul",
    )(a, w)


def _mm_ln_kernel(*refs, n_a, alpha, sub):
    a_refs = refs[:n_a]
    w_ref, res_ref, g_ref, b_ref, o_ref = refs[n_a:]
    for r0 in range(0, o_ref.shape[0], sub):
        rows = slice(r0, r0 + sub)
        y = None
        off = 0
        for a_ref in a_refs:
            k = a_ref.shape[1]
            part = jnp.dot(a_ref[rows, :].astype(BF16), w_ref[off:off + k, :], preferred_element_type=F32)
            y = part if y is None else y + part
            off += k
        o_ref[rows, :] = _layer_norm(alpha * res_ref[rows, :] + y, g_ref[...], b_ref[...])


def _mm_ln(a_list, w, wl, res, ln_g, ln_b, ln_i, alpha, tm_pref=512, sub_pref=256):
    M, N = res.shape
    tm = _tile(M, tm_pref)
    sub = _tile(tm, sub_pref)
    in_specs = [pl.BlockSpec((tm, a.shape[1]), lambda i: (i, 0)) for a in a_list]
    in_specs += [_sel(w, wl, single=True), pl.BlockSpec((tm, N), lambda i: (i, 0)),
                 _sel(ln_g, ln_i), _sel(ln_b, ln_i)]
    return pl.pallas_call(
        functools.partial(_mm_ln_kernel, n_a=len(a_list), alpha=alpha, sub=sub),
        out_shape=jax.ShapeDtypeStruct((M, N), F32),
        grid=(M // tm,),
        in_specs=in_specs,
        out_specs=pl.BlockSpec((tm, N), lambda i: (i, 0)),
        compiler_params=_params("arbitrary"),
        name="matmul_res_ln",
    )(*a_list, w, res, ln_g, ln_b)


def _mlp_ln_kernel(x_ref, wu_ref, wd_ref, g_ref, b_ref, o_ref, *rest, alpha, sub, emit):
    if emit:
        wub_ref, wdb_ref, xb_ref, acc_ref = rest
    else:
        xb_ref, acc_ref = rest
    j = pl.program_id(1)
    last = pl.num_programs(1) - 1

    @pl.when(j == 0)
    def _():
        xb_ref[...] = x_ref[...].astype(BF16)
        acc_ref[...] = jnp.zeros_like(acc_ref)

    def step(final):
        wu = wu_ref[...].astype(BF16)
        wd = wd_ref[...].astype(BF16)
        if emit:
            wub_ref[...] = wu
            wdb_ref[...] = wd
        h = jnp.dot(xb_ref[...], wu, preferred_element_type=F32)
        h = jnp.square(jnp.maximum(h, 0.0)).astype(BF16)
        if not final:
            acc_ref[...] += jnp.dot(h, wd, preferred_element_type=F32)
            return
        for r0 in range(0, o_ref.shape[0], sub):
            rows = slice(r0, r0 + sub)
            y = acc_ref[rows, :] + jnp.dot(h[rows, :], wd, preferred_element_type=F32)
            o_ref[rows, :] = _layer_norm(alpha * x_ref[rows, :] + y, g_ref[...], b_ref[...])

    pl.when(j < last)(functools.partial(step, False))
    pl.when(j == last)(functools.partial(step, True))


def _mlp_ln(x, w_up, w_down, l, ln_g, ln_b, ln_i, alpha, emit=False, tm_pref=512, tf_pref=1024,
            sub_pref=256):
    M, D = x.shape
    FF = w_up.shape[-1]
    tm = _tile(M, tm_pref)
    tf = _tile(FF, tf_pref)
    sub = _tile(tm, sub_pref)
    out_shape = [jax.ShapeDtypeStruct((M, D), F32)]
    out_specs = [pl.BlockSpec((tm, D), lambda i, j: (i, 0))]
    if emit:
        assert M == tm
        w_specs = [pl.BlockSpec((None, D, tf), lambda i, j: (l, 0, j)),
                   pl.BlockSpec((None, tf, D), lambda i, j: (l, j, 0))]
        out_shape += [jax.ShapeDtypeStruct((D, FF), BF16), jax.ShapeDtypeStruct((FF, D), BF16)]
        out_specs += [pl.BlockSpec((D, tf), lambda i, j: (0, j)), pl.BlockSpec((tf, D), lambda i, j: (j, 0))]
    else:
        w_specs = [pl.BlockSpec((D, tf), lambda i, j: (0, j)), pl.BlockSpec((tf, D), lambda i, j: (j, 0))]
    out = pl.pallas_call(
        functools.partial(_mlp_ln_kernel, alpha=alpha, sub=sub, emit=emit),
        out_shape=out_shape,
        grid=(M // tm, FF // tf),
        in_specs=[pl.BlockSpec((tm, D), lambda i, j: (i, 0))] + w_specs + [_sel(ln_g, ln_i), _sel(ln_b, ln_i)],
        out_specs=out_specs,
        scratch_shapes=[pltpu.VMEM((tm, D), BF16), pltpu.VMEM((tm, D), F32)],
        compiler_params=_params("arbitrary", "arbitrary"),
        name="mlp_res_ln_round" if emit else "mlp_res_ln",
    )(x, w_up, w_down, ln_g, ln_b)
    return out if emit else out[0]


def _in_proj_kernel(x_ref, wa_ref, wb_ref, qg_ref, kg_ref, cos_ref, sin_ref,
                    ug_ref, cqn_ref, ckv_ref, kpe_ref, *, q_lora, kv_lora, qk_rope):
    xb = x_ref[...].astype(BF16)
    ug_ref[...] = jnp.dot(xb, wa_ref[...], preferred_element_type=F32)
    z = jnp.dot(xb, wb_ref[...], preferred_element_type=F32)
    cqn_ref[...] = _rms_norm(z[:, :q_lora], qg_ref[...]).astype(BF16)
    ckv_ref[...] = _rms_norm(z[:, q_lora:q_lora + kv_lora], kg_ref[...])
    pe = z[:, q_lora + kv_lora:]
    kpe_ref[...] = _rope_lanes(pe, cos_ref[...], sin_ref[...])[:, :qk_rope]


def _in_proj(x, P, j, cos_t, sin_t, tm):
    M, D = x.shape
    wa, wb = P['wa'], P['wb']
    q_lora, kv_lora, qk_rope = P['q_lora'], P['kv_lora'], P['rope_d']
    assert M % tm == 0 and cos_t.shape[0] % tm == 0
    n_pos_blocks = cos_t.shape[0] // tm
    pe_w = wb.shape[2] - q_lora - kv_lora
    row = lambda i: (i, 0)
    pos = lambda i: (i % n_pos_blocks, 0)
    return pl.pallas_call(
        functools.partial(_in_proj_kernel, q_lora=q_lora, kv_lora=kv_lora, qk_rope=qk_rope),
        out_shape=(jax.ShapeDtypeStruct((M, wa.shape[2]), F32),
                   jax.ShapeDtypeStruct((M, q_lora), BF16),
                   jax.ShapeDtypeStruct((M, kv_lora), F32),
                   jax.ShapeDtypeStruct((M, qk_rope), F32)),
        grid=(M // tm,),
        in_specs=[pl.BlockSpec((tm, D), row), _sel(wa, j, single=True), _sel(wb, j, single=True),
                  _sel(P['qg'], j), _sel(P['kg'], j),
                  pl.BlockSpec((tm, pe_w), pos), pl.BlockSpec((tm, pe_w), pos)],
        out_specs=(pl.BlockSpec((tm, wa.shape[2]), row), pl.BlockSpec((tm, q_lora), row),
                   pl.BlockSpec((tm, kv_lora), row), pl.BlockSpec((tm, qk_rope), row)),
        compiler_params=_params("arbitrary"),
        name="in_proj",
    )(x, wa, wb, P['qg'], P['kg'], cos_t, sin_t)


def _lru_kernel(u_ref, gate_ref, cbuf_ref, h0_ref, cw_ref, cb_ref, gaw_ref, gab_ref,
                gxw_ref, gxb_ref, lam_ref, out_ref, hlast_ref,
                ubuf_ref, hcar_ref, a_ref, b_ref, *, tt, cw, heads):
    t = pl.program_id(1)
    pad = SUBLANES
    tail = cw - 1

    @pl.when(t == 0)
    def _():
        ubuf_ref[pad - tail:pad, :] = cbuf_ref[0]
        hcar_ref[...] = h0_ref[0]

    ubuf_ref[pad:pad + tt, :] = u_ref[...]
    uc = cb_ref[...]
    for k in range(cw):
        uc = uc + cw_ref[k:k + 1, :] * ubuf_ref[pad - tail + k:pad - tail + k + tt, :]
    ubuf_ref[pad - tail:pad, :] = ubuf_ref[pad + tt - tail:pad + tt, :]

    width = uc.shape[1]
    blk = width // heads
    ucb = uc.astype(BF16)
    rs, igs = [], []
    for h in range(heads):
        uh = ucb[:, h * blk:(h + 1) * blk]
        rs.append(jnp.dot(uh, gaw_ref[h], preferred_element_type=F32))
        igs.append(jnp.dot(uh, gxw_ref[h], preferred_element_type=F32))
    r = jax.nn.sigmoid(jnp.concatenate(rs, axis=1) + gab_ref[...])
    ig = jax.nn.sigmoid(jnp.concatenate(igs, axis=1) + gxb_ref[...])
    nlam = -lam_ref[...]
    softplus = jnp.maximum(nlam, 0.0) + jnp.log1p(jnp.exp(-jnp.abs(nlam)))
    log_a = -LRU_C * r * softplus
    a = jnp.exp(log_a)
    a_ref[...] = a
    b_ref[...] = jnp.sqrt(-jnp.tanh(log_a) * (a * a + 1.0)) * (ig * uc)

    def step(i, h):
        h = a_ref[pl.ds(i, 1), :] * h + b_ref[pl.ds(i, 1), :]
        b_ref[pl.ds(i, 1), :] = h
        return h

    h_end = lax.fori_loop(0, tt, step, hcar_ref[...], unroll=8)
    hcar_ref[...] = h_end
    out_ref[...] = (b_ref[...] * jax.nn.gelu(gate_ref[...])).astype(out_ref.dtype)

    @pl.when(t == pl.num_programs(1) - 1)
    def _():
        hlast_ref[0] = h_end


def _lru(ug, conv_buf, h0, P, j, B, T, tt_pref=256):
    M = ug.shape[0]
    W = ug.shape[1] // 2
    heads = P['gaw'].shape[1]
    width = P['cw'].shape[1]
    tt = _tile(T, tt_pref)
    assert tt % SUBLANES == 0
    nt = T // tt
    row_u = lambda b, t: (b * nt + t, 0)
    row_g = lambda b, t: (b * nt + t, 1)
    names = ('cw', 'cb', 'gaw', 'gab', 'gxw', 'gxb', 'lam')
    return pl.pallas_call(
        functools.partial(_lru_kernel, tt=tt, cw=width, heads=heads),
        out_shape=(jax.ShapeDtypeStruct((M, W), BF16), jax.ShapeDtypeStruct((B, 1, W), F32)),
        grid=(B, nt),
        in_specs=[pl.BlockSpec((tt, W), row_u), pl.BlockSpec((tt, W), row_g),
                  pl.BlockSpec((1, width - 1, W), lambda b, t: (b, 0, 0)),
                  pl.BlockSpec((1, 1, W), lambda b, t: (b, 0, 0))] + [_sel(P[n], j) for n in names],
        out_specs=(pl.BlockSpec((tt, W), row_u), pl.BlockSpec((1, 1, W), lambda b, t: (b, 0, 0))),
        scratch_shapes=[pltpu.VMEM((SUBLANES + tt, W), F32), pltpu.VMEM((1, W), F32),
                        pltpu.VMEM((tt, W), F32), pltpu.VMEM((tt, W), F32)],
        compiler_params=_params("arbitrary", "arbitrary"),
        name="rg_lru",
    )(ug, ug, conv_buf, h0, *[P[n] for n in names])


def _q_prep_kernel(cqn_ref, wuq_ref, wukt_ref, cos_ref, sin_ref, qlat_ref, qpe_ref,
                   *, heads, nope, rope_d):
    q = jnp.dot(cqn_ref[...], wuq_ref[...], preferred_element_type=F32)
    qn = q[:, :heads * nope].astype(BF16)
    pe = _rope_lanes(q[:, heads * nope:], cos_ref[...], sin_ref[...])
    for h in range(heads):
        qlat_ref[0, h] = jnp.dot(qn[:, h * nope:(h + 1) * nope], wukt_ref[h],
                                 preferred_element_type=F32).astype(BF16)
        qpe_ref[0, h] = pe[:, h * rope_d:(h + 1) * rope_d].astype(BF16)


def _q_prep(cqn, P, j, cos_t, sin_t, B, T, tq_pref=512):
    wuq, wukt = P['wuq'], P['wukt']
    _, heads, nope, lat = wukt.shape
    rope_d = P['rope_d']
    q_lora = cqn.shape[1]
    tq = _tile(T, tq_pref)
    nt = T // tq
    n_pos_blocks = cos_t.shape[0] // tq
    pos = lambda b, t: ((b * nt + t) % n_pos_blocks, 0)
    return pl.pallas_call(
        functools.partial(_q_prep_kernel, heads=heads, nope=nope, rope_d=rope_d),
        out_shape=(jax.ShapeDtypeStruct((B, heads, T, lat), BF16),
                   jax.ShapeDtypeStruct((B, heads, T, rope_d), BF16)),
        grid=(B, nt),
        in_specs=[pl.BlockSpec((tq, q_lora), lambda b, t: (b * nt + t, 0)),
                  _sel(wuq, j), _sel(wukt, j),
                  pl.BlockSpec((tq, heads * rope_d), pos), pl.BlockSpec((tq, heads * rope_d), pos)],
        out_specs=(pl.BlockSpec((1, heads, tq, lat), lambda b, t: (b, 0, t, 0)),
                   pl.BlockSpec((1, heads, tq, rope_d), lambda b, t: (b, 0, t, 0))),
        compiler_params=_params("arbitrary", "arbitrary"),
        name="mla_q_prep",
    )(cqn, wuq, wukt, cos_t, sin_t)


def _mla_kernel(qlat_ref, qpe_ref, ckv_ref, kpe_ref, wuv_ref, o_ref, m_ref, l_ref, acc_ref, sa_ref, sb_ref,
                *, tq, tk, hg, q_pos0, n_keys, scale):
    heads = qlat_ref.shape[1]
    lat = qlat_ref.shape[3]
    vdim = wuv_ref.shape[2]
    rows = hg * tq
    n_groups = heads // hg
    assert n_groups == 1 or n_groups % 2 == 0
    qi = pl.program_id(1)
    q_lo = q_pos0 + qi * tq
    vis_all = jnp.minimum(((q_lo >> CHUNK_SHIFT) + 1) * CHUNK, n_keys)
    vis_any = jnp.minimum((((q_lo + tq - 1) >> CHUNK_SHIFT) + 1) * CHUNK, n_keys)
    n_full = vis_all // tk
    n_tot = (vis_any + tk - 1) // tk

    m_ref[...] = jnp.full_like(m_ref, NEG_BIG)
    l_ref[...] = jnp.zeros_like(l_ref)
    acc_ref[...] = jnp.zeros_like(acc_ref)

    def block(kj, carry, masked):
        k0 = pl.multiple_of(kj * tk, tk)
        kc = ckv_ref[0, pl.ds(k0, tk), :]
        kp = kpe_ref[0, pl.ds(k0, tk), :]
        if masked:
            kpos = k0 + lax.broadcasted_iota(jnp.int32, (rows, tk), 1)
            qpos = q_lo + (lax.broadcasted_iota(jnp.int32, (rows, tk), 0) & (tq - 1))
            ok = ((kpos >> CHUNK_SHIFT) <= (qpos >> CHUNK_SHIFT)) & (kpos < n_keys)
        def scores(g):
            hs = pl.ds(g * hg, hg)
            ql = qlat_ref[0, hs].reshape(rows, lat)
            qp = qpe_ref[0, hs].reshape(rows, qpe_ref.shape[3])
            s = lax.dot_general(ql, kc, NT_DIMS, preferred_element_type=F32)
            return s + lax.dot_general(qp, kp, NT_DIMS, preferred_element_type=F32)

        def update(g, s):
            if masked:
                s = jnp.where(ok, s, NEG_BIG)
            m_old = m_ref[g]
            m_new = jnp.maximum(m_old, jnp.max(s, axis=1, keepdims=True))
            corr = jnp.exp2((m_old - m_new) * (scale * LOG2_E))
            p = jnp.exp2((s - m_new) * (scale * LOG2_E))
            l_ref[g] = corr * l_ref[g] + jnp.sum(p, axis=1, keepdims=True)
            acc_ref[g] = corr * acc_ref[g] + jnp.dot(p.astype(BF16), kc, preferred_element_type=F32)
            m_ref[g] = m_new

        if n_groups == 1:
            update(0, scores(0))
            return carry

        sa_ref[...] = scores(0)

        def pair(i, c):
            g = 2 * i
            sb_ref[...] = scores(g + 1)
            update(g, sa_ref[...])
            sa_ref[...] = scores(jnp.minimum(g + 2, n_groups - 1))
            update(g + 1, sb_ref[...])
            return c

        lax.fori_loop(0, n_groups // 2, pair, 0)
        return carry

    lax.fori_loop(0, n_full, functools.partial(block, masked=False), 0)
    lax.fori_loop(n_full, n_tot, functools.partial(block, masked=True), 0)
    for h in range(heads):
        g, r0 = h // hg, (h % hg) * tq
        o = (acc_ref[g, r0:r0 + tq, :] / l_ref[g, r0:r0 + tq, :]).astype(BF16)
        o_ref[:, h * vdim:(h + 1) * vdim] = jnp.dot(
            o, wuv_ref[h], preferred_element_type=F32).astype(o_ref.dtype)


def _mla_attention(qlat, qpe, ckv_all, kpe_all, wuv, j, q_pos0, n_keys, scale, tq_pref=512, tk_pref=512):
    B, heads, T, lat = qlat.shape
    rope_d = qpe.shape[3]
    vdim = wuv.shape[3]
    Tk = ckv_all.shape[1]
    tq = _tile(T, tq_pref)
    tk = _tile(Tk, tk_pref)
    nt = T // tq
    hg = max(1, min(heads, MLA_STACK_ROWS // tq))
    assert heads % hg == 0 and tq & (tq - 1) == 0
    return pl.pallas_call(
        functools.partial(_mla_kernel, tq=tq, tk=tk, hg=hg, q_pos0=q_pos0, n_keys=n_keys, scale=scale),
        out_shape=jax.ShapeDtypeStruct((B * T, heads * vdim), BF16),
        grid=(B, nt),
        in_specs=[pl.BlockSpec((1, heads, tq, lat), lambda b, t: (b, 0, t, 0)),
                  pl.BlockSpec((1, heads, tq, rope_d), lambda b, t: (b, 0, t, 0)),
                  pl.BlockSpec((1, Tk, lat), lambda b, t: (b, 0, 0)),
                  pl.BlockSpec((1, Tk, rope_d), lambda b, t: (b, 0, 0)),
                  _sel(wuv, j)],
        out_specs=pl.BlockSpec((tq, heads * vdim), lambda b, t: (b * nt + t, 0)),
        scratch_shapes=[pltpu.VMEM((heads // hg, hg * tq, 1), F32), pltpu.VMEM((heads // hg, hg * tq, 1), F32),
                        pltpu.VMEM((heads // hg, hg * tq, lat), F32),
                        pltpu.VMEM((hg * tq, tk), F32), pltpu.VMEM((hg * tq, tk), F32)],
        compiler_params=_params("arbitrary", "arbitrary"),
        name="mla_attention",
    )(qlat, qpe, ckv_all, kpe_all, wuv)


def _xattn_kernel(q_ref, k_ref, v_ref, o_ref, *, heads, scale):
    hd = q_ref.shape[1] // heads
    for h in range(heads):
        sl = slice(h * hd, (h + 1) * hd)
        q = q_ref[:, sl]
        k = k_ref[:, sl].astype(BF16)
        v = v_ref[:, sl].astype(BF16)
        s = lax.dot_general(q, k, NT_DIMS, preferred_element_type=F32) * scale
        p = jnp.exp(s - jnp.max(s, axis=1, keepdims=True))
        l = jnp.sum(p, axis=1, keepdims=True)
        o = jnp.dot(p.astype(BF16), v, preferred_element_type=F32) / l
        o_ref[:, sl] = o.astype(o_ref.dtype)


def _xattn(q, mem_k, mem_v, l, B, T, tq_pref=512):
    M, D = q.shape
    n_mem = mem_k.shape[2]
    tq = _tile(T, tq_pref)
    nt = T // tq
    scale = (D // MEM_HEADS) ** -0.5
    mem_spec = pl.BlockSpec((None, None, n_mem, D), lambda b, t: (l, b, 0, 0))
    return pl.pallas_call(
        functools.partial(_xattn_kernel, heads=MEM_HEADS, scale=scale),
        out_shape=jax.ShapeDtypeStruct((M, D), BF16),
        grid=(B, nt),
        in_specs=[pl.BlockSpec((tq, D), lambda b, t: (b * nt + t, 0)), mem_spec, mem_spec],
        out_specs=pl.BlockSpec((tq, D), lambda b, t: (b * nt + t, 0)),
        compiler_params=_params("arbitrary", "arbitrary"),
        name="mem_xattn",
    )(q, mem_k, mem_v)


def _xattn_ln_kernel(q_ref, k_ref, v_ref, wo_ref, res_ref, g_ref, b_ref, o_ref, *, heads, scale, alpha, sub):
    hd = q_ref.shape[1] // heads
    kb = k_ref[...].astype(BF16)
    vb = v_ref[...].astype(BF16)
    for r0 in range(0, o_ref.shape[0], sub):
        rows = slice(r0, r0 + sub)
        outs = []
        for h in range(heads):
            sl = slice(h * hd, (h + 1) * hd)
            s = lax.dot_general(q_ref[rows, sl], kb[:, sl], NT_DIMS, preferred_element_type=F32) * scale
            p = jnp.exp(s - jnp.max(s, axis=1, keepdims=True))
            l = jnp.sum(p, axis=1, keepdims=True)
            outs.append((jnp.dot(p.astype(BF16), vb[:, sl], preferred_element_type=F32) / l).astype(BF16))
        y = jnp.dot(jnp.concatenate(outs, axis=1), wo_ref[...], preferred_element_type=F32)
        o_ref[rows, :] = _layer_norm(alpha * res_ref[rows, :] + y, g_ref[...], b_ref[...])


def _xattn_ln(q, mem_k, mem_v, l, wo, res, ln_g, ln_b, ln_i, alpha, B, T, tq_pref=512, sub_pref=256):
    M, D = q.shape
    n_mem = mem_k.shape[2]
    tq = _tile(T, tq_pref)
    sub = _tile(tq, sub_pref)
    nt = T // tq
    scale = (D // MEM_HEADS) ** -0.5
    row = lambda b, t: (b * nt + t, 0)
    mem_spec = pl.BlockSpec((None, None, n_mem, D), lambda b, t: (l, b, 0, 0))
    return pl.pallas_call(
        functools.partial(_xattn_ln_kernel, heads=MEM_HEADS, scale=scale, alpha=alpha, sub=sub),
        out_shape=jax.ShapeDtypeStruct((M, D), F32),
        grid=(B, nt),
        in_specs=[pl.BlockSpec((tq, D), row), mem_spec, mem_spec, _sel(wo, l, single=True),
                  pl.BlockSpec((tq, D), row), _sel(ln_g, ln_i), _sel(ln_b, ln_i)],
        out_specs=pl.BlockSpec((tq, D), row),
        compiler_params=_params("arbitrary", "arbitrary"),
        name="mem_xattn_out_ln",
    )(q, mem_k, mem_v, wo, res, ln_g, ln_b)


def _pool_ln_kernel(x_ref, pbuf_ref, pw_ref, ps_ref, g_ref, b_ref, o_ref, full_ref, y_ref,
                    *, tt, pos0, alpha):
    t = pl.program_id(1)
    halo = 2 * SUBLANES
    groups = pw_ref.shape[0]
    gw = pw_ref.shape[1]

    @pl.when(t == 0)
    def _():
        full_ref[0:halo, :] = pbuf_ref[0]

    full_ref[halo:halo + tt, :] = x_ref[...]
    pos = pos0 + t * tt + lax.broadcasted_iota(jnp.int32, (tt, 1), 0)
    for g in range(groups):
        w = POOL_WINDOWS[g]
        sl = slice(g * gw, (g + 1) * gw)
        f = full_ref[:, sl]
        s = f
        d = 1
        while d < w:
            s = s + pltpu.roll(s, d, 0)
            d *= 2
        cnt = jnp.minimum(pos + 1, w).astype(F32)
        dlt = (s[halo:] / cnt - f[halo:]).astype(BF16)
        y_ref[:, sl] = jnp.dot(dlt, pw_ref[g], preferred_element_type=F32)
    full_ref[0:halo, :] = full_ref[tt:tt + halo, :]
    o_ref[...] = _layer_norm(alpha * x_ref[...] + y_ref[...] * ps_ref[...], g_ref[...], b_ref[...])


def _pool_ln(x, pbuf16, P, j, ln_i, alpha, B, T, pos0, tt_pref=512):
    M, D = x.shape
    tt = _tile(T, tt_pref)
    nt = T // tt
    halo = 2 * SUBLANES
    assert max(POOL_WINDOWS) <= halo and tt >= halo
    row = lambda bb, t: (bb * nt + t, 0)
    return pl.pallas_call(
        functools.partial(_pool_ln_kernel, tt=tt, pos0=pos0, alpha=alpha),
        out_shape=jax.ShapeDtypeStruct((M, D), F32),
        grid=(B, nt),
        in_specs=[pl.BlockSpec((tt, D), row),
                  pl.BlockSpec((1, halo, D), lambda bb, t: (bb, 0, 0)),
                  _sel(P['pw'], j), _sel(P['ps'], j), _sel(P['ln_g'], ln_i), _sel(P['ln_b'], ln_i)],
        out_specs=pl.BlockSpec((tt, D), row),
        scratch_shapes=[pltpu.VMEM((halo + tt, D), F32), pltpu.VMEM((tt, D), F32)],
        compiler_params=_params("arbitrary", "arbitrary"),
        name="pool_res_ln",
    )(x, pbuf16, P['pw'], P['ps'], P['ln_g'], P['ln_b'])


def _rope_tables(pos, rope_d, heads):
    half = rope_d // 2
    inv = ROPE_THETA ** (-jnp.arange(half, dtype=F32) / half)
    ang = pos.astype(F32)[:, None] * inv[None, :]
    cos, sin = jnp.cos(ang), jnp.sin(ang)
    cos_t = jnp.tile(jnp.concatenate([cos, cos], axis=1), (1, heads))
    sin_t = jnp.tile(jnp.concatenate([-sin, sin], axis=1), (1, heads))
    return cos_t, sin_t


def _prep_weights(w_in, q_norm_g, w_uq, kv_norm_g, w_uk, w_uv, conv_w, conv_b,
                  gate_a_w, gate_a_b, gate_x_w, gate_x_b, lru_lambda, w_out,
                  pool_w, pool_scale, xa_wq, xa_wo, mlp_up, mlp_down, ln_g, ln_b):
    n_even = w_in.shape[0]
    lru_w = conv_w.shape[2]
    d_model = ln_g.shape[2]
    q_lora = q_norm_g.shape[1]
    kv_lora = kv_norm_g.shape[1]
    heads, nope = w_uk.shape[2], w_uk.shape[3]
    rope_d = w_uq.shape[2] // heads - nope
    ug_cols = 2 * lru_w
    pe_pad = (-rope_d) % LANES
    wa = w_in[:, :, :ug_cols].astype(BF16)
    wb = jnp.pad(w_in[:, :, ug_cols:], ((0, 0), (0, 0), (0, pe_pad))).astype(BF16)
    wuq = w_uq.reshape(n_even, q_lora, heads, nope + rope_d)
    wuq = jnp.concatenate([wuq[..., :nope].reshape(n_even, q_lora, heads * nope),
                           wuq[..., nope:].reshape(n_even, q_lora, heads * rope_d)], axis=2).astype(BF16)
    wukt = jnp.transpose(w_uk, (0, 2, 3, 1)).astype(BF16)
    wuv = jnp.transpose(w_uv, (0, 2, 1, 3)).astype(BF16)
    row = lambda a: a.reshape(a.shape[0], 1, -1)
    return dict(
        wa=wa, wb=wb, qg=row(q_norm_g), kg=row(kv_norm_g), wuq=wuq, wukt=wukt, wuv=wuv,
        cw=conv_w, cb=row(conv_b), gaw=gate_a_w.astype(BF16), gab=row(gate_a_b),
        gxw=gate_x_w.astype(BF16), gxb=row(gate_x_b), lam=row(lru_lambda),
        w_out=w_out.astype(BF16), pw=pool_w.astype(BF16), ps=row(pool_scale),
        wq=xa_wq.astype(BF16), wo=xa_wo.astype(BF16), up=mlp_up, down=mlp_down,
        ln_g=ln_g.reshape(-1, 1, d_model), ln_b=ln_b.reshape(-1, 1, d_model),
        q_lora=q_lora, kv_lora=kv_lora, heads=heads, nope=nope, rope_d=rope_d)


N_NORMS = 3


class _Group:
    def __init__(self, x3, t_past, mem_k, mem_v, conv_buf, h0, ckv_past, kpe_past, pool_buf, P):
        self.B, self.T, self.D = x3.shape
        B, T = self.B, self.T
        self.x = x3.reshape(B * T, self.D)
        self.t_past, self.mem_k, self.mem_v = t_past, mem_k, mem_v
        self.conv_buf, self.h0, self.ckv_past, self.kpe_past, self.pool_buf = (
            conv_buf, h0, ckv_past, kpe_past, pool_buf)
        self.cos_t, self.sin_t = _rope_tables(t_past + jnp.arange(T), P['rope_d'], P['heads'])
        self.tm_in = _tile(T, IN_PROJ_ROWS) if T >= IN_PROJ_ROWS else B * T
        reps = max(1, self.tm_in // T)
        self.cos_rows = jnp.tile(self.cos_t, (reps, 1))[:, :LANES]
        self.sin_rows = jnp.tile(self.sin_t, (reps, 1))[:, :LANES]
        self.convs, self.hs, self.ckvs, self.kpes, self.pools = [], [], [], [], []

    def pre_mlp(self, l, P, alpha):
        B, T, D = self.B, self.T, self.D
        x, t_past, mem_k, mem_v = self.x, self.t_past, self.mem_k, self.mem_v
        conv_buf, h0, ckv_past, kpe_past, pool_buf = (
            self.conv_buf, self.h0, self.ckv_past, self.kpe_past, self.pool_buf)
        cos_t, sin_t = self.cos_t, self.sin_t
        convs, hs, ckvs, kpes, pools = self.convs, self.hs, self.ckvs, self.kpes, self.pools
        n_norms = N_NORMS
        scale = (P['nope'] + P['rope_d']) ** -0.5
        ln_g, ln_b = P['ln_g'], P['ln_b']
        j = l // 2
        if l % 2 == 0:
            ug, cqn, ckv, kpe = _in_proj(x, P, j, self.cos_rows, self.sin_rows, self.tm_in)
            lru_w = ug.shape[1] // 2
            rec, h_last = _lru(ug, conv_buf[j], h0[j][:, None, :], P, j, B, T)
            qlat, qpe = _q_prep(cqn, P, j, cos_t, sin_t, B, T)
            ckv3 = ckv.reshape(B, T, -1)
            kpe3 = kpe.reshape(B, T, -1)
            n_keys = ckv_past[j].shape[1] + T
            tk = 512 if n_keys % 512 == 0 else 256
            padk = (-n_keys) % tk
            ckv_all = jnp.concatenate([ckv_past[j].astype(BF16), ckv3.astype(BF16),
                                       jnp.zeros((B, padk, ckv3.shape[2]), BF16)], axis=1)
            kpe_all = jnp.concatenate([kpe_past[j].astype(BF16), kpe3.astype(BF16),
                                       jnp.zeros((B, padk, kpe3.shape[2]), BF16)], axis=1)
            attn = _mla_attention(qlat, qpe, ckv_all, kpe_all, P['wuv'], j, t_past, n_keys, scale,
                                  tk_pref=tk)
            x = _mm_ln([rec, attn], P['w_out'], j, x, ln_g, ln_b, n_norms * l, alpha)
            tail = P['cw'].shape[1] - 1
            convs.append(ug.reshape(B, T, -1)[:, T - tail:, :lru_w])
            hs.append(h_last[:, 0, :])
            ckvs.append(ckv3)
            kpes.append(kpe3)
        else:
            nbuf = pool_buf[j].shape[1]
            pools.append(x.reshape(B, T, D)[:, T - nbuf:])
            pbuf16 = jnp.pad(pool_buf[j], ((0, 0), (2 * SUBLANES - nbuf, 0), (0, 0)))
            x = _pool_ln(x, pbuf16, P, j, n_norms * l, alpha, B, T, t_past)
        q = _rows_matmul(x, P['wq'], BF16, g0=l, n_g=1)[0]
        if T >= XATTN_FUSE_ROWS:
            x = _xattn_ln(q, mem_k, mem_v, l, P['wo'], x, ln_g, ln_b, n_norms * l + 1, alpha, B, T)
        else:
            o = _xattn(q, mem_k, mem_v, l, B, T)
            x = _mm_ln([o], P['wo'], l, x, ln_g, ln_b, n_norms * l + 1, alpha)
        self.x = x

    def results(self):
        return (self.x.reshape(self.B, self.T, self.D), jnp.stack(self.convs), jnp.stack(self.hs),
                jnp.stack(self.ckvs), jnp.stack(self.kpes), jnp.stack(self.pools))


def kernel(x_prompt, x_sample, mem_prompt, cache_mem_k, cache_mem_v, cache_mla_ckv, cache_mla_kpe,
           state_rglru_h, state_rglru_conv, state_pool,
           w_in, q_norm_g, w_uq, kv_norm_g, w_uk, w_uv, conv_w, conv_b,
           gate_a_w, gate_a_b, gate_x_w, gate_x_b, lru_lambda, w_out,
           pool_w, pool_scale, xa_wq, xa_wk, xa_wv, xa_wo, mlp_up, mlp_down, ln_g, ln_b):
    P = _prep_weights(w_in, q_norm_g, w_uq, kv_norm_g, w_uk, w_uv, conv_w, conv_b,
                      gate_a_w, gate_a_b, gate_x_w, gate_x_b, lru_lambda, w_out,
                      pool_w, pool_scale, xa_wq, xa_wo, mlp_up, mlp_down, ln_g, ln_b)
    depth = ln_g.shape[0]
    n_even, n_odd = w_in.shape[0], pool_w.shape[0]
    bp, _, d_model = x_prompt.shape
    n_mem = mem_prompt.shape[1]
    dt = x_prompt.dtype
    lru_w = conv_w.shape[2]
    mem_rows = mem_prompt.reshape(bp * n_mem, d_model)
    p_mem_k = _rows_matmul(mem_rows, xa_wk, F32, tm_pref=1024, tn_pref=512).reshape(depth, bp, n_mem, d_model)
    p_mem_v = _rows_matmul(mem_rows, xa_wv, F32, tm_pref=1024, tn_pref=512).reshape(depth, bp, n_mem, d_model)
    prompt = _Group(
        x_prompt, 0, p_mem_k, p_mem_v,
        jnp.zeros((n_even, bp, conv_w.shape[1] - 1, lru_w), dt),
        jnp.zeros((n_even, bp, lru_w), dt),
        jnp.zeros((n_even, bp, 0, kv_norm_g.shape[1]), dt),
        jnp.zeros((n_even, bp, 0, cache_mla_kpe.shape[3]), dt),
        jnp.zeros((n_odd, bp, state_pool.shape[2], d_model), dt), P)
    sample = _Group(
        x_sample, cache_mla_ckv.shape[2], cache_mem_k, cache_mem_v,
        state_rglru_conv, state_rglru_h, cache_mla_ckv, cache_mla_kpe, state_pool, P)
    alpha = (2.0 * depth) ** 0.25
    ln_g3, ln_b3 = P['ln_g'], P['ln_b']
    for l in range(depth):
        sample.pre_mlp(l, P, alpha)
        prompt.pre_mlp(l, P, alpha)
        ln_i = N_NORMS * l + 2
        sample.x, up_b, down_b = _mlp_ln(sample.x, mlp_up, mlp_down, l, ln_g3, ln_b3, ln_i, alpha,
                                         emit=True, tf_pref=512)
        prompt.x = _mlp_ln(prompt.x, up_b, down_b, None, ln_g3, ln_b3, ln_i, alpha)
    y_prompt, p_conv, p_h, p_ckv, p_kpe, p_pool = prompt.results()
    y_sample, s_conv, s_h, s_ckv, s_kpe, s_pool = sample.results()
    return (y_prompt, y_sample, p_conv, p_h, p_ckv, p_kpe, p_pool, p_mem_k, p_mem_v,
            s_conv, s_h, s_ckv, s_kpe, s_pool)
```

```python
import functools

import jax
import jax.numpy as jnp
from jax import lax
from jax.experimental import pallas as pl
from jax.experimental.pallas import tpu as pltpu

F32 = jnp.float32
BF16 = jnp.bfloat16

CHUNK = 64
CHUNK_SHIFT = 6
assert 1 << CHUNK_SHIFT == CHUNK
MEM_HEADS = 4
POOL_WINDOWS = (2, 4, 8, 16)
LRU_C = 8.0
ROPE_THETA = 10000.0
LN_EPS = 1e-5
RMS_EPS = 1e-6
NEG_BIG = -1e30
LOG2_E = 1.4426950408889634

V7X_VMEM_BYTES = 64 * 1024 * 1024
VMEM_LIMIT = V7X_VMEM_BYTES - 8 * 1024 * 1024
LANES = 128
SUBLANES = 8
IN_PROJ_ROWS = 512
MLA_STACK_ROWS = 512
XATTN_FUSE_ROWS = 256
NT_DIMS = (((1,), (1,)), ((), ()))


def _params(*sem):
    return pltpu.CompilerParams(dimension_semantics=sem, vmem_limit_bytes=VMEM_LIMIT)


def _tile(n, pref):
    if n <= pref:
        return n
    t = pref
    while n % t:
        t -= SUBLANES
    assert t > 0
    return t


def _sel(arr, *lead, single=False):
    n_lead = len(lead)
    rest = arr.shape[n_lead:]
    idx = tuple(lead) + (0,) * len(rest)
    mode = dict(pipeline_mode=pl.Buffered(1)) if single else {}
    return pl.BlockSpec((None,) * n_lead + tuple(rest), lambda *_: idx, **mode)


def _layer_norm(y, g, b):
    mu = jnp.mean(y, axis=-1, keepdims=True)
    d = y - mu
    var = jnp.mean(d * d, axis=-1, keepdims=True)
    return d * lax.rsqrt(var + LN_EPS) * g + b


def _rms_norm(x, g):
    return x * lax.rsqrt(jnp.mean(x * x, axis=-1, keepdims=True) + RMS_EPS) * g


def _rope_lanes(x, cos_t, sin_t):
    n = x.shape[-1]
    half = 32
    lane = lax.broadcasted_iota(jnp.int32, x.shape, x.ndim - 1)
    first = (lane % (2 * half)) < half
    swapped = jnp.where(first, pltpu.roll(x, n - half, x.ndim - 1), pltpu.roll(x, half, x.ndim - 1))
    return x * cos_t + swapped * sin_t


def _rows_matmul_kernel(a_ref, w_ref, o_ref):
    a = a_ref[...].astype(BF16)
    w = w_ref[...].astype(BF16)
    o_ref[...] = jnp.dot(a, w, preferred_element_type=F32).astype(o_ref.dtype)


def _rows_matmul(a, w, out_dtype, g0=0, n_g=None, tm_pref=512, tn_pref=None):
    M, K = a.shape
    N = w.shape[2]
    n_g = w.shape[0] if n_g is None else n_g
    tm = _tile(M, tm_pref)
    tn = N if tn_pref is None else _tile(N, tn_pref)
    return pl.pallas_call(
        _rows_matmul_kernel,
        out_shape=jax.ShapeDtypeStruct((n_g, M, N), out_dtype),
        grid=(n_g, N // tn, M // tm),
        in_specs=[pl.BlockSpec((tm, K), lambda g, n, i: (i, 0)),
                  pl.BlockSpec((None, K, tn), lambda g, n, i: (g0 + g, 0, n))],
        out_specs=pl.BlockSpec((None, tm, tn), lambda g, n, i: (g, i, n)),
        compiler_params=_params("arbitrary", "arbitrary", "arbitrary"),
        name="rows_matmul",
    )(a, w)


def _mm_ln_kernel(*refs, n_a, alpha, sub):
    a_refs = refs[:n_a]
    w_ref, res_ref, g_ref, b_ref, o_ref = refs[n_a:]
    for r0 in range(0, o_ref.shape[0], sub):
        rows = slice(r0, r0 + sub)
        y = None
        off = 0
        for a_ref in a_refs:
            k = a_ref.shape[1]
            part = jnp.dot(a_ref[rows, :].astype(BF16), w_ref[off:off + k, :], preferred_element_type=F32)
            y = part if y is None else y + part
            off += k
        o_ref[rows, :] = _layer_norm(alpha * res_ref[rows, :] + y, g_ref[...], b_ref[...])


def _mm_ln(a_list, w, wl, res, ln_g, ln_b, ln_i, alpha, tm_pref=512, sub_pref=256):
    M, N = res.shape
    tm = _tile(M, tm_pref)
    sub = _tile(tm, sub_pref)
    in_specs = [pl.BlockSpec((tm, a.shape[1]), lambda i: (i, 0)) for a in a_list]
    in_specs += [_sel(w, wl, single=True), pl.BlockSpec((tm, N), lambda i: (i, 0)),
                 _sel(ln_g, ln_i), _sel(ln_b, ln_i)]
    return pl.pallas_call(
        functools.partial(_mm_ln_kernel, n_a=len(a_list), alpha=alpha, sub=sub),
        out_shape=jax.ShapeDtypeStruct((M, N), F32),
        grid=(M // tm,),
        in_specs=in_specs,
        out_specs=pl.BlockSpec((tm, N), lambda i: (i, 0)),
        compiler_params=_params("arbitrary"),
        name="matmul_res_ln",
    )(*a_list, w, res, ln_g, ln_b)


def _mlp_ln_kernel(x_ref, wu_ref, wd_ref, g_ref, b_ref, o_ref, *rest, alpha, sub, emit, n_steps):
    if emit:
        wub_ref, wdb_ref, xb_ref, acc_ref = rest
    else:
        xb_ref, acc_ref = rest
    j = pl.program_id(1)
    last = n_steps - 1

    def step(first, final):
        wu = wu_ref[...].astype(BF16)
        wd = wd_ref[...].astype(BF16)
        if emit:
            wub_ref[...] = wu
            wdb_ref[...] = wd
        if first:
            xb = x_ref[...].astype(BF16)
            xb_ref[...] = xb
        else:
            xb = xb_ref[...]
        h = jnp.dot(xb, wu, preferred_element_type=F32)
        h = jnp.square(jnp.maximum(h, 0.0)).astype(BF16)
        if not final:
            part = jnp.dot(h, wd, preferred_element_type=F32)
            if first:
                acc_ref[...] = part
            else:
                acc_ref[...] += part
            return
        for r0 in range(0, o_ref.shape[0], sub):
            rows = slice(r0, r0 + sub)
            y = jnp.dot(h[rows, :], wd, preferred_element_type=F32)
            if not first:
                y = acc_ref[rows, :] + y
            o_ref[rows, :] = _layer_norm(alpha * x_ref[rows, :] + y, g_ref[...], b_ref[...])

    if n_steps == 1:
        step(True, True)
    else:
        pl.when(j == 0)(functools.partial(step, True, False))
        if n_steps > 2:
            pl.when((j > 0) & (j < last))(functools.partial(step, False, False))
        pl.when(j == last)(functools.partial(step, False, True))


def _mlp_ln(x, w_up, w_down, l, ln_g, ln_b, ln_i, alpha, emit=False, tm_pref=512, tf_pref=1024,
            sub_pref=256):
    M, D = x.shape
    FF = w_up.shape[-1]
    tm = _tile(M, tm_pref)
    tf = _tile(FF, tf_pref)
    sub = _tile(tm, sub_pref)
    out_shape = [jax.ShapeDtypeStruct((M, D), F32)]
    out_specs = [pl.BlockSpec((tm, D), lambda i, j: (i, 0))]
    if emit:
        assert M == tm
        w_specs = [pl.BlockSpec((None, D, tf), lambda i, j: (l, 0, j)),
                   pl.BlockSpec((None, tf, D), lambda i, j: (l, j, 0))]
        out_shape += [jax.ShapeDtypeStruct((D, FF), BF16), jax.ShapeDtypeStruct((FF, D), BF16)]
        out_specs += [pl.BlockSpec((D, tf), lambda i, j: (0, j)), pl.BlockSpec((tf, D), lambda i, j: (j, 0))]
    else:
        w_specs = [pl.BlockSpec((D, tf), lambda i, j: (0, j)), pl.BlockSpec((tf, D), lambda i, j: (j, 0))]
    out = pl.pallas_call(
        functools.partial(_mlp_ln_kernel, alpha=alpha, sub=sub, emit=emit, n_steps=FF // tf),
        out_shape=out_shape,
        grid=(M // tm, FF // tf),
        in_specs=[pl.BlockSpec((tm, D), lambda i, j: (i, 0))] + w_specs + [_sel(ln_g, ln_i), _sel(ln_b, ln_i)],
        out_specs=out_specs,
        scratch_shapes=[pltpu.VMEM((tm, D), BF16), pltpu.VMEM((tm, D), F32)],
        compiler_params=_params("arbitrary", "arbitrary"),
        name="mlp_res_ln_round" if emit else "mlp_res_ln",
    )(x, w_up, w_down, ln_g, ln_b)
    return out if emit else out[0]


def _in_proj_kernel(x_ref, w_ref, qg_ref, kg_ref, cos_ref, sin_ref,
                    ug_ref, cqn_ref, ckv_ref, kpe_ref, *, ug_cols, q_lora, kv_lora, qk_rope):
    xb = x_ref[...].astype(BF16)
    ug_ref[...] = jnp.dot(xb, w_ref[:, :ug_cols], preferred_element_type=F32)
    z = jnp.dot(xb, w_ref[:, ug_cols:], preferred_element_type=F32)
    cqn_ref[...] = _rms_norm(z[:, :q_lora], qg_ref[...]).astype(BF16)
    ckv_ref[...] = _rms_norm(z[:, q_lora:q_lora + kv_lora], kg_ref[...])
    pe = z[:, q_lora + kv_lora:]
    pad = cos_ref.shape[1] - qk_rope
    pe = jnp.concatenate([pe, jnp.zeros((pe.shape[0], pad), F32)], axis=1)
    kpe_ref[...] = _rope_lanes(pe, cos_ref[...], sin_ref[...])[:, :qk_rope]


def _in_proj(x, P, j, cos_t, sin_t, tm):
    M, D = x.shape
    w = P['w_in']
    q_lora, kv_lora, qk_rope = P['q_lora'], P['kv_lora'], P['rope_d']
    ug_cols = w.shape[2] - q_lora - kv_lora - qk_rope
    assert M % tm == 0 and cos_t.shape[0] % tm == 0 and ug_cols % LANES == 0
    n_pos_blocks = cos_t.shape[0] // tm
    pe_w = cos_t.shape[1]
    row = lambda i: (i, 0)
    pos = lambda i: (i % n_pos_blocks, 0)
    return pl.pallas_call(
        functools.partial(_in_proj_kernel, ug_cols=ug_cols, q_lora=q_lora, kv_lora=kv_lora, qk_rope=qk_rope),
        out_shape=(jax.ShapeDtypeStruct((M, ug_cols), F32),
                   jax.ShapeDtypeStruct((M, q_lora), BF16),
                   jax.ShapeDtypeStruct((M, kv_lora), F32),
                   jax.ShapeDtypeStruct((M, qk_rope), F32)),
        grid=(M // tm,),
        in_specs=[pl.BlockSpec((tm, D), row), _sel(w, j, single=True),
                  _sel(P['qg'], j), _sel(P['kg'], j),
                  pl.BlockSpec((tm, pe_w), pos), pl.BlockSpec((tm, pe_w), pos)],
        out_specs=(pl.BlockSpec((tm, ug_cols), row), pl.BlockSpec((tm, q_lora), row),
                   pl.BlockSpec((tm, kv_lora), row), pl.BlockSpec((tm, qk_rope), row)),
        compiler_params=_params("arbitrary"),
        name="in_proj",
    )(x, w, P['qg'], P['kg'], cos_t, sin_t)


def _lru_kernel(u_ref, gate_ref, cbuf_ref, h0_ref, cw_ref, cb_ref, gaw_ref, gab_ref,
                gxw_ref, gxb_ref, lam_ref, out_ref, hlast_ref,
                ubuf_ref, hcar_ref, a_ref, b_ref, *, tt, cw, heads):
    t = pl.program_id(1)
    pad = SUBLANES
    tail = cw - 1

    @pl.when(t == 0)
    def _():
        ubuf_ref[pad - tail:pad, :] = cbuf_ref[0]
        hcar_ref[...] = h0_ref[0]

    ubuf_ref[pad:pad + tt, :] = u_ref[...]
    uc = cb_ref[...]
    for k in range(cw):
        uc = uc + cw_ref[k:k + 1, :] * ubuf_ref[pad - tail + k:pad - tail + k + tt, :]
    ubuf_ref[pad - tail:pad, :] = ubuf_ref[pad + tt - tail:pad + tt, :]

    width = uc.shape[1]
    blk = width // heads
    ucb = uc.astype(BF16)
    rs, igs = [], []
    for h in range(heads):
        uh = ucb[:, h * blk:(h + 1) * blk]
        rs.append(jnp.dot(uh, gaw_ref[h], preferred_element_type=F32))
        igs.append(jnp.dot(uh, gxw_ref[h], preferred_element_type=F32))
    r = jax.nn.sigmoid(jnp.concatenate(rs, axis=1) + gab_ref[...])
    ig = jax.nn.sigmoid(jnp.concatenate(igs, axis=1) + gxb_ref[...])
    nlam = -lam_ref[...]
    softplus = jnp.maximum(nlam, 0.0) + jnp.log1p(jnp.exp(-jnp.abs(nlam)))
    log_a = -LRU_C * r * softplus
    a = jnp.exp(log_a)
    a_ref[...] = a
    b_ref[...] = jnp.sqrt(-jnp.tanh(log_a) * (a * a + 1.0)) * (ig * uc)

    def step(i, h):
        h = a_ref[pl.ds(i, 1), :] * h + b_ref[pl.ds(i, 1), :]
        b_ref[pl.ds(i, 1), :] = h
        return h

    h_end = lax.fori_loop(0, tt, step, hcar_ref[...], unroll=8)
    hcar_ref[...] = h_end
    out_ref[...] = (b_ref[...] * jax.nn.gelu(gate_ref[...])).astype(out_ref.dtype)

    @pl.when(t == pl.num_programs(1) - 1)
    def _():
        hlast_ref[0] = h_end


def _lru(ug, conv_buf, h0, P, j, B, T, tt_pref=256):
    M = ug.shape[0]
    W = ug.shape[1] // 2
    heads = P['gaw'].shape[1]
    width = P['cw'].shape[1]
    tt = _tile(T, tt_pref)
    assert tt % SUBLANES == 0
    nt = T // tt
    row_u = lambda b, t: (b * nt + t, 0)
    row_g = lambda b, t: (b * nt + t, 1)
    names = ('cw', 'cb', 'gaw', 'gab', 'gxw', 'gxb', 'lam')
    return pl.pallas_call(
        functools.partial(_lru_kernel, tt=tt, cw=width, heads=heads),
        out_shape=(jax.ShapeDtypeStruct((M, W), BF16), jax.ShapeDtypeStruct((B, 1, W), F32)),
        grid=(B, nt),
        in_specs=[pl.BlockSpec((tt, W), row_u), pl.BlockSpec((tt, W), row_g),
                  pl.BlockSpec((1, width - 1, W), lambda b, t: (b, 0, 0)),
                  pl.BlockSpec((1, 1, W), lambda b, t: (b, 0, 0))] + [_sel(P[n], j) for n in names],
        out_specs=(pl.BlockSpec((tt, W), row_u), pl.BlockSpec((1, 1, W), lambda b, t: (b, 0, 0))),
        scratch_shapes=[pltpu.VMEM((SUBLANES + tt, W), F32), pltpu.VMEM((1, W), F32),
                        pltpu.VMEM((tt, W), F32), pltpu.VMEM((tt, W), F32)],
        compiler_params=_params("arbitrary", "arbitrary"),
        name="rg_lru",
    )(ug, ug, conv_buf, h0, *[P[n] for n in names])


def _mla_kernel(cqn_ref, wuq_ref, wukt_ref, cos_ref, sin_ref, ckv_ref, kpe_ref, wuv_ref, o_ref,
                qlat_ref, qpe_ref, m_ref, l_ref, acc_ref, sa_ref, sb_ref,
                *, tq, tk, hg, q_pos0, n_keys, scale):
    heads, nope, lat = wukt_ref.shape
    rope_d = qpe_ref.shape[2]
    vdim = wuv_ref.shape[2]
    rows = hg * tq
    n_groups = heads // hg
    assert n_groups == 1 or n_groups % 2 == 0

    q = jnp.dot(cqn_ref[...], wuq_ref[...], preferred_element_type=F32)
    qn = q[:, :heads * nope].astype(BF16)
    pe = _rope_lanes(q[:, heads * nope:], cos_ref[...], sin_ref[...])
    for h in range(heads):
        qlat_ref[h] = jnp.dot(qn[:, h * nope:(h + 1) * nope], wukt_ref[h],
                              preferred_element_type=F32).astype(BF16)
        qpe_ref[h] = pe[:, h * rope_d:(h + 1) * rope_d].astype(BF16)
    qi = pl.program_id(1)
    q_lo = q_pos0 + qi * tq
    vis_all = jnp.minimum(((q_lo >> CHUNK_SHIFT) + 1) * CHUNK, n_keys)
    vis_any = jnp.minimum((((q_lo + tq - 1) >> CHUNK_SHIFT) + 1) * CHUNK, n_keys)
    n_full = vis_all // tk
    n_tot = (vis_any + tk - 1) // tk

    m_ref[...] = jnp.full_like(m_ref, NEG_BIG)
    l_ref[...] = jnp.zeros_like(l_ref)
    acc_ref[...] = jnp.zeros_like(acc_ref)

    def block(kj, carry, masked):
        k0 = pl.multiple_of(kj * tk, tk)
        kc = ckv_ref[0, pl.ds(k0, tk), :]
        kp = kpe_ref[0, pl.ds(k0, tk), :]
        if masked:
            kpos = k0 + lax.broadcasted_iota(jnp.int32, (rows, tk), 1)
            qpos = q_lo + (lax.broadcasted_iota(jnp.int32, (rows, tk), 0) & (tq - 1))
            ok = ((kpos >> CHUNK_SHIFT) <= (qpos >> CHUNK_SHIFT)) & (kpos < n_keys)
        def scores(g):
            hs = pl.ds(g * hg, hg)
            ql = qlat_ref[hs].reshape(rows, lat)
            qp = qpe_ref[hs].reshape(rows, rope_d)
            s = lax.dot_general(ql, kc, NT_DIMS, preferred_element_type=F32)
            return s + lax.dot_general(qp, kp, NT_DIMS, preferred_element_type=F32)

        def update(g, s):
            if masked:
                s = jnp.where(ok, s, NEG_BIG)
            m_old = m_ref[g]
            m_new = jnp.maximum(m_old, jnp.max(s, axis=1, keepdims=True))
            corr = jnp.exp2((m_old - m_new) * (scale * LOG2_E))
            p = jnp.exp2((s - m_new) * (scale * LOG2_E))
            l_ref[g] = corr * l_ref[g] + jnp.sum(p, axis=1, keepdims=True)
            acc_ref[g] = corr * acc_ref[g] + jnp.dot(p.astype(BF16), kc, preferred_element_type=F32)
            m_ref[g] = m_new

        if n_groups == 1:
            update(0, scores(0))
            return carry

        sa_ref[...] = scores(0)

        def pair(i, c):
            g = 2 * i
            sb_ref[...] = scores(g + 1)
            update(g, sa_ref[...])
            sa_ref[...] = scores(jnp.minimum(g + 2, n_groups - 1))
            update(g + 1, sb_ref[...])
            return c

        lax.fori_loop(0, n_groups // 2, pair, 0)
        return carry

    lax.fori_loop(0, n_full, functools.partial(block, masked=False), 0)
    lax.fori_loop(n_full, n_tot, functools.partial(block, masked=True), 0)
    for h in range(heads):
        g, r0 = h // hg, (h % hg) * tq
        o = (acc_ref[g, r0:r0 + tq, :] / l_ref[g, r0:r0 + tq, :]).astype(BF16)
        o_ref[:, h * vdim:(h + 1) * vdim] = jnp.dot(
            o, wuv_ref[h], preferred_element_type=F32).astype(o_ref.dtype)


def _mla_attention(cqn, ckv_all, kpe_all, P, j, cos_t, sin_t, B, T, q_pos0, n_keys, tq_pref=512, tk_pref=512):
    wuq, wukt, wuv = P['wuq'], P['wukt'], P['wuv']
    _, heads, nope, lat = wukt.shape
    rope_d = P['rope_d']
    vdim = wuv.shape[3]
    q_lora = cqn.shape[1]
    Tk = ckv_all.shape[1]
    tq = _tile(T, tq_pref)
    tk = _tile(Tk, tk_pref)
    nt = T // tq
    hg = max(1, min(heads, MLA_STACK_ROWS // tq))
    assert heads % hg == 0 and tq & (tq - 1) == 0 and cos_t.shape[0] == T
    scale = (nope + rope_d) ** -0.5
    return pl.pallas_call(
        functools.partial(_mla_kernel, tq=tq, tk=tk, hg=hg, q_pos0=q_pos0, n_keys=n_keys, scale=scale),
        out_shape=jax.ShapeDtypeStruct((B * T, heads * vdim), BF16),
        grid=(B, nt),
        in_specs=[pl.BlockSpec((tq, q_lora), lambda b, t: (b * nt + t, 0)),
                  _sel(wuq, j), _sel(wukt, j),
                  pl.BlockSpec((tq, heads * rope_d), lambda b, t: (t, 0)),
                  pl.BlockSpec((tq, heads * rope_d), lambda b, t: (t, 0)),
                  pl.BlockSpec((1, Tk, lat), lambda b, t: (b, 0, 0)),
                  pl.BlockSpec((1, Tk, rope_d), lambda b, t: (b, 0, 0)),
                  _sel(wuv, j)],
        out_specs=pl.BlockSpec((tq, heads * vdim), lambda b, t: (b * nt + t, 0)),
        scratch_shapes=[pltpu.VMEM((heads, tq, lat), BF16), pltpu.VMEM((heads, tq, rope_d), BF16),
                        pltpu.VMEM((heads // hg, hg * tq, 1), F32), pltpu.VMEM((heads // hg, hg * tq, 1), F32),
                        pltpu.VMEM((heads // hg, hg * tq, lat), F32),
                        pltpu.VMEM((hg * tq, tk), F32), pltpu.VMEM((hg * tq, tk), F32)],
        compiler_params=_params("arbitrary", "arbitrary"),
        name="mla_attention",
    )(cqn, wuq, wukt, cos_t, sin_t, ckv_all, kpe_all, wuv)


def _xattn_kernel(q_ref, k_ref, v_ref, o_ref, *, heads, scale):
    hd = q_ref.shape[1] // heads
    for h in range(heads):
        sl = slice(h * hd, (h + 1) * hd)
        q = q_ref[:, sl]
        k = k_ref[:, sl].astype(BF16)
        v = v_ref[:, sl].astype(BF16)
        s = lax.dot_general(q, k, NT_DIMS, preferred_element_type=F32) * scale
        p = jnp.exp(s - jnp.max(s, axis=1, keepdims=True))
        l = jnp.sum(p, axis=1, keepdims=True)
        o = jnp.dot(p.astype(BF16), v, preferred_element_type=F32) / l
        o_ref[:, sl] = o.astype(o_ref.dtype)


def _xattn(q, mem_k, mem_v, l, B, T, tq_pref=512):
    M, D = q.shape
    n_mem = mem_k.shape[2]
    tq = _tile(T, tq_pref)
    nt = T // tq
    scale = (D // MEM_HEADS) ** -0.5
    mem_spec = pl.BlockSpec((None, None, n_mem, D), lambda b, t: (l, b, 0, 0))
    return pl.pallas_call(
        functools.partial(_xattn_kernel, heads=MEM_HEADS, scale=scale),
        out_shape=jax.ShapeDtypeStruct((M, D), BF16),
        grid=(B, nt),
        in_specs=[pl.BlockSpec((tq, D), lambda b, t: (b * nt + t, 0)), mem_spec, mem_spec],
        out_specs=pl.BlockSpec((tq, D), lambda b, t: (b * nt + t, 0)),
        compiler_params=_params("arbitrary", "arbitrary"),
        name="mem_xattn",
    )(q, mem_k, mem_v)


def _xattn_ln_kernel(q_ref, k_ref, v_ref, wo_ref, res_ref, g_ref, b_ref, o_ref, *, heads, scale, alpha, sub):
    hd = q_ref.shape[1] // heads
    kb = k_ref[...].astype(BF16)
    vb = v_ref[...].astype(BF16)
    for r0 in range(0, o_ref.shape[0], sub):
        rows = slice(r0, r0 + sub)
        outs = []
        for h in range(heads):
            sl = slice(h * hd, (h + 1) * hd)
            s = lax.dot_general(q_ref[rows, sl], kb[:, sl], NT_DIMS, preferred_element_type=F32) * scale
            p = jnp.exp(s - jnp.max(s, axis=1, keepdims=True))
            l = jnp.sum(p, axis=1, keepdims=True)
            outs.append((jnp.dot(p.astype(BF16), vb[:, sl], preferred_element_type=F32) / l).astype(BF16))
        y = jnp.dot(jnp.concatenate(outs, axis=1), wo_ref[...], preferred_element_type=F32)
        o_ref[rows, :] = _layer_norm(alpha * res_ref[rows, :] + y, g_ref[...], b_ref[...])


def _xattn_ln(q, mem_k, mem_v, l, wo, res, ln_g, ln_b, ln_i, alpha, B, T, tq_pref=512, sub_pref=256):
    M, D = q.shape
    n_mem = mem_k.shape[2]
    tq = _tile(T, tq_pref)
    sub = _tile(tq, sub_pref)
    nt = T // tq
    scale = (D // MEM_HEADS) ** -0.5
    row = lambda b, t: (b * nt + t, 0)
    mem_spec = pl.BlockSpec((None, None, n_mem, D), lambda b, t: (l, b, 0, 0))
    return pl.pallas_call(
        functools.partial(_xattn_ln_kernel, heads=MEM_HEADS, scale=scale, alpha=alpha, sub=sub),
        out_shape=jax.ShapeDtypeStruct((M, D), F32),
        grid=(B, nt),
        in_specs=[pl.BlockSpec((tq, D), row), mem_spec, mem_spec, _sel(wo, l, single=True),
                  pl.BlockSpec((tq, D), row), _sel(ln_g, ln_i), _sel(ln_b, ln_i)],
        out_specs=pl.BlockSpec((tq, D), row),
        compiler_params=_params("arbitrary", "arbitrary"),
        name="mem_xattn_out_ln",
    )(q, mem_k, mem_v, wo, res, ln_g, ln_b)


def _pool_ln_kernel(x_ref, pbuf_ref, pw_ref, ps_ref, g_ref, b_ref, o_ref, full_ref, y_ref,
                    *, tt, pos0, alpha):
    t = pl.program_id(1)
    halo = 2 * SUBLANES
    groups = pw_ref.shape[0]
    gw = pw_ref.shape[1]

    @pl.when(t == 0)
    def _():
        full_ref[0:halo, :] = pbuf_ref[0]

    full_ref[halo:halo + tt, :] = x_ref[...]
    pos = pos0 + t * tt + lax.broadcasted_iota(jnp.int32, (tt, 1), 0)
    for g in range(groups):
        w = POOL_WINDOWS[g]
        sl = slice(g * gw, (g + 1) * gw)
        f = full_ref[:, sl]
        s = f
        d = 1
        while d < w:
            s = s + pltpu.roll(s, d, 0)
            d *= 2
        cnt = jnp.minimum(pos + 1, w).astype(F32)
        dlt = (s[halo:] / cnt - f[halo:]).astype(BF16)
        y_ref[:, sl] = jnp.dot(dlt, pw_ref[g], preferred_element_type=F32)
    full_ref[0:halo, :] = full_ref[tt:tt + halo, :]
    o_ref[...] = _layer_norm(alpha * x_ref[...] + y_ref[...] * ps_ref[...], g_ref[...], b_ref[...])


def _pool_ln(x, pbuf16, P, j, ln_i, alpha, B, T, pos0, tt_pref=512):
    M, D = x.shape
    tt = _tile(T, tt_pref)
    nt = T // tt
    halo = 2 * SUBLANES
    assert max(POOL_WINDOWS) <= halo and tt >= halo
    row = lambda bb, t: (bb * nt + t, 0)
    return pl.pallas_call(
        functools.partial(_pool_ln_kernel, tt=tt, pos0=pos0, alpha=alpha),
        out_shape=jax.ShapeDtypeStruct((M, D), F32),
        grid=(B, nt),
        in_specs=[pl.BlockSpec((tt, D), row),
                  pl.BlockSpec((1, halo, D), lambda bb, t: (bb, 0, 0)),
                  _sel(P['pw'], j), _sel(P['ps'], j), _sel(P['ln_g'], ln_i), _sel(P['ln_b'], ln_i)],
        out_specs=pl.BlockSpec((tt, D), row),
        scratch_shapes=[pltpu.VMEM((halo + tt, D), F32), pltpu.VMEM((tt, D), F32)],
        compiler_params=_params("arbitrary", "arbitrary"),
        name="pool_res_ln",
    )(x, pbuf16, P['pw'], P['ps'], P['ln_g'], P['ln_b'])


def _rope_tables(pos, rope_d, heads):
    half = rope_d // 2
    inv = ROPE_THETA ** (-jnp.arange(half, dtype=F32) / half)
    ang = pos.astype(F32)[:, None] * inv[None, :]
    cos, sin = jnp.cos(ang), jnp.sin(ang)
    cos_t = jnp.tile(jnp.concatenate([cos, cos], axis=1), (1, heads))
    sin_t = jnp.tile(jnp.concatenate([-sin, sin], axis=1), (1, heads))
    return cos_t, sin_t


def _prep_weights(w_in, q_norm_g, w_uq, kv_norm_g, w_uk, w_uv, conv_w, conv_b,
                  gate_a_w, gate_a_b, gate_x_w, gate_x_b, lru_lambda, w_out,
                  pool_w, pool_scale, xa_wq, xa_wo, ln_g, ln_b):
    n_even = w_in.shape[0]
    d_model = ln_g.shape[2]
    q_lora = q_norm_g.shape[1]
    kv_lora = kv_norm_g.shape[1]
    heads, nope = w_uk.shape[2], w_uk.shape[3]
    rope_d = w_uq.shape[2] // heads - nope
    wuq = w_uq.reshape(n_even, q_lora, heads, nope + rope_d)
    wuq = jnp.concatenate([wuq[..., :nope].reshape(n_even, q_lora, heads * nope),
                           wuq[..., nope:].reshape(n_even, q_lora, heads * rope_d)], axis=2).astype(BF16)
    wukt = jnp.transpose(w_uk, (0, 2, 3, 1)).astype(BF16)
    wuv = jnp.transpose(w_uv, (0, 2, 1, 3)).astype(BF16)
    row = lambda a: a.reshape(a.shape[0], 1, -1)
    return dict(
        w_in=w_in.astype(BF16), qg=row(q_norm_g), kg=row(kv_norm_g), wuq=wuq, wukt=wukt, wuv=wuv,
        cw=conv_w, cb=row(conv_b), gaw=gate_a_w.astype(BF16), gab=row(gate_a_b),
        gxw=gate_x_w.astype(BF16), gxb=row(gate_x_b), lam=row(lru_lambda),
        w_out=w_out.astype(BF16), pw=pool_w.astype(BF16), ps=row(pool_scale),
        wq=xa_wq.astype(BF16), wo=xa_wo.astype(BF16),
        ln_g=ln_g.reshape(-1, 1, d_model), ln_b=ln_b.reshape(-1, 1, d_model),
        q_lora=q_lora, kv_lora=kv_lora, heads=heads, nope=nope, rope_d=rope_d)


N_NORMS = 3


class _Group:
    def __init__(self, x3, t_past, mem_k, mem_v, conv_buf, h0, ckv_past, kpe_past, pool_buf, P):
        self.B, self.T, self.D = x3.shape
        B, T = self.B, self.T
        self.x = x3.reshape(B * T, self.D)
        self.t_past, self.mem_k, self.mem_v = t_past, mem_k, mem_v
        self.conv_buf, self.h0, self.ckv_past, self.kpe_past, self.pool_buf = (
            conv_buf, h0, ckv_past, kpe_past, pool_buf)
        self.cos_t, self.sin_t = _rope_tables(t_past + jnp.arange(T), P['rope_d'], P['heads'])
        self.tm_in = _tile(T, IN_PROJ_ROWS) if T >= IN_PROJ_ROWS else B * T
        reps = max(1, self.tm_in // T)
        self.cos_rows = jnp.tile(self.cos_t, (reps, 1))[:, :LANES]
        self.sin_rows = jnp.tile(self.sin_t, (reps, 1))[:, :LANES]
        self.convs, self.hs, self.ckvs, self.kpes, self.pools = [], [], [], [], []

    def pre_mlp(self, l, P, alpha):
        B, T, D = self.B, self.T, self.D
        x, t_past, mem_k, mem_v = self.x, self.t_past, self.mem_k, self.mem_v
        conv_buf, h0, ckv_past, kpe_past, pool_buf = (
            self.conv_buf, self.h0, self.ckv_past, self.kpe_past, self.pool_buf)
        cos_t, sin_t = self.cos_t, self.sin_t
        convs, hs, ckvs, kpes, pools = self.convs, self.hs, self.ckvs, self.kpes, self.pools
        n_norms = N_NORMS
        ln_g, ln_b = P['ln_g'], P['ln_b']
        j = l // 2
        if l % 2 == 0:
            ug, cqn, ckv, kpe = _in_proj(x, P, j, self.cos_rows, self.sin_rows, self.tm_in)
            lru_w = ug.shape[1] // 2
            rec, h_last = _lru(ug, conv_buf[j], h0[j][:, None, :], P, j, B, T)
            ckv3 = ckv.reshape(B, T, -1)
            kpe3 = kpe.reshape(B, T, -1)
            n_keys = ckv_past[j].shape[1] + T
            tk = 512 if n_keys % 512 == 0 else 256
            padk = (-n_keys) % tk
            ckv_all = jnp.concatenate([ckv_past[j].astype(BF16), ckv3.astype(BF16),
                                       jnp.zeros((B, padk, ckv3.shape[2]), BF16)], axis=1)
            kpe_all = jnp.concatenate([kpe_past[j].astype(BF16), kpe3.astype(BF16),
                                       jnp.zeros((B, padk, kpe3.shape[2]), BF16)], axis=1)
            attn = _mla_attention(cqn, ckv_all, kpe_all, P, j, cos_t, sin_t, B, T, t_past, n_keys,
                                  tk_pref=tk)
            x = _mm_ln([rec, attn], P['w_out'], j, x, ln_g, ln_b, n_norms * l, alpha)
            tail = P['cw'].shape[1] - 1
            convs.append(ug.reshape(B, T, -1)[:, T - tail:, :lru_w])
            hs.append(h_last[:, 0, :])
            ckvs.append(ckv3)
            kpes.append(kpe3)
        else:
            nbuf = pool_buf[j].shape[1]
            pools.append(x.reshape(B, T, D)[:, T - nbuf:])
            pbuf16 = jnp.pad(pool_buf[j], ((0, 0), (2 * SUBLANES - nbuf, 0), (0, 0)))
            x = _pool_ln(x, pbuf16, P, j, n_norms * l, alpha, B, T, t_past)
        q = _rows_matmul(x, P['wq'], BF16, g0=l, n_g=1)[0]
        if T >= XATTN_FUSE_ROWS:
            x = _xattn_ln(q, mem_k, mem_v, l, P['wo'], x, ln_g, ln_b, n_norms * l + 1, alpha, B, T)
        else:
            o = _xattn(q, mem_k, mem_v, l, B, T)
            x = _mm_ln([o], P['wo'], l, x, ln_g, ln_b, n_norms * l + 1, alpha)
        self.x = x

    def results(self):
        return (self.x.reshape(self.B, self.T, self.D), jnp.stack(self.convs), jnp.stack(self.hs),
                jnp.stack(self.ckvs), jnp.stack(self.kpes), jnp.stack(self.pools))


def kernel(x_prompt, x_sample, mem_prompt, cache_mem_k, cache_mem_v, cache_mla_ckv, cache_mla_kpe,
           state_rglru_h, state_rglru_conv, state_pool,
           w_in, q_norm_g, w_uq, kv_norm_g, w_uk, w_uv, conv_w, conv_b,
           gate_a_w, gate_a_b, gate_x_w, gate_x_b, lru_lambda, w_out,
           pool_w, pool_scale, xa_wq, xa_wk, xa_wv, xa_wo, mlp_up, mlp_down, ln_g, ln_b):
    P = _prep_weights(w_in, q_norm_g, w_uq, kv_norm_g, w_uk, w_uv, conv_w, conv_b,
                      gate_a_w, gate_a_b, gate_x_w, gate_x_b, lru_lambda, w_out,
                      pool_w, pool_scale, xa_wq, xa_wo, ln_g, ln_b)
    depth = ln_g.shape[0]
    n_even, n_odd = w_in.shape[0], pool_w.shape[0]
    bp, _, d_model = x_prompt.shape
    n_mem = mem_prompt.shape[1]
    dt = x_prompt.dtype
    lru_w = conv_w.shape[2]
    mem_rows = mem_prompt.reshape(bp * n_mem, d_model)
    p_mem_k = _rows_matmul(mem_rows, xa_wk, F32, tm_pref=1024, tn_pref=512).reshape(depth, bp, n_mem, d_model)
    p_mem_v = _rows_matmul(mem_rows, xa_wv, F32, tm_pref=1024, tn_pref=512).reshape(depth, bp, n_mem, d_model)
    prompt = _Group(
        x_prompt, 0, p_mem_k, p_mem_v,
        jnp.zeros((n_even, bp, conv_w.shape[1] - 1, lru_w), dt),
        jnp.zeros((n_even, bp, lru_w), dt),
        jnp.zeros((n_even, bp, 0, kv_norm_g.shape[1]), dt),
        jnp.zeros((n_even, bp, 0, cache_mla_kpe.shape[3]), dt),
        jnp.zeros((n_odd, bp, state_pool.shape[2], d_model), dt), P)
    sample = _Group(
        x_sample, cache_mla_ckv.shape[2], cache_mem_k, cache_mem_v,
        state_rglru_conv, state_rglru_h, cache_mla_ckv, cache_mla_kpe, state_pool, P)
    alpha = (2.0 * depth) ** 0.25
    ln_g3, ln_b3 = P['ln_g'], P['ln_b']
    for l in range(depth):
        sample.pre_mlp(l, P, alpha)
        prompt.pre_mlp(l, P, alpha)
        ln_i = N_NORMS * l + 2
        sample.x, up_b, down_b = _mlp_ln(sample.x, mlp_up, mlp_down, l, ln_g3, ln_b3, ln_i, alpha,
                                         emit=True, tf_pref=512)
        prompt.x = _mlp_ln(prompt.x, up_b, down_b, None, ln_g3, ln_b3, ln_i, alpha)
    y_prompt, p_conv, p_h, p_ckv, p_kpe, p_pool = prompt.results()
    y_sample, s_conv, s_h, s_ckv, s_kpe, s_pool = sample.results()
    return (y_prompt, y_sample, p_conv, p_h, p_ckv, p_kpe, p_pool, p_mem_k, p_mem_v,
            s_conv, s_h, s_ckv, s_kpe, s_pool)
```

```python
import functools

import jax
import jax.numpy as jnp
from jax import lax
from jax.experimental import pallas as pl
from jax.experimental.pallas import tpu as pltpu

F32 = jnp.float32
BF16 = jnp.bfloat16

CHUNK = 64
CHUNK_SHIFT = 6
assert 1 << CHUNK_SHIFT == CHUNK
MEM_HEADS = 4
POOL_WINDOWS = (2, 4, 8, 16)
LRU_C = 8.0
ROPE_THETA = 10000.0
LN_EPS = 1e-5
RMS_EPS = 1e-6
NEG_BIG = -1e30
LOG2_E = 1.4426950408889634

V7X_VMEM_BYTES = 64 * 1024 * 1024
VMEM_LIMIT = V7X_VMEM_BYTES - 8 * 1024 * 1024
LANES = 128
SUBLANES = 8
IN_PROJ_ROWS = 512
MLA_STACK_ROWS = 512
XATTN_FUSE_ROWS = 256
NT_DIMS = (((1,), (1,)), ((), ()))


def _params(*sem):
    return pltpu.CompilerParams(dimension_semantics=sem, vmem_limit_bytes=VMEM_LIMIT)


def _tile(n, pref):
    if n <= pref:
        return n
    t = pref
    while n % t:
        t -= SUBLANES
    assert t > 0
    return t


def _sel(arr, *lead, single=False):
    n_lead = len(lead)
    rest = arr.shape[n_lead:]
    idx = tuple(lead) + (0,) * len(rest)
    mode = dict(pipeline_mode=pl.Buffered(1)) if single else {}
    return pl.BlockSpec((None,) * n_lead + tuple(rest), lambda *_: idx, **mode)


def _layer_norm(y, g, b):
    mu = jnp.mean(y, axis=-1, keepdims=True)
    d = y - mu
    var = jnp.mean(d * d, axis=-1, keepdims=True)
    return d * lax.rsqrt(var + LN_EPS) * g + b


def _rms_norm(x, g):
    return x * lax.rsqrt(jnp.mean(x * x, axis=-1, keepdims=True) + RMS_EPS) * g


def _rope_lanes(x, cos_t, sin_t):
    n = x.shape[-1]
    half = 32
    lane = lax.broadcasted_iota(jnp.int32, x.shape, x.ndim - 1)
    first = (lane % (2 * half)) < half
    swapped = jnp.where(first, pltpu.roll(x, n - half, x.ndim - 1), pltpu.roll(x, half, x.ndim - 1))
    return x * cos_t + swapped * sin_t


def _rows_matmul_kernel(a_ref, w_ref, o_ref):
    a = a_ref[...].astype(BF16)
    w = w_ref[...].astype(BF16)
    o_ref[...] = jnp.dot(a, w, preferred_element_type=F32).astype(o_ref.dtype)


def _rows_matmul(a, w, out_dtype, g0=0, n_g=None, tm_pref=512, tn_pref=None):
    M, K = a.shape
    N = w.shape[2]
    n_g = w.shape[0] if n_g is None else n_g
    tm = _tile(M, tm_pref)
    tn = N if tn_pref is None else _tile(N, tn_pref)
    return pl.pallas_call(
        _rows_matmul_kernel,
        out_shape=jax.ShapeDtypeStruct((n_g, M, N), out_dtype),
        grid=(n_g, N // tn, M // tm),
        in_specs=[pl.BlockSpec((tm, K), lambda g, n, i: (i, 0)),
                  pl.BlockSpec((None, K, tn), lambda g, n, i: (g0 + g, 0, n))],
        out_specs=pl.BlockSpec((None, tm, tn), lambda g, n, i: (g, i, n)),
        compiler_params=_params("arbitrary", "arbitrary", "arbitrary"),
        name="rows_matmul",
    )(a, w)


def _mm_ln_kernel(*refs, n_a, alpha, sub):
    a_refs = refs[:n_a]
    w_ref, res_ref, g_ref, b_ref, o_ref = refs[n_a:]
    for r0 in range(0, o_ref.shape[0], sub):
        rows = slice(r0, r0 + sub)
        y = None
        off = 0
        for a_ref in a_refs:
            k = a_ref.shape[1]
            part = jnp.dot(a_ref[rows, :].astype(BF16), w_ref[off:off + k, :], preferred_element_type=F32)
            y = part if y is None else y + part
            off += k
        o_ref[rows, :] = _layer_norm(alpha * res_ref[rows, :] + y, g_ref[...], b_ref[...])


def _mm_ln(a_list, w, wl, res, ln_g, ln_b, ln_i, alpha, tm_pref=512, sub_pref=256):
    M, N = res.shape
    tm = _tile(M, tm_pref)
    sub = _tile(tm, sub_pref)
    in_specs = [pl.BlockSpec((tm, a.shape[1]), lambda i: (i, 0)) for a in a_list]
    in_specs += [_sel(w, wl, single=True), pl.BlockSpec((tm, N), lambda i: (i, 0)),
                 _sel(ln_g, ln_i), _sel(ln_b, ln_i)]
    return pl.pallas_call(
        functools.partial(_mm_ln_kernel, n_a=len(a_list), alpha=alpha, sub=sub),
        out_shape=jax.ShapeDtypeStruct((M, N), F32),
        grid=(M // tm,),
        in_specs=in_specs,
        out_specs=pl.BlockSpec((tm, N), lambda i: (i, 0)),
        compiler_params=_params("arbitrary"),
        name="matmul_res_ln",
    )(*a_list, w, res, ln_g, ln_b)


def _mlp_ln_kernel(x_ref, wu_ref, wd_ref, g_ref, b_ref, o_ref, *rest, alpha, sub, emit, n_steps):
    if emit:
        wub_ref, wdb_ref, xb_ref, acc_ref = rest
    else:
        xb_ref, acc_ref = rest
    j = pl.program_id(1)
    last = n_steps - 1

    def step(first, final):
        wu = wu_ref[...].astype(BF16)
        wd = wd_ref[...].astype(BF16)
        if emit:
            wub_ref[...] = wu
            wdb_ref[...] = wd
        if first:
            xb = x_ref[...].astype(BF16)
            xb_ref[...] = xb
        else:
            xb = xb_ref[...]
        h = jnp.dot(xb, wu, preferred_element_type=F32)
        h = jnp.square(jnp.maximum(h, 0.0)).astype(BF16)
        if not final:
            part = jnp.dot(h, wd, preferred_element_type=F32)
            if first:
                acc_ref[...] = part
            else:
                acc_ref[...] += part
            return
        for r0 in range(0, o_ref.shape[0], sub):
            rows = slice(r0, r0 + sub)
            y = jnp.dot(h[rows, :], wd, preferred_element_type=F32)
            if not first:
                y = acc_ref[rows, :] + y
            o_ref[rows, :] = _layer_norm(alpha * x_ref[rows, :] + y, g_ref[...], b_ref[...])

    if n_steps == 1:
        step(True, True)
    else:
        pl.when(j == 0)(functools.partial(step, True, False))
        if n_steps > 2:
            pl.when((j > 0) & (j < last))(functools.partial(step, False, False))
        pl.when(j == last)(functools.partial(step, False, True))


def _mlp_ln(x, w_up, w_down, l, ln_g, ln_b, ln_i, alpha, emit=False, tm_pref=512, tf_pref=1024,
            sub_pref=256):
    M, D = x.shape
    FF = w_up.shape[-1]
    tm = _tile(M, tm_pref)
    tf = _tile(FF, tf_pref)
    sub = _tile(tm, sub_pref)
    out_shape = [jax.ShapeDtypeStruct((M, D), F32)]
    out_specs = [pl.BlockSpec((tm, D), lambda i, j: (i, 0))]
    if emit:
        assert M == tm
        w_specs = [pl.BlockSpec((None, D, tf), lambda i, j: (l, 0, j)),
                   pl.BlockSpec((None, tf, D), lambda i, j: (l, j, 0))]
        out_shape += [jax.ShapeDtypeStruct((D, FF), BF16), jax.ShapeDtypeStruct((FF, D), BF16)]
        out_specs += [pl.BlockSpec((D, tf), lambda i, j: (0, j)), pl.BlockSpec((tf, D), lambda i, j: (j, 0))]
    else:
        w_specs = [pl.BlockSpec((D, tf), lambda i, j: (0, j)), pl.BlockSpec((tf, D), lambda i, j: (j, 0))]
    out = pl.pallas_call(
        functools.partial(_mlp_ln_kernel, alpha=alpha, sub=sub, emit=emit, n_steps=FF // tf),
        out_shape=out_shape,
        grid=(M // tm, FF // tf),
        in_specs=[pl.BlockSpec((tm, D), lambda i, j: (i, 0))] + w_specs + [_sel(ln_g, ln_i), _sel(ln_b, ln_i)],
        out_specs=out_specs,
        scratch_shapes=[pltpu.VMEM((tm, D), BF16), pltpu.VMEM((tm, D), F32)],
        compiler_params=_params("arbitrary", "arbitrary"),
        name="mlp_res_ln_round" if emit else "mlp_res_ln",
    )(x, w_up, w_down, ln_g, ln_b)
    return out if emit else out[0]


def _in_proj_kernel(x_ref, w_ref, qg_ref, kg_ref, cos_ref, sin_ref,
                    ug_ref, cqn_ref, ckv_ref, kpe_ref, *, ug_cols, q_lora, kv_lora, qk_rope):
    xb = x_ref[...].astype(BF16)
    z = jnp.dot(xb, w_ref[:, ug_cols:], preferred_element_type=F32)
    cqn_ref[...] = _rms_norm(z[:, :q_lora], qg_ref[...]).astype(BF16)
    ckv_ref[...] = _rms_norm(z[:, q_lora:q_lora + kv_lora], kg_ref[...])
    pe = z[:, q_lora + kv_lora:]
    pad = cos_ref.shape[1] - qk_rope
    pe = jnp.concatenate([pe, jnp.zeros((pe.shape[0], pad), F32)], axis=1)
    kpe_ref[...] = _rope_lanes(pe, cos_ref[...], sin_ref[...])[:, :qk_rope]
    ug_ref[...] = jnp.dot(xb, w_ref[:, :ug_cols], preferred_element_type=F32)


def _in_proj(x, P, j, cos_t, sin_t, tm):
    M, D = x.shape
    w = P['w_in']
    q_lora, kv_lora, qk_rope = P['q_lora'], P['kv_lora'], P['rope_d']
    ug_cols = w.shape[2] - q_lora - kv_lora - qk_rope
    assert M % tm == 0 and cos_t.shape[0] % tm == 0 and ug_cols % LANES == 0
    n_pos_blocks = cos_t.shape[0] // tm
    pe_w = cos_t.shape[1]
    row = lambda i: (i, 0)
    pos = lambda i: (i % n_pos_blocks, 0)
    return pl.pallas_call(
        functools.partial(_in_proj_kernel, ug_cols=ug_cols, q_lora=q_lora, kv_lora=kv_lora, qk_rope=qk_rope),
        out_shape=(jax.ShapeDtypeStruct((M, ug_cols), F32),
                   jax.ShapeDtypeStruct((M, q_lora), BF16),
                   jax.ShapeDtypeStruct((M, kv_lora), F32),
                   jax.ShapeDtypeStruct((M, qk_rope), F32)),
        grid=(M // tm,),
        in_specs=[pl.BlockSpec((tm, D), row), _sel(w, j, single=True),
                  _sel(P['qg'], j), _sel(P['kg'], j),
                  pl.BlockSpec((tm, pe_w), pos), pl.BlockSpec((tm, pe_w), pos)],
        out_specs=(pl.BlockSpec((tm, ug_cols), row), pl.BlockSpec((tm, q_lora), row),
                   pl.BlockSpec((tm, kv_lora), row), pl.BlockSpec((tm, qk_rope), row)),
        compiler_params=_params("arbitrary"),
        name="in_proj",
    )(x, w, P['qg'], P['kg'], cos_t, sin_t)


def _lru_kernel(u_ref, gate_ref, cbuf_ref, h0_ref, cw_ref, cb_ref, gaw_ref, gab_ref,
                gxw_ref, gxb_ref, lam_ref, out_ref, hlast_ref,
                ubuf_ref, hcar_ref, a_ref, b_ref, *, tt, cw, heads):
    t = pl.program_id(1)
    pad = SUBLANES
    tail = cw - 1

    @pl.when(t == 0)
    def _():
        ubuf_ref[pad - tail:pad, :] = cbuf_ref[0]
        hcar_ref[...] = h0_ref[0]

    ubuf_ref[pad:pad + tt, :] = u_ref[...]
    rows_all = ubuf_ref[...]
    uc = cb_ref[...] + cw_ref[tail:cw, :] * rows_all[pad:, :]
    for k in range(tail):
        uc = uc + cw_ref[k:k + 1, :] * pltpu.roll(rows_all, tail - k, 0)[pad:, :]
    ubuf_ref[pad - tail:pad, :] = ubuf_ref[pad + tt - tail:pad + tt, :]

    width = uc.shape[1]
    blk = width // heads
    ucb = uc.astype(BF16)
    rs, igs = [], []
    for h in range(heads):
        uh = ucb[:, h * blk:(h + 1) * blk]
        rs.append(jnp.dot(uh, gaw_ref[h], preferred_element_type=F32))
        igs.append(jnp.dot(uh, gxw_ref[h], preferred_element_type=F32))
    r = jax.nn.sigmoid(jnp.concatenate(rs, axis=1) + gab_ref[...])
    ig = jax.nn.sigmoid(jnp.concatenate(igs, axis=1) + gxb_ref[...])
    nlam = -lam_ref[...]
    softplus = jnp.maximum(nlam, 0.0) + jnp.log1p(jnp.exp(-jnp.abs(nlam)))
    log_a = -LRU_C * r * softplus
    a = jnp.exp(log_a)
    a_ref[...] = a
    b_ref[...] = jnp.sqrt(-jnp.tanh(log_a) * (a * a + 1.0)) * (ig * uc)

    def step(i, h):
        h = a_ref[pl.ds(i, 1), :] * h + b_ref[pl.ds(i, 1), :]
        b_ref[pl.ds(i, 1), :] = h
        return h

    h_end = lax.fori_loop(0, tt, step, hcar_ref[...], unroll=8)
    hcar_ref[...] = h_end
    out_ref[...] = (b_ref[...] * jax.nn.gelu(gate_ref[...])).astype(out_ref.dtype)

    @pl.when(t == pl.num_programs(1) - 1)
    def _():
        hlast_ref[0] = h_end


def _lru(ug, conv_buf, h0, P, j, B, T, tt_pref=256):
    M = ug.shape[0]
    W = ug.shape[1] // 2
    heads = P['gaw'].shape[1]
    width = P['cw'].shape[1]
    tt = _tile(T, tt_pref)
    assert tt % SUBLANES == 0
    nt = T // tt
    row_u = lambda b, t: (b * nt + t, 0)
    row_g = lambda b, t: (b * nt + t, 1)
    names = ('cw', 'cb', 'gaw', 'gab', 'gxw', 'gxb', 'lam')
    return pl.pallas_call(
        functools.partial(_lru_kernel, tt=tt, cw=width, heads=heads),
        out_shape=(jax.ShapeDtypeStruct((M, W), BF16), jax.ShapeDtypeStruct((B, 1, W), F32)),
        grid=(B, nt),
        in_specs=[pl.BlockSpec((tt, W), row_u), pl.BlockSpec((tt, W), row_g),
                  pl.BlockSpec((1, width - 1, W), lambda b, t: (b, 0, 0)),
                  pl.BlockSpec((1, 1, W), lambda b, t: (b, 0, 0))] + [_sel(P[n], j) for n in names],
        out_specs=(pl.BlockSpec((tt, W), row_u), pl.BlockSpec((1, 1, W), lambda b, t: (b, 0, 0))),
        scratch_shapes=[pltpu.VMEM((SUBLANES + tt, W), F32), pltpu.VMEM((1, W), F32),
                        pltpu.VMEM((tt, W), F32), pltpu.VMEM((tt, W), F32)],
        compiler_params=_params("arbitrary", "arbitrary"),
        name="rg_lru",
    )(ug, ug, conv_buf, h0, *[P[n] for n in names])


def _mla_kernel(cqn_ref, wuq_ref, wukt_ref, cos_ref, sin_ref, ckv_ref, kpe_ref, wuv_ref, o_ref,
                qlat_ref, qpe_ref, m_ref, l_ref, acc_ref, sa_ref, sb_ref,
                *, tq, tk, hg, q_pos0, n_keys, scale):
    heads, nope, lat = wukt_ref.shape
    rope_d = qpe_ref.shape[2]
    vdim = wuv_ref.shape[2]
    rows = hg * tq
    n_groups = heads // hg
    assert n_groups == 1 or n_groups % 2 == 0

    q = jnp.dot(cqn_ref[...], wuq_ref[...], preferred_element_type=F32)
    qn = q[:, :heads * nope].astype(BF16)
    pe = _rope_lanes(q[:, heads * nope:], cos_ref[...], sin_ref[...])
    for h in range(heads):
        qlat_ref[h] = jnp.dot(qn[:, h * nope:(h + 1) * nope], wukt_ref[h],
                              preferred_element_type=F32).astype(BF16)
        qpe_ref[h] = pe[:, h * rope_d:(h + 1) * rope_d].astype(BF16)
    qi = pl.program_id(1)
    q_lo = q_pos0 + qi * tq
    vis_all = jnp.minimum(((q_lo >> CHUNK_SHIFT) + 1) * CHUNK, n_keys)
    vis_any = jnp.minimum((((q_lo + tq - 1) >> CHUNK_SHIFT) + 1) * CHUNK, n_keys)
    n_full = vis_all // tk
    n_tot = (vis_any + tk - 1) // tk

    m_ref[...] = jnp.full_like(m_ref, NEG_BIG)
    l_ref[...] = jnp.zeros_like(l_ref)
    acc_ref[...] = jnp.zeros_like(acc_ref)

    def block(kj, carry, masked):
        k0 = pl.multiple_of(kj * tk, tk)
        kc = ckv_ref[0, pl.ds(k0, tk), :]
        kp = kpe_ref[0, pl.ds(k0, tk), :]
        if masked:
            kpos = k0 + lax.broadcasted_iota(jnp.int32, (rows, tk), 1)
            qpos = q_lo + (lax.broadcasted_iota(jnp.int32, (rows, tk), 0) & (tq - 1))
            ok = ((kpos >> CHUNK_SHIFT) <= (qpos >> CHUNK_SHIFT)) & (kpos < n_keys)
        def scores(g):
            hs = pl.ds(g * hg, hg)
            ql = qlat_ref[hs].reshape(rows, lat)
            qp = qpe_ref[hs].reshape(rows, rope_d)
            s = lax.dot_general(ql, kc, NT_DIMS, preferred_element_type=F32)
            return s + lax.dot_general(qp, kp, NT_DIMS, preferred_element_type=F32)

        def update(g, s):
            if masked:
                s = jnp.where(ok, s, NEG_BIG)
            m_old = m_ref[g]
            m_new = jnp.maximum(m_old, jnp.max(s, axis=1, keepdims=True))
            corr = jnp.exp2((m_old - m_new) * (scale * LOG2_E))
            p = jnp.exp2((s - m_new) * (scale * LOG2_E))
            l_ref[g] = corr * l_ref[g] + jnp.sum(p, axis=1, keepdims=True)
            acc_ref[g] = corr * acc_ref[g] + jnp.dot(p.astype(BF16), kc, preferred_element_type=F32)
            m_ref[g] = m_new

        if n_groups == 1:
            update(0, scores(0))
            return carry

        sa_ref[...] = scores(0)

        def pair(i, c):
            g = 2 * i
            sb_ref[...] = scores(g + 1)
            update(g, sa_ref[...])
            sa_ref[...] = scores(jnp.minimum(g + 2, n_groups - 1))
            update(g + 1, sb_ref[...])
            return c

        lax.fori_loop(0, n_groups // 2, pair, 0)
        return carry

    lax.fori_loop(0, n_full, functools.partial(block, masked=False), 0)
    lax.fori_loop(n_full, n_tot, functools.partial(block, masked=True), 0)
    for h in range(heads):
        g, r0 = h // hg, (h % hg) * tq
        o = (acc_ref[g, r0:r0 + tq, :] / l_ref[g, r0:r0 + tq, :]).astype(BF16)
        o_ref[:, h * vdim:(h + 1) * vdim] = jnp.dot(
            o, wuv_ref[h], preferred_element_type=F32).astype(o_ref.dtype)


def _mla_attention(cqn, ckv_all, kpe_all, P, j, cos_t, sin_t, B, T, q_pos0, n_keys, tq_pref=512, tk_pref=512):
    wuq, wukt, wuv = P['wuq'], P['wukt'], P['wuv']
    _, heads, nope, lat = wukt.shape
    rope_d = P['rope_d']
    vdim = wuv.shape[3]
    q_lora = cqn.shape[1]
    Tk = ckv_all.shape[1]
    tq = _tile(T, tq_pref)
    tk = _tile(Tk, tk_pref)
    nt = T // tq
    hg = max(1, min(heads, MLA_STACK_ROWS // tq))
    assert heads % hg == 0 and tq & (tq - 1) == 0 and cos_t.shape[0] == T
    scale = (nope + rope_d) ** -0.5
    return pl.pallas_call(
        functools.partial(_mla_kernel, tq=tq, tk=tk, hg=hg, q_pos0=q_pos0, n_keys=n_keys, scale=scale),
        out_shape=jax.ShapeDtypeStruct((B * T, heads * vdim), BF16),
        grid=(B, nt),
        in_specs=[pl.BlockSpec((tq, q_lora), lambda b, t: (b * nt + t, 0)),
                  _sel(wuq, j), _sel(wukt, j),
                  pl.BlockSpec((tq, heads * rope_d), lambda b, t: (t, 0)),
                  pl.BlockSpec((tq, heads * rope_d), lambda b, t: (t, 0)),
                  pl.BlockSpec((1, Tk, lat), lambda b, t: (b, 0, 0)),
                  pl.BlockSpec((1, Tk, rope_d), lambda b, t: (b, 0, 0)),
                  _sel(wuv, j)],
        out_specs=pl.BlockSpec((tq, heads * vdim), lambda b, t: (b * nt + t, 0)),
        scratch_shapes=[pltpu.VMEM((heads, tq, lat), BF16), pltpu.VMEM((heads, tq, rope_d), BF16),
                        pltpu.VMEM((heads // hg, hg * tq, 1), F32), pltpu.VMEM((heads // hg, hg * tq, 1), F32),
                        pltpu.VMEM((heads // hg, hg * tq, lat), F32),
                        pltpu.VMEM((hg * tq, tk), F32), pltpu.VMEM((hg * tq, tk), F32)],
        compiler_params=_params("arbitrary", "arbitrary"),
        name="mla_attention",
    )(cqn, wuq, wukt, cos_t, sin_t, ckv_all, kpe_all, wuv)


def _xattn_kernel(q_ref, k_ref, v_ref, o_ref, *, heads, scale):
    hd = q_ref.shape[1] // heads
    for h in range(heads):
        sl = slice(h * hd, (h + 1) * hd)
        q = q_ref[:, sl]
        k = k_ref[:, sl].astype(BF16)
        v = v_ref[:, sl].astype(BF16)
        s = lax.dot_general(q, k, NT_DIMS, preferred_element_type=F32) * scale
        p = jnp.exp(s - jnp.max(s, axis=1, keepdims=True))
        l = jnp.sum(p, axis=1, keepdims=True)
        o = jnp.dot(p.astype(BF16), v, preferred_element_type=F32) / l
        o_ref[:, sl] = o.astype(o_ref.dtype)


def _xattn(q, mem_k, mem_v, l, B, T, tq_pref=512):
    M, D = q.shape
    n_mem = mem_k.shape[2]
    tq = _tile(T, tq_pref)
    nt = T // tq
    scale = (D // MEM_HEADS) ** -0.5
    mem_spec = pl.BlockSpec((None, None, n_mem, D), lambda b, t: (l, b, 0, 0))
    return pl.pallas_call(
        functools.partial(_xattn_kernel, heads=MEM_HEADS, scale=scale),
        out_shape=jax.ShapeDtypeStruct((M, D), BF16),
        grid=(B, nt),
        in_specs=[pl.BlockSpec((tq, D), lambda b, t: (b * nt + t, 0)), mem_spec, mem_spec],
        out_specs=pl.BlockSpec((tq, D), lambda b, t: (b * nt + t, 0)),
        compiler_params=_params("arbitrary", "arbitrary"),
        name="mem_xattn",
    )(q, mem_k, mem_v)


def _xattn_ln_kernel(q_ref, k_ref, v_ref, wo_ref, res_ref, g_ref, b_ref, o_ref, *, heads, scale, alpha, sub):
    hd = q_ref.shape[1] // heads
    kb = k_ref[...].astype(BF16)
    vb = v_ref[...].astype(BF16)
    for r0 in range(0, o_ref.shape[0], sub):
        rows = slice(r0, r0 + sub)
        outs = []
        for h in range(heads):
            sl = slice(h * hd, (h + 1) * hd)
            s = lax.dot_general(q_ref[rows, sl], kb[:, sl], NT_DIMS, preferred_element_type=F32) * scale
            p = jnp.exp(s - jnp.max(s, axis=1, keepdims=True))
            l = jnp.sum(p, axis=1, keepdims=True)
            outs.append((jnp.dot(p.astype(BF16), vb[:, sl], preferred_element_type=F32) / l).astype(BF16))
        y = jnp.dot(jnp.concatenate(outs, axis=1), wo_ref[...], preferred_element_type=F32)
        o_ref[rows, :] = _layer_norm(alpha * res_ref[rows, :] + y, g_ref[...], b_ref[...])


def _xattn_ln(q, mem_k, mem_v, l, wo, res, ln_g, ln_b, ln_i, alpha, B, T, tq_pref=512, sub_pref=256):
    M, D = q.shape
    n_mem = mem_k.shape[2]
    tq = _tile(T, tq_pref)
    sub = _tile(tq, sub_pref)
    nt = T // tq
    scale = (D // MEM_HEADS) ** -0.5
    row = lambda b, t: (b * nt + t, 0)
    mem_spec = pl.BlockSpec((None, None, n_mem, D), lambda b, t: (l, b, 0, 0))
    return pl.pallas_call(
        functools.partial(_xattn_ln_kernel, heads=MEM_HEADS, scale=scale, alpha=alpha, sub=sub),
        out_shape=jax.ShapeDtypeStruct((M, D), F32),
        grid=(B, nt),
        in_specs=[pl.BlockSpec((tq, D), row), mem_spec, mem_spec, _sel(wo, l, single=True),
                  pl.BlockSpec((tq, D), row), _sel(ln_g, ln_i), _sel(ln_b, ln_i)],
        out_specs=pl.BlockSpec((tq, D), row),
        compiler_params=_params("arbitrary", "arbitrary"),
        name="mem_xattn_out_ln",
    )(q, mem_k, mem_v, wo, res, ln_g, ln_b)


def _pool_ln_kernel(x_ref, pbuf_ref, pw_ref, ps_ref, g_ref, b_ref, o_ref, full_ref, y_ref,
                    *, tt, pos0, alpha):
    t = pl.program_id(1)
    halo = 2 * SUBLANES
    groups = pw_ref.shape[0]
    gw = pw_ref.shape[1]

    @pl.when(t == 0)
    def _():
        full_ref[0:halo, :] = pbuf_ref[0]

    full_ref[halo:halo + tt, :] = x_ref[...]
    pos = pos0 + t * tt + lax.broadcasted_iota(jnp.int32, (tt, 1), 0)
    for g in range(groups):
        w = POOL_WINDOWS[g]
        sl = slice(g * gw, (g + 1) * gw)
        f = full_ref[:, sl]
        s = f
        d = 1
        while d < w:
            s = s + pltpu.roll(s, d, 0)
            d *= 2
        cnt = jnp.minimum(pos + 1, w).astype(F32)
        dlt = (s[halo:] / cnt - f[halo:]).astype(BF16)
        y_ref[:, sl] = jnp.dot(dlt, pw_ref[g], preferred_element_type=F32)
    full_ref[0:halo, :] = full_ref[tt:tt + halo, :]
    o_ref[...] = _layer_norm(alpha * x_ref[...] + y_ref[...] * ps_ref[...], g_ref[...], b_ref[...])


def _pool_ln(x, pbuf16, P, j, ln_i, alpha, B, T, pos0, tt_pref=512):
    M, D = x.shape
    tt = _tile(T, tt_pref)
    nt = T // tt
    halo = 2 * SUBLANES
    assert max(POOL_WINDOWS) <= halo and tt >= halo
    row = lambda bb, t: (bb * nt + t, 0)
    return pl.pallas_call(
        functools.partial(_pool_ln_kernel, tt=tt, pos0=pos0, alpha=alpha),
        out_shape=jax.ShapeDtypeStruct((M, D), F32),
        grid=(B, nt),
        in_specs=[pl.BlockSpec((tt, D), row),
                  pl.BlockSpec((1, halo, D), lambda bb, t: (bb, 0, 0)),
                  _sel(P['pw'], j), _sel(P['ps'], j), _sel(P['ln_g'], ln_i), _sel(P['ln_b'], ln_i)],
        out_specs=pl.BlockSpec((tt, D), row),
        scratch_shapes=[pltpu.VMEM((halo + tt, D), F32), pltpu.VMEM((tt, D), F32)],
        compiler_params=_params("arbitrary", "arbitrary"),
        name="pool_res_ln",
    )(x, pbuf16, P['pw'], P['ps'], P['ln_g'], P['ln_b'])


def _rope_tables(pos, rope_d, heads):
    half = rope_d // 2
    inv = ROPE_THETA ** (-jnp.arange(half, dtype=F32) / half)
    ang = pos.astype(F32)[:, None] * inv[None, :]
    cos, sin = jnp.cos(ang), jnp.sin(ang)
    cos_t = jnp.tile(jnp.concatenate([cos, cos], axis=1), (1, heads))
    sin_t = jnp.tile(jnp.concatenate([-sin, sin], axis=1), (1, heads))
    return cos_t, sin_t


def _prep_weights(w_in, q_norm_g, w_uq, kv_norm_g, w_uk, w_uv, conv_w, conv_b,
                  gate_a_w, gate_a_b, gate_x_w, gate_x_b, lru_lambda, w_out,
                  pool_w, pool_scale, xa_wq, xa_wo, ln_g, ln_b):
    n_even = w_in.shape[0]
    d_model = ln_g.shape[2]
    q_lora = q_norm_g.shape[1]
    kv_lora = kv_norm_g.shape[1]
    heads, nope = w_uk.shape[2], w_uk.shape[3]
    rope_d = w_uq.shape[2] // heads - nope
    wuq = w_uq.reshape(n_even, q_lora, heads, nope + rope_d)
    wuq = jnp.concatenate([wuq[..., :nope].reshape(n_even, q_lora, heads * nope),
                           wuq[..., nope:].reshape(n_even, q_lora, heads * rope_d)], axis=2).astype(BF16)
    wukt = jnp.transpose(w_uk, (0, 2, 3, 1)).astype(BF16)
    wuv = jnp.transpose(w_uv, (0, 2, 1, 3)).astype(BF16)
    row = lambda a: a.reshape(a.shape[0], 1, -1)
    return dict(
        w_in=w_in.astype(BF16), qg=row(q_norm_g), kg=row(kv_norm_g), wuq=wuq, wukt=wukt, wuv=wuv,
        cw=conv_w, cb=row(conv_b), gaw=gate_a_w.astype(BF16), gab=row(gate_a_b),
        gxw=gate_x_w.astype(BF16), gxb=row(gate_x_b), lam=row(lru_lambda),
        w_out=w_out.astype(BF16), pw=pool_w.astype(BF16), ps=row(pool_scale),
        wq=xa_wq.astype(BF16), wo=xa_wo.astype(BF16),
        ln_g=ln_g.reshape(-1, 1, d_model), ln_b=ln_b.reshape(-1, 1, d_model),
        q_lora=q_lora, kv_lora=kv_lora, heads=heads, nope=nope, rope_d=rope_d)


N_NORMS = 3


class _Group:
    def __init__(self, x3, t_past, mem_k, mem_v, conv_buf, h0, ckv_past, kpe_past, pool_buf, P):
        self.B, self.T, self.D = x3.shape
        B, T = self.B, self.T
        self.x = x3.reshape(B * T, self.D)
        self.t_past, self.mem_k, self.mem_v = t_past, mem_k, mem_v
        self.conv_buf, self.h0, self.ckv_past, self.kpe_past, self.pool_buf = (
            conv_buf, h0, ckv_past, kpe_past, pool_buf)
        self.cos_t, self.sin_t = _rope_tables(t_past + jnp.arange(T), P['rope_d'], P['heads'])
        self.tm_in = _tile(T, IN_PROJ_ROWS) if T >= IN_PROJ_ROWS else B * T
        reps = max(1, self.tm_in // T)
        self.cos_rows = jnp.tile(self.cos_t, (reps, 1))[:, :LANES]
        self.sin_rows = jnp.tile(self.sin_t, (reps, 1))[:, :LANES]
        self.convs, self.hs, self.ckvs, self.kpes, self.pools = [], [], [], [], []

    def pre_mlp(self, l, P, alpha):
        B, T, D = self.B, self.T, self.D
        x, t_past, mem_k, mem_v = self.x, self.t_past, self.mem_k, self.mem_v
        conv_buf, h0, ckv_past, kpe_past, pool_buf = (
            self.conv_buf, self.h0, self.ckv_past, self.kpe_past, self.pool_buf)
        cos_t, sin_t = self.cos_t, self.sin_t
        convs, hs, ckvs, kpes, pools = self.convs, self.hs, self.ckvs, self.kpes, self.pools
        n_norms = N_NORMS
        ln_g, ln_b = P['ln_g'], P['ln_b']
        j = l // 2
        if l % 2 == 0:
            ug, cqn, ckv, kpe = _in_proj(x, P, j, self.cos_rows, self.sin_rows, self.tm_in)
            lru_w = ug.shape[1] // 2
            rec, h_last = _lru(ug, conv_buf[j], h0[j][:, None, :], P, j, B, T)
            ckv3 = ckv.reshape(B, T, -1)
            kpe3 = kpe.reshape(B, T, -1)
            n_keys = ckv_past[j].shape[1] + T
            tk = 512 if n_keys % 512 == 0 else 256
            padk = (-n_keys) % tk
            ckv_all = jnp.concatenate([ckv_past[j].astype(BF16), ckv3.astype(BF16),
                                       jnp.zeros((B, padk, ckv3.shape[2]), BF16)], axis=1)
            kpe_all = jnp.concatenate([kpe_past[j].astype(BF16), kpe3.astype(BF16),
                                       jnp.zeros((B, padk, kpe3.shape[2]), BF16)], axis=1)
            attn = _mla_attention(cqn, ckv_all, kpe_all, P, j, cos_t, sin_t, B, T, t_past, n_keys,
                                  tk_pref=tk)
            x = _mm_ln([rec, attn], P['w_out'], j, x, ln_g, ln_b, n_norms * l, alpha)
            tail = P['cw'].shape[1] - 1
            convs.append(ug.reshape(B, T, -1)[:, T - tail:, :lru_w])
            hs.append(h_last[:, 0, :])
            ckvs.append(ckv3)
            kpes.append(kpe3)
        else:
            nbuf = pool_buf[j].shape[1]
            pools.append(x.reshape(B, T, D)[:, T - nbuf:])
            pbuf16 = jnp.pad(pool_buf[j], ((0, 0), (2 * SUBLANES - nbuf, 0), (0, 0)))
            x = _pool_ln(x, pbuf16, P, j, n_norms * l, alpha, B, T, t_past)
        q = _rows_matmul(x, P['wq'], BF16, g0=l, n_g=1, tn_pref=None if B * T > 512 else 512)[0]
        if T >= XATTN_FUSE_ROWS:
            x = _xattn_ln(q, mem_k, mem_v, l, P['wo'], x, ln_g, ln_b, n_norms * l + 1, alpha, B, T)
        else:
            o = _xattn(q, mem_k, mem_v, l, B, T)
            x = _mm_ln([o], P['wo'], l, x, ln_g, ln_b, n_norms * l + 1, alpha)
        self.x = x

    def results(self):
        return (self.x.reshape(self.B, self.T, self.D), jnp.stack(self.convs), jnp.stack(self.hs),
                jnp.stack(self.ckvs), jnp.stack(self.kpes), jnp.stack(self.pools))


def kernel(x_prompt, x_sample, mem_prompt, cache_mem_k, cache_mem_v, cache_mla_ckv, cache_mla_kpe,
           state_rglru_h, state_rglru_conv, state_pool,
           w_in, q_norm_g, w_uq, kv_norm_g, w_uk, w_uv, conv_w, conv_b,
           gate_a_w, gate_a_b, gate_x_w, gate_x_b, lru_lambda, w_out,
           pool_w, pool_scale, xa_wq, xa_wk, xa_wv, xa_wo, mlp_up, mlp_down, ln_g, ln_b):
    P = _prep_weights(w_in, q_norm_g, w_uq, kv_norm_g, w_uk, w_uv, conv_w, conv_b,
                      gate_a_w, gate_a_b, gate_x_w, gate_x_b, lru_lambda, w_out,
                      pool_w, pool_scale, xa_wq, xa_wo, ln_g, ln_b)
    depth = ln_g.shape[0]
    n_even, n_odd = w_in.shape[0], pool_w.shape[0]
    bp, _, d_model = x_prompt.shape
    n_mem = mem_prompt.shape[1]
    dt = x_prompt.dtype
    lru_w = conv_w.shape[2]
    mem_rows = mem_prompt.reshape(bp * n_mem, d_model)
    p_mem_k = _rows_matmul(mem_rows, xa_wk, F32, tm_pref=1024, tn_pref=512).reshape(depth, bp, n_mem, d_model)
    p_mem_v = _rows_matmul(mem_rows, xa_wv, F32, tm_pref=1024, tn_pref=512).reshape(depth, bp, n_mem, d_model)
    prompt = _Group(
        x_prompt, 0, p_mem_k, p_mem_v,
        jnp.zeros((n_even, bp, conv_w.shape[1] - 1, lru_w), dt),
        jnp.zeros((n_even, bp, lru_w), dt),
        jnp.zeros((n_even, bp, 0, kv_norm_g.shape[1]), dt),
        jnp.zeros((n_even, bp, 0, cache_mla_kpe.shape[3]), dt),
        jnp.zeros((n_odd, bp, state_pool.shape[2], d_model), dt), P)
    sample = _Group(
        x_sample, cache_mla_ckv.shape[2], cache_mem_k, cache_mem_v,
        state_rglru_conv, state_rglru_h, cache_mla_ckv, cache_mla_kpe, state_pool, P)
    alpha = (2.0 * depth) ** 0.25
    ln_g3, ln_b3 = P['ln_g'], P['ln_b']
    for l in range(depth):
        sample.pre_mlp(l, P, alpha)
        prompt.pre_mlp(l, P, alpha)
        ln_i = N_NORMS * l + 2
        sample.x, up_b, down_b = _mlp_ln(sample.x, mlp_up, mlp_down, l, ln_g3, ln_b3, ln_i, alpha,
                                         emit=True, tf_pref=512)
        prompt.x = _mlp_ln(prompt.x, up_b, down_b, None, ln_g3, ln_b3, ln_i, alpha)
    y_prompt, p_conv, p_h, p_ckv, p_kpe, p_pool = prompt.results()
    y_sample, s_conv, s_h, s_ckv, s_kpe, s_pool = sample.results()
    return (y_prompt, y_sample, p_conv, p_h, p_ckv, p_kpe, p_pool, p_mem_k, p_mem_v,
            s_conv, s_h, s_ckv, s_kpe, s_pool)
```

```python
import functools

import jax
import jax.numpy as jnp
from jax import lax
from jax.experimental import pallas as pl
from jax.experimental.pallas import tpu as pltpu

F32 = jnp.float32
BF16 = jnp.bfloat16

CHUNK = 64
CHUNK_SHIFT = 6
assert 1 << CHUNK_SHIFT == CHUNK
MEM_HEADS = 4
POOL_WINDOWS = (2, 4, 8, 16)
LRU_C = 8.0
ROPE_THETA = 10000.0
LN_EPS = 1e-5
RMS_EPS = 1e-6
NEG_BIG = -1e30
LOG2_E = 1.4426950408889634

V7X_VMEM_BYTES = 64 * 1024 * 1024
VMEM_LIMIT = V7X_VMEM_BYTES - 8 * 1024 * 1024
LANES = 128
SUBLANES = 8
IN_PROJ_ROWS = 512
MLA_STACK_ROWS = 512
XATTN_FUSE_ROWS = 256
NT_DIMS = (((1,), (1,)), ((), ()))


def _params(*sem):
    return pltpu.CompilerParams(dimension_semantics=sem, vmem_limit_bytes=VMEM_LIMIT)


def _tile(n, pref):
    if n <= pref:
        return n
    t = pref
    while n % t:
        t -= SUBLANES
    assert t > 0
    return t


def _sel(arr, *lead, single=False):
    n_lead = len(lead)
    rest = arr.shape[n_lead:]
    idx = tuple(lead) + (0,) * len(rest)
    mode = dict(pipeline_mode=pl.Buffered(1)) if single else {}
    return pl.BlockSpec((None,) * n_lead + tuple(rest), lambda *_: idx, **mode)


def _layer_norm(y, g, b):
    mu = jnp.mean(y, axis=-1, keepdims=True)
    d = y - mu
    var = jnp.mean(d * d, axis=-1, keepdims=True)
    return d * lax.rsqrt(var + LN_EPS) * g + b


def _rms_norm(x, g):
    return x * lax.rsqrt(jnp.mean(x * x, axis=-1, keepdims=True) + RMS_EPS) * g


def _rope_lanes(x, cos_t, sin_t):
    n = x.shape[-1]
    half = 32
    lane = lax.broadcasted_iota(jnp.int32, x.shape, x.ndim - 1)
    first = (lane % (2 * half)) < half
    swapped = jnp.where(first, pltpu.roll(x, n - half, x.ndim - 1), pltpu.roll(x, half, x.ndim - 1))
    return x * cos_t + swapped * sin_t


def _rows_matmul_kernel(a_ref, w_ref, o_ref):
    a = a_ref[...].astype(BF16)
    w = w_ref[...].astype(BF16)
    o_ref[...] = jnp.dot(a, w, preferred_element_type=F32).astype(o_ref.dtype)


def _rows_matmul(a, w, out_dtype, g0=0, n_g=None, tm_pref=512, tn_pref=None):
    M, K = a.shape
    N = w.shape[2]
    n_g = w.shape[0] if n_g is None else n_g
    tm = _tile(M, tm_pref)
    tn = N if tn_pref is None else _tile(N, tn_pref)
    return pl.pallas_call(
        _rows_matmul_kernel,
        out_shape=jax.ShapeDtypeStruct((n_g, M, N), out_dtype),
        grid=(n_g, N // tn, M // tm),
        in_specs=[pl.BlockSpec((tm, K), lambda g, n, i: (i, 0)),
                  pl.BlockSpec((None, K, tn), lambda g, n, i: (g0 + g, 0, n))],
        out_specs=pl.BlockSpec((None, tm, tn), lambda g, n, i: (g, i, n)),
        compiler_params=_params("arbitrary", "arbitrary", "arbitrary"),
        name="rows_matmul",
    )(a, w)


def _mm_ln_kernel(*refs, n_a, alpha, sub, with_next):
    a_refs = refs[:n_a]
    if with_next:
        w_ref, res_ref, g_ref, b_ref, wn_ref, o_ref, on_ref = refs[n_a:]
    else:
        w_ref, res_ref, g_ref, b_ref, o_ref = refs[n_a:]
    normed = []
    for r0 in range(0, o_ref.shape[0], sub):
        rows = slice(r0, r0 + sub)
        y = None
        off = 0
        for a_ref in a_refs:
            k = a_ref.shape[1]
            part = jnp.dot(a_ref[rows, :].astype(BF16), w_ref[off:off + k, :], preferred_element_type=F32)
            y = part if y is None else y + part
            off += k
        xo = _layer_norm(alpha * res_ref[rows, :] + y, g_ref[...], b_ref[...])
        o_ref[rows, :] = xo
        normed.append((rows, xo))
    if with_next:
        for rows, xo in normed:
            on_ref[rows, :] = jnp.dot(xo.astype(BF16), wn_ref[...], preferred_element_type=F32).astype(on_ref.dtype)


def _mm_ln(a_list, w, wl, res, ln_g, ln_b, ln_i, alpha, w_next=None, next_l=None, tm_pref=512, sub_pref=256):
    M, N = res.shape
    tm = _tile(M, tm_pref)
    sub = _tile(tm, sub_pref)
    row = lambda i: (i, 0)
    in_specs = [pl.BlockSpec((tm, a.shape[1]), row) for a in a_list]
    in_specs += [_sel(w, wl, single=True), pl.BlockSpec((tm, N), row), _sel(ln_g, ln_i), _sel(ln_b, ln_i)]
    out_shape = [jax.ShapeDtypeStruct((M, N), F32)]
    out_specs = [pl.BlockSpec((tm, N), row)]
    operands = list(a_list) + [w, res, ln_g, ln_b]
    if w_next is not None:
        n_next = w_next.shape[2]
        in_specs.append(_sel(w_next, next_l, single=True))
        out_shape.append(jax.ShapeDtypeStruct((M, n_next), BF16))
        out_specs.append(pl.BlockSpec((tm, n_next), row))
        operands.append(w_next)
    out = pl.pallas_call(
        functools.partial(_mm_ln_kernel, n_a=len(a_list), alpha=alpha, sub=sub, with_next=w_next is not None),
        out_shape=out_shape,
        grid=(M // tm,),
        in_specs=in_specs,
        out_specs=out_specs,
        compiler_params=_params("arbitrary"),
        name="matmul_res_ln",
    )(*operands)
    return out[0] if w_next is None else out


def _mlp_ln_kernel(x_ref, wu_ref, wd_ref, g_ref, b_ref, o_ref, *rest, alpha, sub, emit, n_steps):
    if emit:
        wub_ref, wdb_ref, xb_ref, acc_ref = rest
    else:
        xb_ref, acc_ref = rest
    j = pl.program_id(1)
    last = n_steps - 1

    def step(first, final):
        wu = wu_ref[...].astype(BF16)
        wd = wd_ref[...].astype(BF16)
        if emit:
            wub_ref[...] = wu
            wdb_ref[...] = wd
        if first:
            xb = x_ref[...].astype(BF16)
            xb_ref[...] = xb
        else:
            xb = xb_ref[...]
        h = jnp.dot(xb, wu, preferred_element_type=F32)
        h = jnp.square(jnp.maximum(h, 0.0)).astype(BF16)
        if not final:
            part = jnp.dot(h, wd, preferred_element_type=F32)
            if first:
                acc_ref[...] = part
            else:
                acc_ref[...] += part
            return
        for r0 in range(0, o_ref.shape[0], sub):
            rows = slice(r0, r0 + sub)
            y = jnp.dot(h[rows, :], wd, preferred_element_type=F32)
            if not first:
                y = acc_ref[rows, :] + y
            o_ref[rows, :] = _layer_norm(alpha * x_ref[rows, :] + y, g_ref[...], b_ref[...])

    if n_steps == 1:
        step(True, True)
    else:
        pl.when(j == 0)(functools.partial(step, True, False))
        if n_steps > 2:
            pl.when((j > 0) & (j < last))(functools.partial(step, False, False))
        pl.when(j == last)(functools.partial(step, False, True))


def _mlp_ln(x, w_up, w_down, l, ln_g, ln_b, ln_i, alpha, emit=False, tm_pref=512, tf_pref=1024,
            sub_pref=256):
    M, D = x.shape
    FF = w_up.shape[-1]
    tm = _tile(M, tm_pref)
    tf = _tile(FF, tf_pref)
    sub = _tile(tm, sub_pref)
    out_shape = [jax.ShapeDtypeStruct((M, D), F32)]
    out_specs = [pl.BlockSpec((tm, D), lambda i, j: (i, 0))]
    if emit:
        assert M == tm
        w_specs = [pl.BlockSpec((None, D, tf), lambda i, j: (l, 0, j)),
                   pl.BlockSpec((None, tf, D), lambda i, j: (l, j, 0))]
        out_shape += [jax.ShapeDtypeStruct((D, FF), BF16), jax.ShapeDtypeStruct((FF, D), BF16)]
        out_specs += [pl.BlockSpec((D, tf), lambda i, j: (0, j)), pl.BlockSpec((tf, D), lambda i, j: (j, 0))]
    else:
        w_specs = [pl.BlockSpec((D, tf), lambda i, j: (0, j)), pl.BlockSpec((tf, D), lambda i, j: (j, 0))]
    out = pl.pallas_call(
        functools.partial(_mlp_ln_kernel, alpha=alpha, sub=sub, emit=emit, n_steps=FF // tf),
        out_shape=out_shape,
        grid=(M // tm, FF // tf),
        in_specs=[pl.BlockSpec((tm, D), lambda i, j: (i, 0))] + w_specs + [_sel(ln_g, ln_i), _sel(ln_b, ln_i)],
        out_specs=out_specs,
        scratch_shapes=[pltpu.VMEM((tm, D), BF16), pltpu.VMEM((tm, D), F32)],
        compiler_params=_params("arbitrary", "arbitrary"),
        name="mlp_res_ln_round" if emit else "mlp_res_ln",
    )(x, w_up, w_down, ln_g, ln_b)
    return out if emit else out[0]


def _in_proj_kernel(x_ref, w_ref, qg_ref, kg_ref, cos_ref, sin_ref,
                    ug_ref, cqn_ref, ckv_ref, kpe_ref, *, ug_cols, q_lora, kv_lora, qk_rope):
    xb = x_ref[...].astype(BF16)
    z = jnp.dot(xb, w_ref[:, ug_cols:], preferred_element_type=F32)
    cqn_ref[...] = _rms_norm(z[:, :q_lora], qg_ref[...]).astype(BF16)
    ckv_ref[...] = _rms_norm(z[:, q_lora:q_lora + kv_lora], kg_ref[...])
    pe = z[:, q_lora + kv_lora:]
    pad = cos_ref.shape[1] - qk_rope
    pe = jnp.concatenate([pe, jnp.zeros((pe.shape[0], pad), F32)], axis=1)
    kpe_ref[...] = _rope_lanes(pe, cos_ref[...], sin_ref[...])[:, :qk_rope]
    ug_ref[...] = jnp.dot(xb, w_ref[:, :ug_cols], preferred_element_type=F32)


def _in_proj(x, P, j, cos_t, sin_t, tm):
    M, D = x.shape
    w = P['w_in']
    q_lora, kv_lora, qk_rope = P['q_lora'], P['kv_lora'], P['rope_d']
    ug_cols = w.shape[2] - q_lora - kv_lora - qk_rope
    assert M % tm == 0 and cos_t.shape[0] % tm == 0 and ug_cols % LANES == 0
    n_pos_blocks = cos_t.shape[0] // tm
    pe_w = cos_t.shape[1]
    row = lambda i: (i, 0)
    pos = lambda i: (i % n_pos_blocks, 0)
    return pl.pallas_call(
        functools.partial(_in_proj_kernel, ug_cols=ug_cols, q_lora=q_lora, kv_lora=kv_lora, qk_rope=qk_rope),
        out_shape=(jax.ShapeDtypeStruct((M, ug_cols), F32),
                   jax.ShapeDtypeStruct((M, q_lora), BF16),
                   jax.ShapeDtypeStruct((M, kv_lora), F32),
                   jax.ShapeDtypeStruct((M, qk_rope), F32)),
        grid=(M // tm,),
        in_specs=[pl.BlockSpec((tm, D), row), _sel(w, j, single=True),
                  _sel(P['qg'], j), _sel(P['kg'], j),
                  pl.BlockSpec((tm, pe_w), pos), pl.BlockSpec((tm, pe_w), pos)],
        out_specs=(pl.BlockSpec((tm, ug_cols), row), pl.BlockSpec((tm, q_lora), row),
                   pl.BlockSpec((tm, kv_lora), row), pl.BlockSpec((tm, qk_rope), row)),
        compiler_params=_params("arbitrary"),
        name="in_proj",
    )(x, w, P['qg'], P['kg'], cos_t, sin_t)


def _lru_kernel(u_ref, gate_ref, cbuf_ref, h0_ref, cw_ref, cb_ref, gaw_ref, gab_ref,
                gxw_ref, gxb_ref, lam_ref, out_ref, hlast_ref,
                ubuf_ref, hcar_ref, a_ref, b_ref, *, tt, cw, heads):
    t = pl.program_id(1)
    pad = SUBLANES
    tail = cw - 1

    @pl.when(t == 0)
    def _():
        ubuf_ref[pad - tail:pad, :] = cbuf_ref[0]
        hcar_ref[...] = h0_ref[0]

    ubuf_ref[pad:pad + tt, :] = u_ref[...]
    rows_all = ubuf_ref[...]
    uc = cb_ref[...] + cw_ref[tail:cw, :] * rows_all[pad:, :]
    for k in range(tail):
        uc = uc + cw_ref[k:k + 1, :] * pltpu.roll(rows_all, tail - k, 0)[pad:, :]
    ubuf_ref[pad - tail:pad, :] = ubuf_ref[pad + tt - tail:pad + tt, :]

    width = uc.shape[1]
    blk = width // heads
    ucb = uc.astype(BF16)
    rs, igs = [], []
    for h in range(heads):
        uh = ucb[:, h * blk:(h + 1) * blk]
        rs.append(jnp.dot(uh, gaw_ref[h], preferred_element_type=F32))
        igs.append(jnp.dot(uh, gxw_ref[h], preferred_element_type=F32))
    r = jax.nn.sigmoid(jnp.concatenate(rs, axis=1) + gab_ref[...])
    ig = jax.nn.sigmoid(jnp.concatenate(igs, axis=1) + gxb_ref[...])
    nlam = -lam_ref[...]
    softplus = jnp.maximum(nlam, 0.0) + jnp.log1p(jnp.exp(-jnp.abs(nlam)))
    log_a = -LRU_C * r * softplus
    a = jnp.exp(log_a)
    a_ref[...] = a
    b_ref[...] = jnp.sqrt(-jnp.tanh(log_a) * (a * a + 1.0)) * (ig * uc)

    def step(i, h):
        h = a_ref[pl.ds(i, 1), :] * h + b_ref[pl.ds(i, 1), :]
        b_ref[pl.ds(i, 1), :] = h
        return h

    h_end = lax.fori_loop(0, tt, step, hcar_ref[...], unroll=8)
    hcar_ref[...] = h_end
    out_ref[...] = (b_ref[...] * jax.nn.gelu(gate_ref[...])).astype(out_ref.dtype)

    @pl.when(t == pl.num_programs(1) - 1)
    def _():
        hlast_ref[0] = h_end


def _lru(ug, conv_buf, h0, P, j, B, T, tt_pref=256):
    M = ug.shape[0]
    W = ug.shape[1] // 2
    heads = P['gaw'].shape[1]
    width = P['cw'].shape[1]
    tt = _tile(T, tt_pref)
    assert tt % SUBLANES == 0
    nt = T // tt
    row_u = lambda b, t: (b * nt + t, 0)
    row_g = lambda b, t: (b * nt + t, 1)
    names = ('cw', 'cb', 'gaw', 'gab', 'gxw', 'gxb', 'lam')
    return pl.pallas_call(
        functools.partial(_lru_kernel, tt=tt, cw=width, heads=heads),
        out_shape=(jax.ShapeDtypeStruct((M, W), BF16), jax.ShapeDtypeStruct((B, 1, W), F32)),
        grid=(B, nt),
        in_specs=[pl.BlockSpec((tt, W), row_u), pl.BlockSpec((tt, W), row_g),
                  pl.BlockSpec((1, width - 1, W), lambda b, t: (b, 0, 0)),
                  pl.BlockSpec((1, 1, W), lambda b, t: (b, 0, 0))] + [_sel(P[n], j) for n in names],
        out_specs=(pl.BlockSpec((tt, W), row_u), pl.BlockSpec((1, 1, W), lambda b, t: (b, 0, 0))),
        scratch_shapes=[pltpu.VMEM((SUBLANES + tt, W), F32), pltpu.VMEM((1, W), F32),
                        pltpu.VMEM((tt, W), F32), pltpu.VMEM((tt, W), F32)],
        compiler_params=_params("arbitrary", "arbitrary"),
        name="rg_lru",
    )(ug, ug, conv_buf, h0, *[P[n] for n in names])


def _mla_kernel(cqn_ref, wuq_ref, wukt_ref, cos_ref, sin_ref, ckv_ref, kpe_ref, wuv_ref, o_ref,
                qlat_ref, qpe_ref, m_ref, l_ref, acc_ref, sa_ref, sb_ref,
                *, tq, tk, hg, q_pos0, n_keys, scale):
    heads, nope, lat = wukt_ref.shape
    rope_d = qpe_ref.shape[2]
    vdim = wuv_ref.shape[2]
    rows = hg * tq
    n_groups = heads // hg
    assert n_groups == 1 or n_groups % 2 == 0

    q = jnp.dot(cqn_ref[...], wuq_ref[...], preferred_element_type=F32)
    qn = q[:, :heads * nope].astype(BF16)
    pe = _rope_lanes(q[:, heads * nope:], cos_ref[...], sin_ref[...])
    for h in range(heads):
        qlat_ref[h] = jnp.dot(qn[:, h * nope:(h + 1) * nope], wukt_ref[h],
                              preferred_element_type=F32).astype(BF16)
        qpe_ref[h] = pe[:, h * rope_d:(h + 1) * rope_d].astype(BF16)
    qi = pl.program_id(1)
    q_lo = q_pos0 + qi * tq
    vis_all = jnp.minimum(((q_lo >> CHUNK_SHIFT) + 1) * CHUNK, n_keys)
    vis_any = jnp.minimum((((q_lo + tq - 1) >> CHUNK_SHIFT) + 1) * CHUNK, n_keys)
    n_full = vis_all // tk
    n_tot = (vis_any + tk - 1) // tk

    m_ref[...] = jnp.full_like(m_ref, NEG_BIG)
    l_ref[...] = jnp.zeros_like(l_ref)
    acc_ref[...] = jnp.zeros_like(acc_ref)

    def block(kj, carry, masked):
        k0 = pl.multiple_of(kj * tk, tk)
        kc = ckv_ref[0, pl.ds(k0, tk), :]
        kp = kpe_ref[0, pl.ds(k0, tk), :]
        if masked:
            kpos = k0 + lax.broadcasted_iota(jnp.int32, (rows, tk), 1)
            qpos = q_lo + (lax.broadcasted_iota(jnp.int32, (rows, tk), 0) & (tq - 1))
            ok = ((kpos >> CHUNK_SHIFT) <= (qpos >> CHUNK_SHIFT)) & (kpos < n_keys)
        def scores(g):
            hs = pl.ds(g * hg, hg)
            ql = qlat_ref[hs].reshape(rows, lat)
            qp = qpe_ref[hs].reshape(rows, rope_d)
            s = lax.dot_general(ql, kc, NT_DIMS, preferred_element_type=F32)
            return s + lax.dot_general(qp, kp, NT_DIMS, preferred_element_type=F32)

        def update(g, s):
            if masked:
                s = jnp.where(ok, s, NEG_BIG)
            m_old = m_ref[g]
            m_new = jnp.maximum(m_old, jnp.max(s, axis=1, keepdims=True))
            corr = jnp.exp2((m_old - m_new) * (scale * LOG2_E))
            p = jnp.exp2((s - m_new) * (scale * LOG2_E))
            l_ref[g] = corr * l_ref[g] + jnp.sum(p, axis=1, keepdims=True)
            acc_ref[g] = corr * acc_ref[g] + jnp.dot(p.astype(BF16), kc, preferred_element_type=F32)
            m_ref[g] = m_new

        if n_groups == 1:
            update(0, scores(0))
            return carry

        sa_ref[...] = scores(0)

        def pair(i, c):
            g = 2 * i
            sb_ref[...] = scores(g + 1)
            update(g, sa_ref[...])
            sa_ref[...] = scores(jnp.minimum(g + 2, n_groups - 1))
            update(g + 1, sb_ref[...])
            return c

        lax.fori_loop(0, n_groups // 2, pair, 0)
        return carry

    lax.fori_loop(0, n_full, functools.partial(block, masked=False), 0)
    lax.fori_loop(n_full, n_tot, functools.partial(block, masked=True), 0)
    for h in range(heads):
        g, r0 = h // hg, (h % hg) * tq
        o = (acc_ref[g, r0:r0 + tq, :] / l_ref[g, r0:r0 + tq, :]).astype(BF16)
        o_ref[:, h * vdim:(h + 1) * vdim] = jnp.dot(
            o, wuv_ref[h], preferred_element_type=F32).astype(o_ref.dtype)


def _mla_attention(cqn, ckv_all, kpe_all, P, j, cos_t, sin_t, B, T, q_pos0, n_keys, tq_pref=512, tk_pref=512):
    wuq, wukt, wuv = P['wuq'], P['wukt'], P['wuv']
    _, heads, nope, lat = wukt.shape
    rope_d = P['rope_d']
    vdim = wuv.shape[3]
    q_lora = cqn.shape[1]
    Tk = ckv_all.shape[1]
    tq = _tile(T, tq_pref)
    tk = _tile(Tk, tk_pref)
    nt = T // tq
    hg = max(1, min(heads, MLA_STACK_ROWS // tq))
    assert heads % hg == 0 and tq & (tq - 1) == 0 and cos_t.shape[0] == T
    scale = (nope + rope_d) ** -0.5
    return pl.pallas_call(
        functools.partial(_mla_kernel, tq=tq, tk=tk, hg=hg, q_pos0=q_pos0, n_keys=n_keys, scale=scale),
        out_shape=jax.ShapeDtypeStruct((B * T, heads * vdim), BF16),
        grid=(B, nt),
        in_specs=[pl.BlockSpec((tq, q_lora), lambda b, t: (b * nt + t, 0)),
                  _sel(wuq, j), _sel(wukt, j),
                  pl.BlockSpec((tq, heads * rope_d), lambda b, t: (t, 0)),
                  pl.BlockSpec((tq, heads * rope_d), lambda b, t: (t, 0)),
                  pl.BlockSpec((1, Tk, lat), lambda b, t: (b, 0, 0)),
                  pl.BlockSpec((1, Tk, rope_d), lambda b, t: (b, 0, 0)),
                  _sel(wuv, j)],
        out_specs=pl.BlockSpec((tq, heads * vdim), lambda b, t: (b * nt + t, 0)),
        scratch_shapes=[pltpu.VMEM((heads, tq, lat), BF16), pltpu.VMEM((heads, tq, rope_d), BF16),
                        pltpu.VMEM((heads // hg, hg * tq, 1), F32), pltpu.VMEM((heads // hg, hg * tq, 1), F32),
                        pltpu.VMEM((heads // hg, hg * tq, lat), F32),
                        pltpu.VMEM((hg * tq, tk), F32), pltpu.VMEM((hg * tq, tk), F32)],
        compiler_params=_params("arbitrary", "arbitrary"),
        name="mla_attention",
    )(cqn, wuq, wukt, cos_t, sin_t, ckv_all, kpe_all, wuv)


def _xattn_kernel(q_ref, k_ref, v_ref, o_ref, *, heads, scale):
    hd = q_ref.shape[1] // heads
    for h in range(heads):
        sl = slice(h * hd, (h + 1) * hd)
        q = q_ref[:, sl]
        k = k_ref[:, sl].astype(BF16)
        v = v_ref[:, sl].astype(BF16)
        s = lax.dot_general(q, k, NT_DIMS, preferred_element_type=F32) * scale
        p = jnp.exp(s - jnp.max(s, axis=1, keepdims=True))
        l = jnp.sum(p, axis=1, keepdims=True)
        o = jnp.dot(p.astype(BF16), v, preferred_element_type=F32) / l
        o_ref[:, sl] = o.astype(o_ref.dtype)


def _xattn(q, mem_k, mem_v, l, B, T, tq_pref=512):
    M, D = q.shape
    n_mem = mem_k.shape[2]
    tq = _tile(T, tq_pref)
    nt = T // tq
    scale = (D // MEM_HEADS) ** -0.5
    mem_spec = pl.BlockSpec((None, None, n_mem, D), lambda b, t: (l, b, 0, 0))
    return pl.pallas_call(
        functools.partial(_xattn_kernel, heads=MEM_HEADS, scale=scale),
        out_shape=jax.ShapeDtypeStruct((M, D), BF16),
        grid=(B, nt),
        in_specs=[pl.BlockSpec((tq, D), lambda b, t: (b * nt + t, 0)), mem_spec, mem_spec],
        out_specs=pl.BlockSpec((tq, D), lambda b, t: (b * nt + t, 0)),
        compiler_params=_params("arbitrary", "arbitrary"),
        name="mem_xattn",
    )(q, mem_k, mem_v)


def _xattn_ln_kernel(q_ref, k_ref, v_ref, wo_ref, res_ref, g_ref, b_ref, o_ref, *, heads, scale, alpha, sub):
    hd = q_ref.shape[1] // heads
    kb = k_ref[...].astype(BF16)
    vb = v_ref[...].astype(BF16)
    for r0 in range(0, o_ref.shape[0], sub):
        rows = slice(r0, r0 + sub)
        outs = []
        for h in range(heads):
            sl = slice(h * hd, (h + 1) * hd)
            s = lax.dot_general(q_ref[rows, sl], kb[:, sl], NT_DIMS, preferred_element_type=F32) * scale
            p = jnp.exp(s - jnp.max(s, axis=1, keepdims=True))
            l = jnp.sum(p, axis=1, keepdims=True)
            outs.append((jnp.dot(p.astype(BF16), vb[:, sl], preferred_element_type=F32) / l).astype(BF16))
        y = jnp.dot(jnp.concatenate(outs, axis=1), wo_ref[...], preferred_element_type=F32)
        o_ref[rows, :] = _layer_norm(alpha * res_ref[rows, :] + y, g_ref[...], b_ref[...])


def _xattn_ln(q, mem_k, mem_v, l, wo, res, ln_g, ln_b, ln_i, alpha, B, T, tq_pref=512, sub_pref=256):
    M, D = q.shape
    n_mem = mem_k.shape[2]
    tq = _tile(T, tq_pref)
    sub = _tile(tq, sub_pref)
    nt = T // tq
    scale = (D // MEM_HEADS) ** -0.5
    row = lambda b, t: (b * nt + t, 0)
    mem_spec = pl.BlockSpec((None, None, n_mem, D), lambda b, t: (l, b, 0, 0))
    return pl.pallas_call(
        functools.partial(_xattn_ln_kernel, heads=MEM_HEADS, scale=scale, alpha=alpha, sub=sub),
        out_shape=jax.ShapeDtypeStruct((M, D), F32),
        grid=(B, nt),
        in_specs=[pl.BlockSpec((tq, D), row), mem_spec, mem_spec, _sel(wo, l, single=True),
                  pl.BlockSpec((tq, D), row), _sel(ln_g, ln_i), _sel(ln_b, ln_i)],
        out_specs=pl.BlockSpec((tq, D), row),
        compiler_params=_params("arbitrary", "arbitrary"),
        name="mem_xattn_out_ln",
    )(q, mem_k, mem_v, wo, res, ln_g, ln_b)


def _pool_ln_kernel(x_ref, pbuf_ref, pw_ref, ps_ref, g_ref, b_ref, wn_ref, o_ref, on_ref, full_ref, y_ref,
                    *, tt, sub, pos0, alpha):
    t = pl.program_id(1)
    halo = 2 * SUBLANES
    groups = pw_ref.shape[0]
    gw = pw_ref.shape[1]

    @pl.when(t == 0)
    def _():
        full_ref[0:halo, :] = pbuf_ref[0]

    full_ref[halo:halo + tt, :] = x_ref[...]
    normed = []
    for r0 in range(0, tt, sub):
        rows = slice(r0, r0 + sub)
        pos = pos0 + t * tt + r0 + lax.broadcasted_iota(jnp.int32, (sub, 1), 0)
        for g in range(groups):
            w = POOL_WINDOWS[g]
            sl = slice(g * gw, (g + 1) * gw)
            f = full_ref[r0:r0 + halo + sub, sl]
            s = f
            d = 1
            while d < w:
                s = s + pltpu.roll(s, d, 0)
                d *= 2
            cnt = jnp.minimum(pos + 1, w).astype(F32)
            dlt = (s[halo:] / cnt - f[halo:]).astype(BF16)
            y_ref[rows, sl] = jnp.dot(dlt, pw_ref[g], preferred_element_type=F32)
        xo = _layer_norm(alpha * x_ref[rows, :] + y_ref[rows, :] * ps_ref[...], g_ref[...], b_ref[...])
        o_ref[rows, :] = xo
        normed.append((rows, xo))
    full_ref[0:halo, :] = full_ref[tt:tt + halo, :]
    for rows, xo in normed:
        on_ref[rows, :] = jnp.dot(xo.astype(BF16), wn_ref[...], preferred_element_type=F32).astype(on_ref.dtype)


def _pool_ln(x, pbuf16, P, j, ln_i, alpha, B, T, pos0, w_next, next_l, tt_pref=512, sub_pref=256):
    M, D = x.shape
    tt = _tile(T, tt_pref)
    sub = _tile(tt, sub_pref)
    nt = T // tt
    halo = 2 * SUBLANES
    n_next = w_next.shape[2]
    assert max(POOL_WINDOWS) <= halo and tt >= halo
    row = lambda bb, t: (bb * nt + t, 0)
    return pl.pallas_call(
        functools.partial(_pool_ln_kernel, tt=tt, sub=sub, pos0=pos0, alpha=alpha),
        out_shape=(jax.ShapeDtypeStruct((M, D), F32), jax.ShapeDtypeStruct((M, n_next), BF16)),
        grid=(B, nt),
        in_specs=[pl.BlockSpec((tt, D), row),
                  pl.BlockSpec((1, halo, D), lambda bb, t: (bb, 0, 0)),
                  _sel(P['pw'], j), _sel(P['ps'], j), _sel(P['ln_g'], ln_i), _sel(P['ln_b'], ln_i),
                  _sel(w_next, next_l, single=True)],
        out_specs=(pl.BlockSpec((tt, D), row), pl.BlockSpec((tt, n_next), row)),
        scratch_shapes=[pltpu.VMEM((halo + tt, D), F32), pltpu.VMEM((tt, D), F32)],
        compiler_params=_params("arbitrary", "arbitrary"),
        name="pool_res_ln",
    )(x, pbuf16, P['pw'], P['ps'], P['ln_g'], P['ln_b'], w_next)


def _rope_tables(pos, rope_d, heads):
    half = rope_d // 2
    inv = ROPE_THETA ** (-jnp.arange(half, dtype=F32) / half)
    ang = pos.astype(F32)[:, None] * inv[None, :]
    cos, sin = jnp.cos(ang), jnp.sin(ang)
    cos_t = jnp.tile(jnp.concatenate([cos, cos], axis=1), (1, heads))
    sin_t = jnp.tile(jnp.concatenate([-sin, sin], axis=1), (1, heads))
    return cos_t, sin_t


def _prep_weights(w_in, q_norm_g, w_uq, kv_norm_g, w_uk, w_uv, conv_w, conv_b,
                  gate_a_w, gate_a_b, gate_x_w, gate_x_b, lru_lambda, w_out,
                  pool_w, pool_scale, xa_wq, xa_wo, ln_g, ln_b):
    n_even = w_in.shape[0]
    d_model = ln_g.shape[2]
    q_lora = q_norm_g.shape[1]
    kv_lora = kv_norm_g.shape[1]
    heads, nope = w_uk.shape[2], w_uk.shape[3]
    rope_d = w_uq.shape[2] // heads - nope
    wuq = w_uq.reshape(n_even, q_lora, heads, nope + rope_d)
    wuq = jnp.concatenate([wuq[..., :nope].reshape(n_even, q_lora, heads * nope),
                           wuq[..., nope:].reshape(n_even, q_lora, heads * rope_d)], axis=2).astype(BF16)
    wukt = jnp.transpose(w_uk, (0, 2, 3, 1)).astype(BF16)
    wuv = jnp.transpose(w_uv, (0, 2, 1, 3)).astype(BF16)
    row = lambda a: a.reshape(a.shape[0], 1, -1)
    return dict(
        w_in=w_in.astype(BF16), qg=row(q_norm_g), kg=row(kv_norm_g), wuq=wuq, wukt=wukt, wuv=wuv,
        cw=conv_w, cb=row(conv_b), gaw=gate_a_w.astype(BF16), gab=row(gate_a_b),
        gxw=gate_x_w.astype(BF16), gxb=row(gate_x_b), lam=row(lru_lambda),
        w_out=w_out.astype(BF16), pw=pool_w.astype(BF16), ps=row(pool_scale),
        wq=xa_wq.astype(BF16), wo=xa_wo.astype(BF16),
        ln_g=ln_g.reshape(-1, 1, d_model), ln_b=ln_b.reshape(-1, 1, d_model),
        q_lora=q_lora, kv_lora=kv_lora, heads=heads, nope=nope, rope_d=rope_d)


N_NORMS = 3


class _Group:
    def __init__(self, x3, t_past, mem_k, mem_v, conv_buf, h0, ckv_past, kpe_past, pool_buf, P):
        self.B, self.T, self.D = x3.shape
        B, T = self.B, self.T
        self.x = x3.reshape(B * T, self.D)
        self.t_past, self.mem_k, self.mem_v = t_past, mem_k, mem_v
        self.conv_buf, self.h0, self.ckv_past, self.kpe_past, self.pool_buf = (
            conv_buf, h0, ckv_past, kpe_past, pool_buf)
        self.cos_t, self.sin_t = _rope_tables(t_past + jnp.arange(T), P['rope_d'], P['heads'])
        self.tm_in = _tile(T, IN_PROJ_ROWS) if T >= IN_PROJ_ROWS else B * T
        reps = max(1, self.tm_in // T)
        self.cos_rows = jnp.tile(self.cos_t, (reps, 1))[:, :LANES]
        self.sin_rows = jnp.tile(self.sin_t, (reps, 1))[:, :LANES]
        self.convs, self.hs, self.ckvs, self.kpes, self.pools = [], [], [], [], []

    def pre_mlp(self, l, P, alpha):
        B, T, D = self.B, self.T, self.D
        x, t_past, mem_k, mem_v = self.x, self.t_past, self.mem_k, self.mem_v
        conv_buf, h0, ckv_past, kpe_past, pool_buf = (
            self.conv_buf, self.h0, self.ckv_past, self.kpe_past, self.pool_buf)
        cos_t, sin_t = self.cos_t, self.sin_t
        convs, hs, ckvs, kpes, pools = self.convs, self.hs, self.ckvs, self.kpes, self.pools
        n_norms = N_NORMS
        ln_g, ln_b = P['ln_g'], P['ln_b']
        j = l // 2
        if l % 2 == 0:
            ug, cqn, ckv, kpe = _in_proj(x, P, j, self.cos_rows, self.sin_rows, self.tm_in)
            lru_w = ug.shape[1] // 2
            rec, h_last = _lru(ug, conv_buf[j], h0[j][:, None, :], P, j, B, T)
            ckv3 = ckv.reshape(B, T, -1)
            kpe3 = kpe.reshape(B, T, -1)
            n_keys = ckv_past[j].shape[1] + T
            tk = 512 if n_keys % 512 == 0 else 256
            padk = (-n_keys) % tk
            ckv_all = jnp.concatenate([ckv_past[j].astype(BF16), ckv3.astype(BF16),
                                       jnp.zeros((B, padk, ckv3.shape[2]), BF16)], axis=1)
            kpe_all = jnp.concatenate([kpe_past[j].astype(BF16), kpe3.astype(BF16),
                                       jnp.zeros((B, padk, kpe3.shape[2]), BF16)], axis=1)
            attn = _mla_attention(cqn, ckv_all, kpe_all, P, j, cos_t, sin_t, B, T, t_past, n_keys,
                                  tk_pref=tk)
            x, q = _mm_ln([rec, attn], P['w_out'], j, x, ln_g, ln_b, n_norms * l, alpha,
                          w_next=P['wq'], next_l=l)
            tail = P['cw'].shape[1] - 1
            convs.append(ug.reshape(B, T, -1)[:, T - tail:, :lru_w])
            hs.append(h_last[:, 0, :])
            ckvs.append(ckv3)
            kpes.append(kpe3)
        else:
            nbuf = pool_buf[j].shape[1]
            pools.append(x.reshape(B, T, D)[:, T - nbuf:])
            pbuf16 = jnp.pad(pool_buf[j], ((0, 0), (2 * SUBLANES - nbuf, 0), (0, 0)))
            x, q = _pool_ln(x, pbuf16, P, j, n_norms * l, alpha, B, T, t_past, P['wq'], l)
        if T >= XATTN_FUSE_ROWS:
            x = _xattn_ln(q, mem_k, mem_v, l, P['wo'], x, ln_g, ln_b, n_norms * l + 1, alpha, B, T)
        else:
            o = _xattn(q, mem_k, mem_v, l, B, T)
            x = _mm_ln([o], P['wo'], l, x, ln_g, ln_b, n_norms * l + 1, alpha)
        self.x = x

    def results(self):
        return (self.x.reshape(self.B, self.T, self.D), jnp.stack(self.convs), jnp.stack(self.hs),
                jnp.stack(self.ckvs), jnp.stack(self.kpes), jnp.stack(self.pools))


def kernel(x_prompt, x_sample, mem_prompt, cache_mem_k, cache_mem_v, cache_mla_ckv, cache_mla_kpe,
           state_rglru_h, state_rglru_conv, state_pool,
           w_in, q_norm_g, w_uq, kv_norm_g, w_uk, w_uv, conv_w, conv_b,
           gate_a_w, gate_a_b, gate_x_w, gate_x_b, lru_lambda, w_out,
           pool_w, pool_scale, xa_wq, xa_wk, xa_wv, xa_wo, mlp_up, mlp_down, ln_g, ln_b):
    P = _prep_weights(w_in, q_norm_g, w_uq, kv_norm_g, w_uk, w_uv, conv_w, conv_b,
                      gate_a_w, gate_a_b, gate_x_w, gate_x_b, lru_lambda, w_out,
                      pool_w, pool_scale, xa_wq, xa_wo, ln_g, ln_b)
    depth = ln_g.shape[0]
    n_even, n_odd = w_in.shape[0], pool_w.shape[0]
    bp, _, d_model = x_prompt.shape
    n_mem = mem_prompt.shape[1]
    dt = x_prompt.dtype
    lru_w = conv_w.shape[2]
    mem_rows = mem_prompt.reshape(bp * n_mem, d_model)
    p_mem_k = _rows_matmul(mem_rows, xa_wk, F32, tm_pref=1024, tn_pref=512).reshape(depth, bp, n_mem, d_model)
    p_mem_v = _rows_matmul(mem_rows, xa_wv, F32, tm_pref=1024, tn_pref=512).reshape(depth, bp, n_mem, d_model)
    prompt = _Group(
        x_prompt, 0, p_mem_k, p_mem_v,
        jnp.zeros((n_even, bp, conv_w.shape[1] - 1, lru_w), dt),
        jnp.zeros((n_even, bp, lru_w), dt),
        jnp.zeros((n_even, bp, 0, kv_norm_g.shape[1]), dt),
        jnp.zeros((n_even, bp, 0, cache_mla_kpe.shape[3]), dt),
        jnp.zeros((n_odd, bp, state_pool.shape[2], d_model), dt), P)
    sample = _Group(
        x_sample, cache_mla_ckv.shape[2], cache_mem_k, cache_mem_v,
        state_rglru_conv, state_rglru_h, cache_mla_ckv, cache_mla_kpe, state_pool, P)
    alpha = (2.0 * depth) ** 0.25
    ln_g3, ln_b3 = P['ln_g'], P['ln_b']
    for l in range(depth):
        sample.pre_mlp(l, P, alpha)
        prompt.pre_mlp(l, P, alpha)
        ln_i = N_NORMS * l + 2
        sample.x, up_b, down_b = _mlp_ln(sample.x, mlp_up, mlp_down, l, ln_g3, ln_b3, ln_i, alpha,
                                         emit=True, tf_pref=512)
        prompt.x = _mlp_ln(prompt.x, up_b, down_b, None, ln_g3, ln_b3, ln_i, alpha)
    y_prompt, p_conv, p_h, p_ckv, p_kpe, p_pool = prompt.results()
    y_sample, s_conv, s_h, s_ckv, s_kpe, s_pool = sample.results()
    return (y_prompt, y_sample, p_conv, p_h, p_ckv, p_kpe, p_pool, p_mem_k, p_mem_v,
            s_conv, s_h, s_ckv, s_kpe, s_pool)
```

```python
import functools

import jax
import jax.numpy as jnp
from jax import lax
from jax.experimental import pallas as pl
from jax.experimental.pallas import tpu as pltpu

F32 = jnp.float32
BF16 = jnp.bfloat16

CHUNK = 64
CHUNK_SHIFT = 6
assert 1 << CHUNK_SHIFT == CHUNK
MEM_HEADS = 4
POOL_WINDOWS = (2, 4, 8, 16)
LRU_C = 8.0
ROPE_THETA = 10000.0
LN_EPS = 1e-5
RMS_EPS = 1e-6
NEG_BIG = -1e30
LOG2_E = 1.4426950408889634

V7X_VMEM_BYTES = 64 * 1024 * 1024
VMEM_LIMIT = V7X_VMEM_BYTES - 8 * 1024 * 1024
LANES = 128
SUBLANES = 8
IN_PROJ_ROWS = 512
MLA_STACK_ROWS = 512
XATTN_FUSE_ROWS = 256
NT_DIMS = (((1,), (1,)), ((), ()))


def _params(*sem):
    return pltpu.CompilerParams(dimension_semantics=sem, vmem_limit_bytes=VMEM_LIMIT)


def _tile(n, pref):
    if n <= pref:
        return n
    t = pref
    while n % t:
        t -= SUBLANES
    assert t > 0
    return t


def _sel(arr, *lead, single=False):
    n_lead = len(lead)
    rest = arr.shape[n_lead:]
    idx = tuple(lead) + (0,) * len(rest)
    mode = dict(pipeline_mode=pl.Buffered(1)) if single else {}
    return pl.BlockSpec((None,) * n_lead + tuple(rest), lambda *_: idx, **mode)


def _layer_norm(y, g, b):
    mu = jnp.mean(y, axis=-1, keepdims=True)
    d = y - mu
    var = jnp.mean(d * d, axis=-1, keepdims=True)
    return d * lax.rsqrt(var + LN_EPS) * g + b


def _rms_norm(x, g):
    return x * lax.rsqrt(jnp.mean(x * x, axis=-1, keepdims=True) + RMS_EPS) * g


def _rope_lanes(x, cos_t, sin_t):
    n = x.shape[-1]
    half = 32
    lane = lax.broadcasted_iota(jnp.int32, x.shape, x.ndim - 1)
    first = (lane % (2 * half)) < half
    swapped = jnp.where(first, pltpu.roll(x, n - half, x.ndim - 1), pltpu.roll(x, half, x.ndim - 1))
    return x * cos_t + swapped * sin_t


def _rows_matmul_kernel(a_ref, w_ref, o_ref):
    a = a_ref[...].astype(BF16)
    w = w_ref[...].astype(BF16)
    o_ref[...] = jnp.dot(a, w, preferred_element_type=F32).astype(o_ref.dtype)


def _rows_matmul(a, w, out_dtype, g0=0, n_g=None, tm_pref=512, tn_pref=None):
    M, K = a.shape
    N = w.shape[2]
    n_g = w.shape[0] if n_g is None else n_g
    tm = _tile(M, tm_pref)
    tn = N if tn_pref is None else _tile(N, tn_pref)
    return pl.pallas_call(
        _rows_matmul_kernel,
        out_shape=jax.ShapeDtypeStruct((n_g, M, N), out_dtype),
        grid=(n_g, N // tn, M // tm),
        in_specs=[pl.BlockSpec((tm, K), lambda g, n, i: (i, 0)),
                  pl.BlockSpec((None, K, tn), lambda g, n, i: (g0 + g, 0, n))],
        out_specs=pl.BlockSpec((None, tm, tn), lambda g, n, i: (g, i, n)),
        compiler_params=_params("arbitrary", "arbitrary", "arbitrary"),
        name="rows_matmul",
    )(a, w)


def _mm_ln_kernel(*refs, n_a, alpha, sub, with_next):
    a_refs = refs[:n_a]
    if with_next:
        w_ref, res_ref, g_ref, b_ref, wn_ref, o_ref, on_ref = refs[n_a:]
    else:
        w_ref, res_ref, g_ref, b_ref, o_ref = refs[n_a:]
    normed = []
    for r0 in range(0, o_ref.shape[0], sub):
        rows = slice(r0, r0 + sub)
        y = None
        off = 0
        for a_ref in a_refs:
            k = a_ref.shape[1]
            part = jnp.dot(a_ref[rows, :].astype(BF16), w_ref[off:off + k, :], preferred_element_type=F32)
            y = part if y is None else y + part
            off += k
        xo = _layer_norm(alpha * res_ref[rows, :] + y, g_ref[...], b_ref[...])
        o_ref[rows, :] = xo
        normed.append((rows, xo))
    if with_next:
        for rows, xo in normed:
            on_ref[rows, :] = jnp.dot(xo.astype(BF16), wn_ref[...], preferred_element_type=F32).astype(on_ref.dtype)


def _mm_ln(a_list, w, wl, res, ln_g, ln_b, ln_i, alpha, w_next=None, next_l=None, tm_pref=512, sub_pref=256):
    M, N = res.shape
    tm = _tile(M, tm_pref)
    sub = _tile(tm, sub_pref)
    row = lambda i: (i, 0)
    in_specs = [pl.BlockSpec((tm, a.shape[1]), row) for a in a_list]
    in_specs += [_sel(w, wl, single=True), pl.BlockSpec((tm, N), row), _sel(ln_g, ln_i), _sel(ln_b, ln_i)]
    out_shape = [jax.ShapeDtypeStruct((M, N), F32)]
    out_specs = [pl.BlockSpec((tm, N), row)]
    operands = list(a_list) + [w, res, ln_g, ln_b]
    if w_next is not None:
        n_next = w_next.shape[2]
        in_specs.append(_sel(w_next, next_l, single=True))
        out_shape.append(jax.ShapeDtypeStruct((M, n_next), BF16))
        out_specs.append(pl.BlockSpec((tm, n_next), row))
        operands.append(w_next)
    out = pl.pallas_call(
        functools.partial(_mm_ln_kernel, n_a=len(a_list), alpha=alpha, sub=sub, with_next=w_next is not None),
        out_shape=out_shape,
        grid=(M // tm,),
        in_specs=in_specs,
        out_specs=out_specs,
        compiler_params=_params("arbitrary"),
        name="matmul_res_ln",
    )(*operands)
    return out[0] if w_next is None else out


def _mlp_ln_kernel(x_ref, wu_ref, wd_ref, g_ref, b_ref, o_ref, *rest, alpha, sub, emit, n_steps):
    if emit:
        wub_ref, wdb_ref, xb_ref, acc_ref = rest
    else:
        xb_ref, acc_ref = rest
    j = pl.program_id(1)
    last = n_steps - 1

    def step(first, final):
        wu = wu_ref[...].astype(BF16)
        wd = wd_ref[...].astype(BF16)
        if emit:
            wub_ref[...] = wu
            wdb_ref[...] = wd
        if first:
            xb = x_ref[...].astype(BF16)
            xb_ref[...] = xb
        else:
            xb = xb_ref[...]
        h = jnp.dot(xb, wu, preferred_element_type=F32)
        h = jnp.square(jnp.maximum(h, 0.0)).astype(BF16)
        if not final:
            part = jnp.dot(h, wd, preferred_element_type=F32)
            if first:
                acc_ref[...] = part
            else:
                acc_ref[...] += part
            return
        for r0 in range(0, o_ref.shape[0], sub):
            rows = slice(r0, r0 + sub)
            y = jnp.dot(h[rows, :], wd, preferred_element_type=F32)
            if not first:
                y = acc_ref[rows, :] + y
            o_ref[rows, :] = _layer_norm(alpha * x_ref[rows, :] + y, g_ref[...], b_ref[...])

    if n_steps == 1:
        step(True, True)
    else:
        pl.when(j == 0)(functools.partial(step, True, False))
        if n_steps > 2:
            pl.when((j > 0) & (j < last))(functools.partial(step, False, False))
        pl.when(j == last)(functools.partial(step, False, True))


def _mlp_ln(x, w_up, w_down, l, ln_g, ln_b, ln_i, alpha, emit=False, tm_pref=512, tf_pref=1024,
            sub_pref=256):
    M, D = x.shape
    FF = w_up.shape[-1]
    tm = _tile(M, tm_pref)
    tf = _tile(FF, tf_pref)
    sub = _tile(tm, sub_pref)
    out_shape = [jax.ShapeDtypeStruct((M, D), F32)]
    out_specs = [pl.BlockSpec((tm, D), lambda i, j: (i, 0))]
    if emit:
        assert M == tm
        w_specs = [pl.BlockSpec((None, D, tf), lambda i, j: (l, 0, j)),
                   pl.BlockSpec((None, tf, D), lambda i, j: (l, j, 0))]
        out_shape += [jax.ShapeDtypeStruct((D, FF), BF16), jax.ShapeDtypeStruct((FF, D), BF16)]
        out_specs += [pl.BlockSpec((D, tf), lambda i, j: (0, j)), pl.BlockSpec((tf, D), lambda i, j: (j, 0))]
    else:
        w_specs = [pl.BlockSpec((D, tf), lambda i, j: (0, j)), pl.BlockSpec((tf, D), lambda i, j: (j, 0))]
    out = pl.pallas_call(
        functools.partial(_mlp_ln_kernel, alpha=alpha, sub=sub, emit=emit, n_steps=FF // tf),
        out_shape=out_shape,
        grid=(M // tm, FF // tf),
        in_specs=[pl.BlockSpec((tm, D), lambda i, j: (i, 0))] + w_specs + [_sel(ln_g, ln_i), _sel(ln_b, ln_i)],
        out_specs=out_specs,
        scratch_shapes=[pltpu.VMEM((tm, D), BF16), pltpu.VMEM((tm, D), F32)],
        compiler_params=_params("arbitrary", "arbitrary"),
        name="mlp_res_ln_round" if emit else "mlp_res_ln",
    )(x, w_up, w_down, ln_g, ln_b)
    return out if emit else out[0]


def _in_proj_kernel(x_ref, w_ref, qg_ref, kg_ref, cos_ref, sin_ref,
                    ug_ref, cqn_ref, ckv_ref, kpe_ref, *, ug_cols, q_lora, kv_lora, qk_rope):
    xb = x_ref[...].astype(BF16)
    z = jnp.dot(xb, w_ref[:, ug_cols:], preferred_element_type=F32)
    cqn_ref[...] = _rms_norm(z[:, :q_lora], qg_ref[...]).astype(BF16)
    ckv_ref[...] = _rms_norm(z[:, q_lora:q_lora + kv_lora], kg_ref[...])
    pe = z[:, q_lora + kv_lora:]
    pad = cos_ref.shape[1] - qk_rope
    pe = jnp.concatenate([pe, jnp.zeros((pe.shape[0], pad), F32)], axis=1)
    kpe_ref[...] = _rope_lanes(pe, cos_ref[...], sin_ref[...])[:, :qk_rope]
    ug_ref[...] = jnp.dot(xb, w_ref[:, :ug_cols], preferred_element_type=F32)


def _in_proj(x, P, j, cos_t, sin_t, tm):
    M, D = x.shape
    w = P['w_in']
    q_lora, kv_lora, qk_rope = P['q_lora'], P['kv_lora'], P['rope_d']
    ug_cols = w.shape[2] - q_lora - kv_lora - qk_rope
    assert M % tm == 0 and cos_t.shape[0] % tm == 0 and ug_cols % LANES == 0
    n_pos_blocks = cos_t.shape[0] // tm
    pe_w = cos_t.shape[1]
    row = lambda i: (i, 0)
    pos = lambda i: (i % n_pos_blocks, 0)
    return pl.pallas_call(
        functools.partial(_in_proj_kernel, ug_cols=ug_cols, q_lora=q_lora, kv_lora=kv_lora, qk_rope=qk_rope),
        out_shape=(jax.ShapeDtypeStruct((M, ug_cols), F32),
                   jax.ShapeDtypeStruct((M, q_lora), BF16),
                   jax.ShapeDtypeStruct((M, kv_lora), F32),
                   jax.ShapeDtypeStruct((M, qk_rope), F32)),
        grid=(M // tm,),
        in_specs=[pl.BlockSpec((tm, D), row), _sel(w, j, single=True),
                  _sel(P['qg'], j), _sel(P['kg'], j),
                  pl.BlockSpec((tm, pe_w), pos), pl.BlockSpec((tm, pe_w), pos)],
        out_specs=(pl.BlockSpec((tm, ug_cols), row), pl.BlockSpec((tm, q_lora), row),
                   pl.BlockSpec((tm, kv_lora), row), pl.BlockSpec((tm, qk_rope), row)),
        compiler_params=_params("arbitrary"),
        name="in_proj",
    )(x, w, P['qg'], P['kg'], cos_t, sin_t)


def _lru_kernel(u_ref, gate_ref, cbuf_ref, h0_ref, cw_ref, cb_ref, gaw_ref, gab_ref,
                gxw_ref, gxb_ref, lam_ref, out_ref, hlast_ref,
                ubuf_ref, hcar_ref, a_ref, b_ref, *, tt, cw, heads):
    t = pl.program_id(1)
    pad = SUBLANES
    tail = cw - 1

    @pl.when(t == 0)
    def _():
        ubuf_ref[pad - tail:pad, :] = cbuf_ref[0]
        hcar_ref[...] = h0_ref[0]

    ubuf_ref[pad:pad + tt, :] = u_ref[...]
    rows_all = ubuf_ref[...]
    uc = cb_ref[...] + cw_ref[tail:cw, :] * rows_all[pad:, :]
    for k in range(tail):
        uc = uc + cw_ref[k:k + 1, :] * pltpu.roll(rows_all, tail - k, 0)[pad:, :]
    ubuf_ref[pad - tail:pad, :] = ubuf_ref[pad + tt - tail:pad + tt, :]

    width = uc.shape[1]
    blk = width // heads
    ucb = uc.astype(BF16)
    rs, igs = [], []
    for h in range(heads):
        uh = ucb[:, h * blk:(h + 1) * blk]
        rs.append(jnp.dot(uh, gaw_ref[h], preferred_element_type=F32))
        igs.append(jnp.dot(uh, gxw_ref[h], preferred_element_type=F32))
    r = jax.nn.sigmoid(jnp.concatenate(rs, axis=1) + gab_ref[...])
    ig = jax.nn.sigmoid(jnp.concatenate(igs, axis=1) + gxb_ref[...])
    nlam = -lam_ref[...]
    softplus = jnp.maximum(nlam, 0.0) + jnp.log1p(jnp.exp(-jnp.abs(nlam)))
    log_a = -LRU_C * r * softplus
    a = jnp.exp(log_a)
    a_ref[...] = a
    b_ref[...] = jnp.sqrt(-jnp.tanh(log_a) * (a * a + 1.0)) * (ig * uc)

    def step(i, h):
        h = a_ref[pl.ds(i, 1), :] * h + b_ref[pl.ds(i, 1), :]
        b_ref[pl.ds(i, 1), :] = h
        return h

    h_end = lax.fori_loop(0, tt, step, hcar_ref[...], unroll=8)
    hcar_ref[...] = h_end
    out_ref[...] = (b_ref[...] * jax.nn.gelu(gate_ref[...])).astype(out_ref.dtype)

    @pl.when(t == pl.num_programs(1) - 1)
    def _():
        hlast_ref[0] = h_end


def _lru(ug, conv_buf, h0, P, j, B, T, tt_pref=256):
    M = ug.shape[0]
    W = ug.shape[1] // 2
    heads = P['gaw'].shape[1]
    width = P['cw'].shape[1]
    tt = _tile(T, tt_pref)
    assert tt % SUBLANES == 0
    nt = T // tt
    row_u = lambda b, t: (b * nt + t, 0)
    row_g = lambda b, t: (b * nt + t, 1)
    names = ('cw', 'cb', 'gaw', 'gab', 'gxw', 'gxb', 'lam')
    return pl.pallas_call(
        functools.partial(_lru_kernel, tt=tt, cw=width, heads=heads),
        out_shape=(jax.ShapeDtypeStruct((M, W), BF16), jax.ShapeDtypeStruct((B, 1, W), F32)),
        grid=(B, nt),
        in_specs=[pl.BlockSpec((tt, W), row_u), pl.BlockSpec((tt, W), row_g),
                  pl.BlockSpec((1, width - 1, W), lambda b, t: (b, 0, 0)),
                  pl.BlockSpec((1, 1, W), lambda b, t: (b, 0, 0))] + [_sel(P[n], j) for n in names],
        out_specs=(pl.BlockSpec((tt, W), row_u), pl.BlockSpec((1, 1, W), lambda b, t: (b, 0, 0))),
        scratch_shapes=[pltpu.VMEM((SUBLANES + tt, W), F32), pltpu.VMEM((1, W), F32),
                        pltpu.VMEM((tt, W), F32), pltpu.VMEM((tt, W), F32)],
        compiler_params=_params("arbitrary", "arbitrary"),
        name="rg_lru",
    )(ug, ug, conv_buf, h0, *[P[n] for n in names])


def _mla_kernel(cqn_ref, wuq_ref, wukt_ref, cos_ref, sin_ref, ckv_ref, kpe_ref, wuv_ref, o_ref,
                qlat_ref, qpe_ref, m_ref, l_ref, acc_ref, sa_ref, sb_ref,
                *, tq, tk, hg, q_pos0, n_keys, scale):
    heads, nope, lat = wukt_ref.shape
    rope_d = qpe_ref.shape[2]
    vdim = wuv_ref.shape[2]
    rows = hg * tq
    n_groups = heads // hg
    assert n_groups == 1 or n_groups % 2 == 0

    q = jnp.dot(cqn_ref[...], wuq_ref[...], preferred_element_type=F32)
    qn = q[:, :heads * nope].astype(BF16)
    pe = _rope_lanes(q[:, heads * nope:], cos_ref[...], sin_ref[...])
    for h in range(heads):
        qlat_ref[h] = jnp.dot(qn[:, h * nope:(h + 1) * nope], wukt_ref[h],
                              preferred_element_type=F32).astype(BF16)
        qpe_ref[h] = pe[:, h * rope_d:(h + 1) * rope_d].astype(BF16)
    qi = pl.program_id(1)
    q_lo = q_pos0 + qi * tq
    vis_all = jnp.minimum(((q_lo >> CHUNK_SHIFT) + 1) * CHUNK, n_keys)
    vis_any = jnp.minimum((((q_lo + tq - 1) >> CHUNK_SHIFT) + 1) * CHUNK, n_keys)
    n_full = vis_all // tk
    n_tot = (vis_any + tk - 1) // tk

    m_ref[...] = jnp.full_like(m_ref, NEG_BIG)
    l_ref[...] = jnp.zeros_like(l_ref)
    acc_ref[...] = jnp.zeros_like(acc_ref)

    def block(kj, carry, masked):
        k0 = pl.multiple_of(kj * tk, tk)
        kc = ckv_ref[0, pl.ds(k0, tk), :]
        kp = kpe_ref[0, pl.ds(k0, tk), :]
        if masked:
            kpos = k0 + lax.broadcasted_iota(jnp.int32, (rows, tk), 1)
            qpos = q_lo + (lax.broadcasted_iota(jnp.int32, (rows, tk), 0) & (tq - 1))
            ok = ((kpos >> CHUNK_SHIFT) <= (qpos >> CHUNK_SHIFT)) & (kpos < n_keys)
        def scores(g):
            hs = pl.ds(g * hg, hg)
            ql = qlat_ref[hs].reshape(rows, lat)
            qp = qpe_ref[hs].reshape(rows, rope_d)
            s = lax.dot_general(ql, kc, NT_DIMS, preferred_element_type=F32)
            return s + lax.dot_general(qp, kp, NT_DIMS, preferred_element_type=F32)

        def update(g, s):
            if masked:
                s = jnp.where(ok, s, NEG_BIG)
            m_old = m_ref[g]
            m_new = jnp.maximum(m_old, jnp.max(s, axis=1, keepdims=True))
            corr = jnp.exp2((m_old - m_new) * (scale * LOG2_E))
            p = jnp.exp2((s - m_new) * (scale * LOG2_E))
            l_ref[g] = corr * l_ref[g] + jnp.sum(p, axis=1, keepdims=True)
            acc_ref[g] = corr * acc_ref[g] + jnp.dot(p.astype(BF16), kc, preferred_element_type=F32)
            m_ref[g] = m_new

        if n_groups == 1:
            update(0, scores(0))
            return carry

        sa_ref[...] = scores(0)

        def pair(i, c):
            g = 2 * i
            sb_ref[...] = scores(g + 1)
            update(g, sa_ref[...])
            sa_ref[...] = scores(jnp.minimum(g + 2, n_groups - 1))
            update(g + 1, sb_ref[...])
            return c

        lax.fori_loop(0, n_groups // 2, pair, 0)
        return carry

    lax.fori_loop(0, n_full, functools.partial(block, masked=False), 0)
    lax.fori_loop(n_full, n_tot, functools.partial(block, masked=True), 0)
    for h in range(heads):
        g, r0 = h // hg, (h % hg) * tq
        o = (acc_ref[g, r0:r0 + tq, :] / l_ref[g, r0:r0 + tq, :]).astype(BF16)
        o_ref[:, h * vdim:(h + 1) * vdim] = jnp.dot(
            o, wuv_ref[h], preferred_element_type=F32).astype(o_ref.dtype)


def _mla_attention(cqn, ckv_all, kpe_all, P, j, cos_t, sin_t, B, T, q_pos0, n_keys, tq_pref=512, tk_pref=512):
    wuq, wukt, wuv = P['wuq'], P['wukt'], P['wuv']
    _, heads, nope, lat = wukt.shape
    rope_d = P['rope_d']
    vdim = wuv.shape[3]
    q_lora = cqn.shape[1]
    Tk = ckv_all.shape[1]
    tq = _tile(T, tq_pref)
    tk = _tile(Tk, tk_pref)
    nt = T // tq
    hg = max(1, min(heads, MLA_STACK_ROWS // tq))
    assert heads % hg == 0 and tq & (tq - 1) == 0 and cos_t.shape[0] == T
    scale = (nope + rope_d) ** -0.5
    return pl.pallas_call(
        functools.partial(_mla_kernel, tq=tq, tk=tk, hg=hg, q_pos0=q_pos0, n_keys=n_keys, scale=scale),
        out_shape=jax.ShapeDtypeStruct((B * T, heads * vdim), BF16),
        grid=(B, nt),
        in_specs=[pl.BlockSpec((tq, q_lora), lambda b, t: (b * nt + t, 0)),
                  _sel(wuq, j), _sel(wukt, j),
                  pl.BlockSpec((tq, heads * rope_d), lambda b, t: (t, 0)),
                  pl.BlockSpec((tq, heads * rope_d), lambda b, t: (t, 0)),
                  pl.BlockSpec((1, Tk, lat), lambda b, t: (b, 0, 0)),
                  pl.BlockSpec((1, Tk, rope_d), lambda b, t: (b, 0, 0)),
                  _sel(wuv, j)],
        out_specs=pl.BlockSpec((tq, heads * vdim), lambda b, t: (b * nt + t, 0)),
        scratch_shapes=[pltpu.VMEM((heads, tq, lat), BF16), pltpu.VMEM((heads, tq, rope_d), BF16),
                        pltpu.VMEM((heads // hg, hg * tq, 1), F32), pltpu.VMEM((heads // hg, hg * tq, 1), F32),
                        pltpu.VMEM((heads // hg, hg * tq, lat), F32),
                        pltpu.VMEM((hg * tq, tk), F32), pltpu.VMEM((hg * tq, tk), F32)],
        compiler_params=_params("arbitrary", "arbitrary"),
        name="mla_attention",
    )(cqn, wuq, wukt, cos_t, sin_t, ckv_all, kpe_all, wuv)


def _xattn_kernel(q_ref, k_ref, v_ref, o_ref, *, heads, scale):
    hd = q_ref.shape[1] // heads
    for h in range(heads):
        sl = slice(h * hd, (h + 1) * hd)
        q = q_ref[:, sl]
        k = k_ref[:, sl].astype(BF16)
        v = v_ref[:, sl].astype(BF16)
        s = lax.dot_general(q, k, NT_DIMS, preferred_element_type=F32) * scale
        p = jnp.exp(s - jnp.max(s, axis=1, keepdims=True))
        l = jnp.sum(p, axis=1, keepdims=True)
        o = jnp.dot(p.astype(BF16), v, preferred_element_type=F32) / l
        o_ref[:, sl] = o.astype(o_ref.dtype)


def _xattn(q, mem_k, mem_v, l, B, T, tq_pref=512):
    M, D = q.shape
    n_mem = mem_k.shape[2]
    tq = _tile(T, tq_pref)
    nt = T // tq
    scale = (D // MEM_HEADS) ** -0.5
    mem_spec = pl.BlockSpec((None, None, n_mem, D), lambda b, t: (l, b, 0, 0))
    return pl.pallas_call(
        functools.partial(_xattn_kernel, heads=MEM_HEADS, scale=scale),
        out_shape=jax.ShapeDtypeStruct((M, D), BF16),
        grid=(B, nt),
        in_specs=[pl.BlockSpec((tq, D), lambda b, t: (b * nt + t, 0)), mem_spec, mem_spec],
        out_specs=pl.BlockSpec((tq, D), lambda b, t: (b * nt + t, 0)),
        compiler_params=_params("arbitrary", "arbitrary"),
        name="mem_xattn",
    )(q, mem_k, mem_v)


def _xattn_ln_kernel(q_ref, k_ref, v_ref, wo_ref, res_ref, g_ref, b_ref, o_ref, *, heads, scale, alpha, sub):
    hd = q_ref.shape[1] // heads
    kb = k_ref[...].astype(BF16)
    vb = v_ref[...].astype(BF16)
    for r0 in range(0, o_ref.shape[0], sub):
        rows = slice(r0, r0 + sub)
        outs = []
        for h in range(heads):
            sl = slice(h * hd, (h + 1) * hd)
            s = lax.dot_general(q_ref[rows, sl], kb[:, sl], NT_DIMS, preferred_element_type=F32) * scale
            p = jnp.exp(s - jnp.max(s, axis=1, keepdims=True))
            l = jnp.sum(p, axis=1, keepdims=True)
            outs.append((jnp.dot(p.astype(BF16), vb[:, sl], preferred_element_type=F32) / l).astype(BF16))
        y = jnp.dot(jnp.concatenate(outs, axis=1), wo_ref[...], preferred_element_type=F32)
        o_ref[rows, :] = _layer_norm(alpha * res_ref[rows, :] + y, g_ref[...], b_ref[...])


def _xattn_ln(q, mem_k, mem_v, l, wo, res, ln_g, ln_b, ln_i, alpha, B, T, tq_pref=512, sub_pref=256):
    M, D = q.shape
    n_mem = mem_k.shape[2]
    tq = _tile(T, tq_pref)
    sub = _tile(tq, sub_pref)
    nt = T // tq
    scale = (D // MEM_HEADS) ** -0.5
    row = lambda b, t: (b * nt + t, 0)
    mem_spec = pl.BlockSpec((None, None, n_mem, D), lambda b, t: (l, b, 0, 0))
    return pl.pallas_call(
        functools.partial(_xattn_ln_kernel, heads=MEM_HEADS, scale=scale, alpha=alpha, sub=sub),
        out_shape=jax.ShapeDtypeStruct((M, D), F32),
        grid=(B, nt),
        in_specs=[pl.BlockSpec((tq, D), row), mem_spec, mem_spec, _sel(wo, l, single=True),
                  pl.BlockSpec((tq, D), row), _sel(ln_g, ln_i), _sel(ln_b, ln_i)],
        out_specs=pl.BlockSpec((tq, D), row),
        compiler_params=_params("arbitrary", "arbitrary"),
        name="mem_xattn_out_ln",
    )(q, mem_k, mem_v, wo, res, ln_g, ln_b)


def _pool_ln_kernel(x_ref, pbuf_ref, pw_ref, ps_ref, g_ref, b_ref, *rest, tt, sub, pos0, alpha, with_next):
    if with_next:
        wn_ref, o_ref, on_ref, full_ref, y_ref = rest
    else:
        o_ref, full_ref, y_ref = rest
    t = pl.program_id(1)
    halo = 2 * SUBLANES
    groups = pw_ref.shape[0]
    gw = pw_ref.shape[1]

    @pl.when(t == 0)
    def _():
        full_ref[0:halo, :] = pbuf_ref[0]

    full_ref[halo:halo + tt, :] = x_ref[...]
    normed = []
    for r0 in range(0, tt, sub):
        rows = slice(r0, r0 + sub)
        pos = pos0 + t * tt + r0 + lax.broadcasted_iota(jnp.int32, (sub, 1), 0)
        for g in range(groups):
            w = POOL_WINDOWS[g]
            sl = slice(g * gw, (g + 1) * gw)
            f = full_ref[r0:r0 + halo + sub, sl]
            s = f
            d = 1
            while d < w:
                s = s + pltpu.roll(s, d, 0)
                d *= 2
            cnt = jnp.minimum(pos + 1, w).astype(F32)
            dlt = (s[halo:] / cnt - f[halo:]).astype(BF16)
            y_ref[rows, sl] = jnp.dot(dlt, pw_ref[g], preferred_element_type=F32)
        xo = _layer_norm(alpha * x_ref[rows, :] + y_ref[rows, :] * ps_ref[...], g_ref[...], b_ref[...])
        o_ref[rows, :] = xo
        normed.append((rows, xo))
    full_ref[0:halo, :] = full_ref[tt:tt + halo, :]
    if with_next:
        for rows, xo in normed:
            on_ref[rows, :] = jnp.dot(xo.astype(BF16), wn_ref[...], preferred_element_type=F32).astype(on_ref.dtype)


def _pool_ln(x, pbuf16, P, j, ln_i, alpha, B, T, pos0, w_next=None, next_l=None, tt_pref=512, sub_pref=256):
    M, D = x.shape
    tt = _tile(T, tt_pref)
    sub = _tile(tt, sub_pref)
    nt = T // tt
    halo = 2 * SUBLANES
    assert max(POOL_WINDOWS) <= halo and tt >= halo
    row = lambda bb, t: (bb * nt + t, 0)
    in_specs = [pl.BlockSpec((tt, D), row),
                pl.BlockSpec((1, halo, D), lambda bb, t: (bb, 0, 0)),
                _sel(P['pw'], j), _sel(P['ps'], j), _sel(P['ln_g'], ln_i), _sel(P['ln_b'], ln_i)]
    out_shape = [jax.ShapeDtypeStruct((M, D), F32)]
    out_specs = [pl.BlockSpec((tt, D), row)]
    operands = [x, pbuf16, P['pw'], P['ps'], P['ln_g'], P['ln_b']]
    if w_next is not None:
        n_next = w_next.shape[2]
        in_specs.append(_sel(w_next, next_l, single=True))
        out_shape.append(jax.ShapeDtypeStruct((M, n_next), BF16))
        out_specs.append(pl.BlockSpec((tt, n_next), row))
        operands.append(w_next)
    out = pl.pallas_call(
        functools.partial(_pool_ln_kernel, tt=tt, sub=sub, pos0=pos0, alpha=alpha, with_next=w_next is not None),
        out_shape=out_shape,
        grid=(B, nt),
        in_specs=in_specs,
        out_specs=out_specs,
        scratch_shapes=[pltpu.VMEM((halo + tt, D), F32), pltpu.VMEM((tt, D), F32)],
        compiler_params=_params("arbitrary", "arbitrary"),
        name="pool_res_ln",
    )(*operands)
    return out[0] if w_next is None else out


def _rope_tables(pos, rope_d, heads):
    half = rope_d // 2
    inv = ROPE_THETA ** (-jnp.arange(half, dtype=F32) / half)
    ang = pos.astype(F32)[:, None] * inv[None, :]
    cos, sin = jnp.cos(ang), jnp.sin(ang)
    cos_t = jnp.tile(jnp.concatenate([cos, cos], axis=1), (1, heads))
    sin_t = jnp.tile(jnp.concatenate([-sin, sin], axis=1), (1, heads))
    return cos_t, sin_t


def _prep_weights(w_in, q_norm_g, w_uq, kv_norm_g, w_uk, w_uv, conv_w, conv_b,
                  gate_a_w, gate_a_b, gate_x_w, gate_x_b, lru_lambda, w_out,
                  pool_w, pool_scale, xa_wq, xa_wo, ln_g, ln_b):
    n_even = w_in.shape[0]
    d_model = ln_g.shape[2]
    q_lora = q_norm_g.shape[1]
    kv_lora = kv_norm_g.shape[1]
    heads, nope = w_uk.shape[2], w_uk.shape[3]
    rope_d = w_uq.shape[2] // heads - nope
    wuq = w_uq.reshape(n_even, q_lora, heads, nope + rope_d)
    wuq = jnp.concatenate([wuq[..., :nope].reshape(n_even, q_lora, heads * nope),
                           wuq[..., nope:].reshape(n_even, q_lora, heads * rope_d)], axis=2).astype(BF16)
    wukt = jnp.transpose(w_uk, (0, 2, 3, 1)).astype(BF16)
    wuv = jnp.transpose(w_uv, (0, 2, 1, 3)).astype(BF16)
    row = lambda a: a.reshape(a.shape[0], 1, -1)
    return dict(
        w_in=w_in.astype(BF16), qg=row(q_norm_g), kg=row(kv_norm_g), wuq=wuq, wukt=wukt, wuv=wuv,
        cw=conv_w, cb=row(conv_b), gaw=gate_a_w.astype(BF16), gab=row(gate_a_b),
        gxw=gate_x_w.astype(BF16), gxb=row(gate_x_b), lam=row(lru_lambda),
        w_out=w_out.astype(BF16), pw=pool_w.astype(BF16), ps=row(pool_scale),
        wq=xa_wq.astype(BF16), wo=xa_wo.astype(BF16),
        ln_g=ln_g.reshape(-1, 1, d_model), ln_b=ln_b.reshape(-1, 1, d_model),
        q_lora=q_lora, kv_lora=kv_lora, heads=heads, nope=nope, rope_d=rope_d)


N_NORMS = 3


class _Group:
    def __init__(self, x3, t_past, mem_k, mem_v, conv_buf, h0, ckv_past, kpe_past, pool_buf, P):
        self.B, self.T, self.D = x3.shape
        B, T = self.B, self.T
        self.x = x3.reshape(B * T, self.D)
        self.t_past, self.mem_k, self.mem_v = t_past, mem_k, mem_v
        self.conv_buf, self.h0, self.ckv_past, self.kpe_past, self.pool_buf = (
            conv_buf, h0, ckv_past, kpe_past, pool_buf)
        self.cos_t, self.sin_t = _rope_tables(t_past + jnp.arange(T), P['rope_d'], P['heads'])
        self.tm_in = _tile(T, IN_PROJ_ROWS) if T >= IN_PROJ_ROWS else B * T
        reps = max(1, self.tm_in // T)
        self.cos_rows = jnp.tile(self.cos_t, (reps, 1))[:, :LANES]
        self.sin_rows = jnp.tile(self.sin_t, (reps, 1))[:, :LANES]
        self.convs, self.hs, self.ckvs, self.kpes, self.pools = [], [], [], [], []

    def pre_mlp(self, l, P, alpha):
        B, T, D = self.B, self.T, self.D
        x, t_past, mem_k, mem_v = self.x, self.t_past, self.mem_k, self.mem_v
        conv_buf, h0, ckv_past, kpe_past, pool_buf = (
            self.conv_buf, self.h0, self.ckv_past, self.kpe_past, self.pool_buf)
        cos_t, sin_t = self.cos_t, self.sin_t
        convs, hs, ckvs, kpes, pools = self.convs, self.hs, self.ckvs, self.kpes, self.pools
        n_norms = N_NORMS
        ln_g, ln_b = P['ln_g'], P['ln_b']
        j = l // 2
        if l % 2 == 0:
            ug, cqn, ckv, kpe = _in_proj(x, P, j, self.cos_rows, self.sin_rows, self.tm_in)
            lru_w = ug.shape[1] // 2
            rec, h_last = _lru(ug, conv_buf[j], h0[j][:, None, :], P, j, B, T)
            ckv3 = ckv.reshape(B, T, -1)
            kpe3 = kpe.reshape(B, T, -1)
            n_keys = ckv_past[j].shape[1] + T
            tk = 512 if T >= 512 and n_keys % 512 == 0 else n_keys + (-n_keys) % 256
            padk = (-n_keys) % tk
            ckv_all = jnp.concatenate([ckv_past[j].astype(BF16), ckv3.astype(BF16),
                                       jnp.zeros((B, padk, ckv3.shape[2]), BF16)], axis=1)
            kpe_all = jnp.concatenate([kpe_past[j].astype(BF16), kpe3.astype(BF16),
                                       jnp.zeros((B, padk, kpe3.shape[2]), BF16)], axis=1)
            attn = _mla_attention(cqn, ckv_all, kpe_all, P, j, cos_t, sin_t, B, T, t_past, n_keys,
                                  tk_pref=tk)
            x, q = _mm_ln([rec, attn], P['w_out'], j, x, ln_g, ln_b, n_norms * l, alpha,
                          w_next=P['wq'], next_l=l)
            tail = P['cw'].shape[1] - 1
            convs.append(ug.reshape(B, T, -1)[:, T - tail:, :lru_w])
            hs.append(h_last[:, 0, :])
            ckvs.append(ckv3)
            kpes.append(kpe3)
        else:
            nbuf = pool_buf[j].shape[1]
            pools.append(x.reshape(B, T, D)[:, T - nbuf:])
            pbuf16 = jnp.pad(pool_buf[j], ((0, 0), (2 * SUBLANES - nbuf, 0), (0, 0)))
            if T >= XATTN_FUSE_ROWS:
                x, q = _pool_ln(x, pbuf16, P, j, n_norms * l, alpha, B, T, t_past, P['wq'], l)
            else:
                x = _pool_ln(x, pbuf16, P, j, n_norms * l, alpha, B, T, t_past)
                q = _rows_matmul(x, P['wq'], BF16, g0=l, n_g=1, tn_pref=512)[0]
        if T >= XATTN_FUSE_ROWS:
            x = _xattn_ln(q, mem_k, mem_v, l, P['wo'], x, ln_g, ln_b, n_norms * l + 1, alpha, B, T)
        else:
            o = _xattn(q, mem_k, mem_v, l, B, T)
            x = _mm_ln([o], P['wo'], l, x, ln_g, ln_b, n_norms * l + 1, alpha)
        self.x = x

    def results(self):
        return (self.x.reshape(self.B, self.T, self.D), jnp.stack(self.convs), jnp.stack(self.hs),
                jnp.stack(self.ckvs), jnp.stack(self.kpes), jnp.stack(self.pools))


def kernel(x_prompt, x_sample, mem_prompt, cache_mem_k, cache_mem_v, cache_mla_ckv, cache_mla_kpe,
           state_rglru_h, state_rglru_conv, state_pool,
           w_in, q_norm_g, w_uq, kv_norm_g, w_uk, w_uv, conv_w, conv_b,
           gate_a_w, gate_a_b, gate_x_w, gate_x_b, lru_lambda, w_out,
           pool_w, pool_scale, xa_wq, xa_wk, xa_wv, xa_wo, mlp_up, mlp_down, ln_g, ln_b):
    P = _prep_weights(w_in, q_norm_g, w_uq, kv_norm_g, w_uk, w_uv, conv_w, conv_b,
                      gate_a_w, gate_a_b, gate_x_w, gate_x_b, lru_lambda, w_out,
                      pool_w, pool_scale, xa_wq, xa_wo, ln_g, ln_b)
    depth = ln_g.shape[0]
    n_even, n_odd = w_in.shape[0], pool_w.shape[0]
    bp, _, d_model = x_prompt.shape
    n_mem = mem_prompt.shape[1]
    dt = x_prompt.dtype
    lru_w = conv_w.shape[2]
    mem_rows = mem_prompt.reshape(bp * n_mem, d_model)
    p_mem_k = _rows_matmul(mem_rows, xa_wk, F32, tm_pref=1024, tn_pref=512).reshape(depth, bp, n_mem, d_model)
    p_mem_v = _rows_matmul(mem_rows, xa_wv, F32, tm_pref=1024, tn_pref=512).reshape(depth, bp, n_mem, d_model)
    prompt = _Group(
        x_prompt, 0, p_mem_k, p_mem_v,
        jnp.zeros((n_even, bp, conv_w.shape[1] - 1, lru_w), dt),
        jnp.zeros((n_even, bp, lru_w), dt),
        jnp.zeros((n_even, bp, 0, kv_norm_g.shape[1]), dt),
        jnp.zeros((n_even, bp, 0, cache_mla_kpe.shape[3]), dt),
        jnp.zeros((n_odd, bp, state_pool.shape[2], d_model), dt), P)
    sample = _Group(
        x_sample, cache_mla_ckv.shape[2], cache_mem_k, cache_mem_v,
        state_rglru_conv, state_rglru_h, cache_mla_ckv, cache_mla_kpe, state_pool, P)
    alpha = (2.0 * depth) ** 0.25
    ln_g3, ln_b3 = P['ln_g'], P['ln_b']
    for l in range(depth):
        sample.pre_mlp(l, P, alpha)
        prompt.pre_mlp(l, P, alpha)
        ln_i = N_NORMS * l + 2
        sample.x, up_b, down_b = _mlp_ln(sample.x, mlp_up, mlp_down, l, ln_g3, ln_b3, ln_i, alpha,
                                         emit=True, tf_pref=512)
        prompt.x = _mlp_ln(prompt.x, up_b, down_b, None, ln_g3, ln_b3, ln_i, alpha)
    y_prompt, p_conv, p_h, p_ckv, p_kpe, p_pool = prompt.results()
    y_sample, s_conv, s_h, s_ckv, s_kpe, s_pool = sample.results()
    return (y_prompt, y_sample, p_conv, p_h, p_ckv, p_kpe, p_pool, p_mem_k, p_mem_v,
            s_conv, s_h, s_ckv, s_kpe, s_pool)
```

```python
import functools

import jax
import jax.numpy as jnp
from jax import lax
from jax.experimental import pallas as pl
from jax.experimental.pallas import tpu as pltpu

F32 = jnp.float32
BF16 = jnp.bfloat16

CHUNK = 64
CHUNK_SHIFT = 6
assert 1 << CHUNK_SHIFT == CHUNK
MEM_HEADS = 4
POOL_WINDOWS = (2, 4, 8, 16)
LRU_C = 8.0
ROPE_THETA = 10000.0
LN_EPS = 1e-5
RMS_EPS = 1e-6
NEG_BIG = -1e30
LOG2_E = 1.4426950408889634

V7X_VMEM_BYTES = 64 * 1024 * 1024
VMEM_LIMIT = V7X_VMEM_BYTES - 8 * 1024 * 1024
LANES = 128
SUBLANES = 8
IN_PROJ_ROWS = 512
MLA_STACK_ROWS = 512
XATTN_FUSE_ROWS = 256
NT_DIMS = (((1,), (1,)), ((), ()))


def _params(*sem):
    return pltpu.CompilerParams(dimension_semantics=sem, vmem_limit_bytes=VMEM_LIMIT)


def _tile(n, pref):
    if n <= pref:
        return n
    t = pref
    while n % t:
        t -= SUBLANES
    assert t > 0
    return t


def _sel(arr, *lead, single=False):
    n_lead = len(lead)
    rest = arr.shape[n_lead:]
    idx = tuple(lead) + (0,) * len(rest)
    mode = dict(pipeline_mode=pl.Buffered(1)) if single else {}
    return pl.BlockSpec((None,) * n_lead + tuple(rest), lambda *_: idx, **mode)


def _layer_norm(y, g, b):
    mu = jnp.mean(y, axis=-1, keepdims=True)
    d = y - mu
    var = jnp.mean(d * d, axis=-1, keepdims=True)
    return d * lax.rsqrt(var + LN_EPS) * g + b


def _rms_norm(x, g):
    return x * lax.rsqrt(jnp.mean(x * x, axis=-1, keepdims=True) + RMS_EPS) * g


def _rope_lanes(x, cos_t, sin_t):
    n = x.shape[-1]
    half = 32
    lane = lax.broadcasted_iota(jnp.int32, x.shape, x.ndim - 1)
    first = (lane % (2 * half)) < half
    swapped = jnp.where(first, pltpu.roll(x, n - half, x.ndim - 1), pltpu.roll(x, half, x.ndim - 1))
    return x * cos_t + swapped * sin_t


def _rows_matmul_kernel(a_ref, w_ref, o_ref):
    a = a_ref[...].astype(BF16)
    w = w_ref[...].astype(BF16)
    o_ref[...] = jnp.dot(a, w, preferred_element_type=F32).astype(o_ref.dtype)


def _rows_matmul(a, w, out_dtype, g0=0, n_g=None, tm_pref=512, tn_pref=None):
    M, K = a.shape
    N = w.shape[2]
    n_g = w.shape[0] if n_g is None else n_g
    tm = _tile(M, tm_pref)
    tn = N if tn_pref is None else _tile(N, tn_pref)
    return pl.pallas_call(
        _rows_matmul_kernel,
        out_shape=jax.ShapeDtypeStruct((n_g, M, N), out_dtype),
        grid=(n_g, N // tn, M // tm),
        in_specs=[pl.BlockSpec((tm, K), lambda g, n, i: (i, 0)),
                  pl.BlockSpec((None, K, tn), lambda g, n, i: (g0 + g, 0, n))],
        out_specs=pl.BlockSpec((None, tm, tn), lambda g, n, i: (g, i, n)),
        compiler_params=_params("arbitrary", "arbitrary", "arbitrary"),
        name="rows_matmul",
    )(a, w)


def _mm_ln_kernel(*refs, n_a, alpha, sub, with_next):
    a_refs = refs[:n_a]
    if with_next:
        w_ref, res_ref, g_ref, b_ref, wn_ref, o_ref, on_ref = refs[n_a:]
    else:
        w_ref, res_ref, g_ref, b_ref, o_ref = refs[n_a:]
    normed = []
    for r0 in range(0, o_ref.shape[0], sub):
        rows = slice(r0, r0 + sub)
        y = None
        off = 0
        for a_ref in a_refs:
            k = a_ref.shape[1]
            part = jnp.dot(a_ref[rows, :].astype(BF16), w_ref[off:off + k, :], preferred_element_type=F32)
            y = part if y is None else y + part
            off += k
        xo = _layer_norm(alpha * res_ref[rows, :] + y, g_ref[...], b_ref[...])
        o_ref[rows, :] = xo
        normed.append((rows, xo))
    if with_next:
        for rows, xo in normed:
            on_ref[rows, :] = jnp.dot(xo.astype(BF16), wn_ref[...], preferred_element_type=F32).astype(on_ref.dtype)


def _mm_ln(a_list, w, wl, res, ln_g, ln_b, ln_i, alpha, w_next=None, next_l=None, tm_pref=512, sub_pref=256):
    M, N = res.shape
    tm = _tile(M, tm_pref)
    sub = _tile(tm, sub_pref)
    row = lambda i: (i, 0)
    in_specs = [pl.BlockSpec((tm, a.shape[1]), row) for a in a_list]
    in_specs += [_sel(w, wl, single=True), pl.BlockSpec((tm, N), row), _sel(ln_g, ln_i), _sel(ln_b, ln_i)]
    out_shape = [jax.ShapeDtypeStruct((M, N), F32)]
    out_specs = [pl.BlockSpec((tm, N), row)]
    operands = list(a_list) + [w, res, ln_g, ln_b]
    if w_next is not None:
        n_next = w_next.shape[2]
        in_specs.append(_sel(w_next, next_l, single=True))
        out_shape.append(jax.ShapeDtypeStruct((M, n_next), BF16))
        out_specs.append(pl.BlockSpec((tm, n_next), row))
        operands.append(w_next)
    out = pl.pallas_call(
        functools.partial(_mm_ln_kernel, n_a=len(a_list), alpha=alpha, sub=sub, with_next=w_next is not None),
        out_shape=out_shape,
        grid=(M // tm,),
        in_specs=in_specs,
        out_specs=out_specs,
        compiler_params=_params("arbitrary"),
        name="matmul_res_ln",
    )(*operands)
    return out[0] if w_next is None else out


def _mlp_ln_kernel(x_ref, wu_ref, wd_ref, g_ref, b_ref, o_ref, *rest, alpha, sub, emit, n_steps):
    if emit:
        wub_ref, wdb_ref, xb_ref, acc_ref = rest
    else:
        xb_ref, acc_ref = rest
    j = pl.program_id(1)
    last = n_steps - 1

    def step(first, final):
        wu = wu_ref[...].astype(BF16)
        wd = wd_ref[...].astype(BF16)
        if emit:
            wub_ref[...] = wu
            wdb_ref[...] = wd
        if first:
            xb = x_ref[...].astype(BF16)
            xb_ref[...] = xb
        else:
            xb = xb_ref[...]
        h = jnp.dot(xb, wu, preferred_element_type=F32)
        h = jnp.square(jnp.maximum(h, 0.0)).astype(BF16)
        if not final:
            part = jnp.dot(h, wd, preferred_element_type=F32)
            if first:
                acc_ref[...] = part
            else:
                acc_ref[...] += part
            return
        for r0 in range(0, o_ref.shape[0], sub):
            rows = slice(r0, r0 + sub)
            y = jnp.dot(h[rows, :], wd, preferred_element_type=F32)
            if not first:
                y = acc_ref[rows, :] + y
            o_ref[rows, :] = _layer_norm(alpha * x_ref[rows, :] + y, g_ref[...], b_ref[...])

    if n_steps == 1:
        step(True, True)
    else:
        pl.when(j == 0)(functools.partial(step, True, False))
        if n_steps > 2:
            pl.when((j > 0) & (j < last))(functools.partial(step, False, False))
        pl.when(j == last)(functools.partial(step, False, True))


def _mlp_ln(x, w_up, w_down, l, ln_g, ln_b, ln_i, alpha, emit=False, tm_pref=512, tf_pref=1024,
            sub_pref=256):
    M, D = x.shape
    FF = w_up.shape[-1]
    tm = _tile(M, tm_pref)
    tf = _tile(FF, tf_pref)
    sub = _tile(tm, sub_pref)
    out_shape = [jax.ShapeDtypeStruct((M, D), F32)]
    out_specs = [pl.BlockSpec((tm, D), lambda i, j: (i, 0))]
    if emit:
        assert M == tm
        w_specs = [pl.BlockSpec((None, D, tf), lambda i, j: (l, 0, j)),
                   pl.BlockSpec((None, tf, D), lambda i, j: (l, j, 0))]
        out_shape += [jax.ShapeDtypeStruct((D, FF), BF16), jax.ShapeDtypeStruct((FF, D), BF16)]
        out_specs += [pl.BlockSpec((D, tf), lambda i, j: (0, j)), pl.BlockSpec((tf, D), lambda i, j: (j, 0))]
    else:
        w_specs = [pl.BlockSpec((D, tf), lambda i, j: (0, j)), pl.BlockSpec((tf, D), lambda i, j: (j, 0))]
    out = pl.pallas_call(
        functools.partial(_mlp_ln_kernel, alpha=alpha, sub=sub, emit=emit, n_steps=FF // tf),
        out_shape=out_shape,
        grid=(M // tm, FF // tf),
        in_specs=[pl.BlockSpec((tm, D), lambda i, j: (i, 0))] + w_specs + [_sel(ln_g, ln_i), _sel(ln_b, ln_i)],
        out_specs=out_specs,
        scratch_shapes=[pltpu.VMEM((tm, D), BF16), pltpu.VMEM((tm, D), F32)],
        compiler_params=_params("arbitrary", "arbitrary"),
        name="mlp_res_ln_round" if emit else "mlp_res_ln",
    )(x, w_up, w_down, ln_g, ln_b)
    return out if emit else out[0]


def _in_proj_kernel(x_ref, w_ref, qg_ref, kg_ref, cos_ref, sin_ref,
                    ug_ref, cqn_ref, ckv_ref, kpe_ref, *, ug_cols, q_lora, kv_lora, qk_rope):
    xb = x_ref[...].astype(BF16)
    z = jnp.dot(xb, w_ref[:, ug_cols:], preferred_element_type=F32)
    cqn_ref[...] = _rms_norm(z[:, :q_lora], qg_ref[...]).astype(BF16)
    ckv_ref[...] = _rms_norm(z[:, q_lora:q_lora + kv_lora], kg_ref[...])
    pe = z[:, q_lora + kv_lora:]
    pad = cos_ref.shape[1] - qk_rope
    pe = jnp.concatenate([pe, jnp.zeros((pe.shape[0], pad), F32)], axis=1)
    kpe_ref[...] = _rope_lanes(pe, cos_ref[...], sin_ref[...])[:, :qk_rope]
    ug_ref[...] = jnp.dot(xb, w_ref[:, :ug_cols], preferred_element_type=F32)


def _in_proj(x, P, j, cos_t, sin_t, tm):
    M, D = x.shape
    w = P['w_in']
    q_lora, kv_lora, qk_rope = P['q_lora'], P['kv_lora'], P['rope_d']
    ug_cols = w.shape[2] - q_lora - kv_lora - qk_rope
    assert M % tm == 0 and cos_t.shape[0] % tm == 0 and ug_cols % LANES == 0
    n_pos_blocks = cos_t.shape[0] // tm
    pe_w = cos_t.shape[1]
    row = lambda i: (i, 0)
    pos = lambda i: (i % n_pos_blocks, 0)
    return pl.pallas_call(
        functools.partial(_in_proj_kernel, ug_cols=ug_cols, q_lora=q_lora, kv_lora=kv_lora, qk_rope=qk_rope),
        out_shape=(jax.ShapeDtypeStruct((M, ug_cols), F32),
                   jax.ShapeDtypeStruct((M, q_lora), BF16),
                   jax.ShapeDtypeStruct((M, kv_lora), F32),
                   jax.ShapeDtypeStruct((M, qk_rope), F32)),
        grid=(M // tm,),
        in_specs=[pl.BlockSpec((tm, D), row), _sel(w, j, single=True),
                  _sel(P['qg'], j), _sel(P['kg'], j),
                  pl.BlockSpec((tm, pe_w), pos), pl.BlockSpec((tm, pe_w), pos)],
        out_specs=(pl.BlockSpec((tm, ug_cols), row), pl.BlockSpec((tm, q_lora), row),
                   pl.BlockSpec((tm, kv_lora), row), pl.BlockSpec((tm, qk_rope), row)),
        compiler_params=_params("arbitrary"),
        name="in_proj",
    )(x, w, P['qg'], P['kg'], cos_t, sin_t)


def _lru_kernel(u_ref, gate_ref, cbuf_ref, h0_ref, cw_ref, cb_ref, gaw_ref, gab_ref,
                gxw_ref, gxb_ref, lam_ref, out_ref, hlast_ref,
                ubuf_ref, hcar_ref, a_ref, b_ref, *, tt, cw, heads):
    t = pl.program_id(1)
    pad = SUBLANES
    tail = cw - 1

    @pl.when(t == 0)
    def _():
        ubuf_ref[pad - tail:pad, :] = cbuf_ref[0]
        hcar_ref[...] = h0_ref[0]

    ubuf_ref[pad:pad + tt, :] = u_ref[...]
    rows_all = ubuf_ref[...]
    uc = cb_ref[...] + cw_ref[tail:cw, :] * rows_all[pad:, :]
    for k in range(tail):
        uc = uc + cw_ref[k:k + 1, :] * pltpu.roll(rows_all, tail - k, 0)[pad:, :]
    ubuf_ref[pad - tail:pad, :] = ubuf_ref[pad + tt - tail:pad + tt, :]

    width = uc.shape[1]
    blk = width // heads
    ucb = uc.astype(BF16)
    rs, igs = [], []
    for h in range(heads):
        uh = ucb[:, h * blk:(h + 1) * blk]
        rs.append(jnp.dot(uh, gaw_ref[h], preferred_element_type=F32))
        igs.append(jnp.dot(uh, gxw_ref[h], preferred_element_type=F32))
    r = jax.nn.sigmoid(jnp.concatenate(rs, axis=1) + gab_ref[...])
    ig = jax.nn.sigmoid(jnp.concatenate(igs, axis=1) + gxb_ref[...])
    nlam = -lam_ref[...]
    softplus = jnp.maximum(nlam, 0.0) + jnp.log1p(jnp.exp(-jnp.abs(nlam)))
    log_a = -LRU_C * r * softplus
    a = jnp.exp(log_a)
    a_ref[...] = a
    b_ref[...] = jnp.sqrt(-jnp.tanh(log_a) * (a * a + 1.0)) * (ig * uc)

    def step(i, h):
        h = a_ref[pl.ds(i, 1), :] * h + b_ref[pl.ds(i, 1), :]
        b_ref[pl.ds(i, 1), :] = h
        return h

    h_end = lax.fori_loop(0, tt, step, hcar_ref[...], unroll=8)
    hcar_ref[...] = h_end
    out_ref[...] = (b_ref[...] * jax.nn.gelu(gate_ref[...])).astype(out_ref.dtype)

    @pl.when(t == pl.num_programs(1) - 1)
    def _():
        hlast_ref[0] = h_end


def _lru(ug, conv_buf, h0, P, j, B, T, tt_pref=256):
    M = ug.shape[0]
    W = ug.shape[1] // 2
    heads = P['gaw'].shape[1]
    width = P['cw'].shape[1]
    tt = _tile(T, tt_pref)
    assert tt % SUBLANES == 0
    nt = T // tt
    row_u = lambda b, t: (b * nt + t, 0)
    row_g = lambda b, t: (b * nt + t, 1)
    names = ('cw', 'cb', 'gaw', 'gab', 'gxw', 'gxb', 'lam')
    return pl.pallas_call(
        functools.partial(_lru_kernel, tt=tt, cw=width, heads=heads),
        out_shape=(jax.ShapeDtypeStruct((M, W), BF16), jax.ShapeDtypeStruct((B, 1, W), F32)),
        grid=(B, nt),
        in_specs=[pl.BlockSpec((tt, W), row_u), pl.BlockSpec((tt, W), row_g),
                  pl.BlockSpec((1, width - 1, W), lambda b, t: (b, 0, 0)),
                  pl.BlockSpec((1, 1, W), lambda b, t: (b, 0, 0))] + [_sel(P[n], j) for n in names],
        out_specs=(pl.BlockSpec((tt, W), row_u), pl.BlockSpec((1, 1, W), lambda b, t: (b, 0, 0))),
        scratch_shapes=[pltpu.VMEM((SUBLANES + tt, W), F32), pltpu.VMEM((1, W), F32),
                        pltpu.VMEM((tt, W), F32), pltpu.VMEM((tt, W), F32)],
        compiler_params=_params("arbitrary", "arbitrary"),
        name="rg_lru",
    )(ug, ug, conv_buf, h0, *[P[n] for n in names])


def _mla_kernel(cqn_ref, wuq_ref, wukt_ref, cos_ref, sin_ref, ckv_ref, kpe_ref, wuv_ref, o_ref,
                qlat_ref, qpe_ref, m_ref, l_ref, acc_ref, sa_ref, sb_ref,
                *, tq, tk, hg, q_pos0, n_keys, scale):
    heads, nope, lat = wukt_ref.shape
    rope_d = qpe_ref.shape[2]
    vdim = wuv_ref.shape[2]
    rows = hg * tq
    n_groups = heads // hg
    assert n_groups == 1 or n_groups % 2 == 0

    q = jnp.dot(cqn_ref[...], wuq_ref[...], preferred_element_type=F32)
    qn = q[:, :heads * nope].astype(BF16)
    pe = _rope_lanes(q[:, heads * nope:], cos_ref[...], sin_ref[...])
    for h in range(heads):
        qlat_ref[h] = jnp.dot(qn[:, h * nope:(h + 1) * nope], wukt_ref[h],
                              preferred_element_type=F32).astype(BF16)
        qpe_ref[h] = pe[:, h * rope_d:(h + 1) * rope_d].astype(BF16)
    qi = pl.program_id(1)
    q_lo = q_pos0 + qi * tq
    vis_all = jnp.minimum(((q_lo >> CHUNK_SHIFT) + 1) * CHUNK, n_keys)
    vis_any = jnp.minimum((((q_lo + tq - 1) >> CHUNK_SHIFT) + 1) * CHUNK, n_keys)
    n_full = vis_all // tk
    n_tot = (vis_any + tk - 1) // tk

    m_ref[...] = jnp.full_like(m_ref, NEG_BIG)
    l_ref[...] = jnp.zeros_like(l_ref)
    acc_ref[...] = jnp.zeros_like(acc_ref)

    def block(kj, carry, masked):
        k0 = pl.multiple_of(kj * tk, tk)
        kc = ckv_ref[0, pl.ds(k0, tk), :]
        kp = kpe_ref[0, pl.ds(k0, tk), :]
        if masked:
            kpos = k0 + lax.broadcasted_iota(jnp.int32, (rows, tk), 1)
            qpos = q_lo + (lax.broadcasted_iota(jnp.int32, (rows, tk), 0) & (tq - 1))
            ok = ((kpos >> CHUNK_SHIFT) <= (qpos >> CHUNK_SHIFT)) & (kpos < n_keys)
        def scores(g):
            hs = pl.ds(g * hg, hg)
            ql = qlat_ref[hs].reshape(rows, lat)
            qp = qpe_ref[hs].reshape(rows, rope_d)
            s = lax.dot_general(ql, kc, NT_DIMS, preferred_element_type=F32)
            return s + lax.dot_general(qp, kp, NT_DIMS, preferred_element_type=F32)

        def update(g, s):
            if masked:
                s = jnp.where(ok, s, NEG_BIG)
            m_old = m_ref[g]
            m_new = jnp.maximum(m_old, jnp.max(s, axis=1, keepdims=True))
            corr = jnp.exp2((m_old - m_new) * (scale * LOG2_E))
            p = jnp.exp2((s - m_new) * (scale * LOG2_E))
            l_ref[g] = corr * l_ref[g] + jnp.sum(p, axis=1, keepdims=True)
            acc_ref[g] = corr * acc_ref[g] + jnp.dot(p.astype(BF16), kc, preferred_element_type=F32)
            m_ref[g] = m_new

        if n_groups == 1:
            update(0, scores(0))
            return carry

        sa_ref[...] = scores(0)

        def pair(i, c):
            g = 2 * i
            sb_ref[...] = scores(g + 1)
            update(g, sa_ref[...])
            sa_ref[...] = scores(jnp.minimum(g + 2, n_groups - 1))
            update(g + 1, sb_ref[...])
            return c

        lax.fori_loop(0, n_groups // 2, pair, 0)
        return carry

    lax.fori_loop(0, n_full, functools.partial(block, masked=False), 0)
    lax.fori_loop(n_full, n_tot, functools.partial(block, masked=True), 0)
    for h in range(heads):
        g, r0 = h // hg, (h % hg) * tq
        o = (acc_ref[g, r0:r0 + tq, :] / l_ref[g, r0:r0 + tq, :]).astype(BF16)
        o_ref[:, h * vdim:(h + 1) * vdim] = jnp.dot(
            o, wuv_ref[h], preferred_element_type=F32).astype(o_ref.dtype)


def _mla_attention(cqn, ckv_all, kpe_all, P, j, cos_t, sin_t, B, T, q_pos0, n_keys, tq_pref=512, tk_pref=512):
    wuq, wukt, wuv = P['wuq'], P['wukt'], P['wuv']
    _, heads, nope, lat = wukt.shape
    rope_d = P['rope_d']
    vdim = wuv.shape[3]
    q_lora = cqn.shape[1]
    Tk = ckv_all.shape[1]
    tq = _tile(T, tq_pref)
    tk = _tile(Tk, tk_pref)
    nt = T // tq
    hg = max(1, min(heads, MLA_STACK_ROWS // tq))
    assert heads % hg == 0 and tq & (tq - 1) == 0 and cos_t.shape[0] == T
    scale = (nope + rope_d) ** -0.5
    return pl.pallas_call(
        functools.partial(_mla_kernel, tq=tq, tk=tk, hg=hg, q_pos0=q_pos0, n_keys=n_keys, scale=scale),
        out_shape=jax.ShapeDtypeStruct((B * T, heads * vdim), BF16),
        grid=(B, nt),
        in_specs=[pl.BlockSpec((tq, q_lora), lambda b, t: (b * nt + t, 0)),
                  _sel(wuq, j), _sel(wukt, j),
                  pl.BlockSpec((tq, heads * rope_d), lambda b, t: (t, 0)),
                  pl.BlockSpec((tq, heads * rope_d), lambda b, t: (t, 0)),
                  pl.BlockSpec((1, Tk, lat), lambda b, t: (b, 0, 0)),
                  pl.BlockSpec((1, Tk, rope_d), lambda b, t: (b, 0, 0)),
                  _sel(wuv, j)],
        out_specs=pl.BlockSpec((tq, heads * vdim), lambda b, t: (b * nt + t, 0)),
        scratch_shapes=[pltpu.VMEM((heads, tq, lat), BF16), pltpu.VMEM((heads, tq, rope_d), BF16),
                        pltpu.VMEM((heads // hg, hg * tq, 1), F32), pltpu.VMEM((heads // hg, hg * tq, 1), F32),
                        pltpu.VMEM((heads // hg, hg * tq, lat), F32),
                        pltpu.VMEM((hg * tq, tk), F32), pltpu.VMEM((hg * tq, tk), F32)],
        compiler_params=_params("arbitrary", "arbitrary"),
        name="mla_attention",
    )(cqn, wuq, wukt, cos_t, sin_t, ckv_all, kpe_all, wuv)


def _xattn_kernel(q_ref, k_ref, v_ref, o_ref, *, heads, scale):
    hd = q_ref.shape[1] // heads
    for h in range(heads):
        sl = slice(h * hd, (h + 1) * hd)
        q = q_ref[:, sl]
        k = k_ref[:, sl].astype(BF16)
        v = v_ref[:, sl].astype(BF16)
        s = lax.dot_general(q, k, NT_DIMS, preferred_element_type=F32) * scale
        p = jnp.exp(s - jnp.max(s, axis=1, keepdims=True))
        l = jnp.sum(p, axis=1, keepdims=True)
        o = jnp.dot(p.astype(BF16), v, preferred_element_type=F32) / l
        o_ref[:, sl] = o.astype(o_ref.dtype)


def _xattn(q, mem_k, mem_v, l, B, T, tq_pref=512):
    M, D = q.shape
    n_mem = mem_k.shape[2]
    tq = _tile(T, tq_pref)
    nt = T // tq
    scale = (D // MEM_HEADS) ** -0.5
    mem_spec = pl.BlockSpec((None, None, n_mem, D), lambda b, t: (l, b, 0, 0))
    return pl.pallas_call(
        functools.partial(_xattn_kernel, heads=MEM_HEADS, scale=scale),
        out_shape=jax.ShapeDtypeStruct((M, D), BF16),
        grid=(B, nt),
        in_specs=[pl.BlockSpec((tq, D), lambda b, t: (b * nt + t, 0)), mem_spec, mem_spec],
        out_specs=pl.BlockSpec((tq, D), lambda b, t: (b * nt + t, 0)),
        compiler_params=_params("arbitrary", "arbitrary"),
        name="mem_xattn",
    )(q, mem_k, mem_v)


def _xattn_ln_kernel(q_ref, k_ref, v_ref, wo_ref, res_ref, g_ref, b_ref, o_ref, *, heads, scale, alpha, sub):
    hd = q_ref.shape[1] // heads
    kb = k_ref[...].astype(BF16)
    vb = v_ref[...].astype(BF16)
    cols = [slice(h * hd, (h + 1) * hd) for h in range(heads)]
    attended = []
    for r0 in range(0, o_ref.shape[0], sub):
        rows = slice(r0, r0 + sub)
        scores = [lax.dot_general(q_ref[rows, sl], kb[:, sl], NT_DIMS, preferred_element_type=F32) for sl in cols]
        outs = []
        for s, sl in zip(scores, cols):
            p = jnp.exp((s - jnp.max(s, axis=1, keepdims=True)) * scale)
            l = jnp.sum(p, axis=1, keepdims=True)
            outs.append((jnp.dot(p.astype(BF16), vb[:, sl], preferred_element_type=F32) / l).astype(BF16))
        attended.append((rows, jnp.concatenate(outs, axis=1)))
    for rows, o in attended:
        y = jnp.dot(o, wo_ref[...], preferred_element_type=F32)
        o_ref[rows, :] = _layer_norm(alpha * res_ref[rows, :] + y, g_ref[...], b_ref[...])


def _xattn_ln(q, mem_k, mem_v, l, wo, res, ln_g, ln_b, ln_i, alpha, B, T, tq_pref=512, sub_pref=256):
    M, D = q.shape
    n_mem = mem_k.shape[2]
    tq = _tile(T, tq_pref)
    sub = _tile(tq, sub_pref)
    nt = T // tq
    scale = (D // MEM_HEADS) ** -0.5
    row = lambda b, t: (b * nt + t, 0)
    mem_spec = pl.BlockSpec((None, None, n_mem, D), lambda b, t: (l, b, 0, 0))
    return pl.pallas_call(
        functools.partial(_xattn_ln_kernel, heads=MEM_HEADS, scale=scale, alpha=alpha, sub=sub),
        out_shape=jax.ShapeDtypeStruct((M, D), F32),
        grid=(B, nt),
        in_specs=[pl.BlockSpec((tq, D), row), mem_spec, mem_spec, _sel(wo, l, single=True),
                  pl.BlockSpec((tq, D), row), _sel(ln_g, ln_i), _sel(ln_b, ln_i)],
        out_specs=pl.BlockSpec((tq, D), row),
        compiler_params=_params("arbitrary", "arbitrary"),
        name="mem_xattn_out_ln",
    )(q, mem_k, mem_v, wo, res, ln_g, ln_b)


def _pool_ln_kernel(x_ref, pbuf_ref, pw_ref, ps_ref, g_ref, b_ref, *rest, tt, sub, pos0, alpha, with_next):
    if with_next:
        wn_ref, o_ref, on_ref, full_ref, y_ref = rest
    else:
        o_ref, full_ref, y_ref = rest
    t = pl.program_id(1)
    halo = 2 * SUBLANES
    groups = pw_ref.shape[0]
    gw = pw_ref.shape[1]

    @pl.when(t == 0)
    def _():
        full_ref[0:halo, :] = pbuf_ref[0]

    full_ref[halo:halo + tt, :] = x_ref[...]
    normed = []
    for r0 in range(0, tt, sub):
        rows = slice(r0, r0 + sub)
        pos = pos0 + t * tt + r0 + lax.broadcasted_iota(jnp.int32, (sub, 1), 0)
        for g in range(groups):
            w = POOL_WINDOWS[g]
            sl = slice(g * gw, (g + 1) * gw)
            f = full_ref[r0:r0 + halo + sub, sl]
            s = f
            d = 1
            while d < w:
                s = s + pltpu.roll(s, d, 0)
                d *= 2
            cnt = jnp.minimum(pos + 1, w).astype(F32)
            dlt = (s[halo:] / cnt - f[halo:]).astype(BF16)
            y_ref[rows, sl] = jnp.dot(dlt, pw_ref[g], preferred_element_type=F32)
        xo = _layer_norm(alpha * x_ref[rows, :] + y_ref[rows, :] * ps_ref[...], g_ref[...], b_ref[...])
        o_ref[rows, :] = xo
        normed.append((rows, xo))
    full_ref[0:halo, :] = full_ref[tt:tt + halo, :]
    if with_next:
        for rows, xo in normed:
            on_ref[rows, :] = jnp.dot(xo.astype(BF16), wn_ref[...], preferred_element_type=F32).astype(on_ref.dtype)


def _pool_ln(x, pbuf16, P, j, ln_i, alpha, B, T, pos0, w_next=None, next_l=None, tt_pref=512, sub_pref=256):
    M, D = x.shape
    tt = _tile(T, tt_pref)
    sub = _tile(tt, sub_pref)
    nt = T // tt
    halo = 2 * SUBLANES
    assert max(POOL_WINDOWS) <= halo and tt >= halo
    row = lambda bb, t: (bb * nt + t, 0)
    in_specs = [pl.BlockSpec((tt, D), row),
                pl.BlockSpec((1, halo, D), lambda bb, t: (bb, 0, 0)),
                _sel(P['pw'], j), _sel(P['ps'], j), _sel(P['ln_g'], ln_i), _sel(P['ln_b'], ln_i)]
    out_shape = [jax.ShapeDtypeStruct((M, D), F32)]
    out_specs = [pl.BlockSpec((tt, D), row)]
    operands = [x, pbuf16, P['pw'], P['ps'], P['ln_g'], P['ln_b']]
    if w_next is not None:
        n_next = w_next.shape[2]
        in_specs.append(_sel(w_next, next_l, single=True))
        out_shape.append(jax.ShapeDtypeStruct((M, n_next), BF16))
        out_specs.append(pl.BlockSpec((tt, n_next), row))
        operands.append(w_next)
    out = pl.pallas_call(
        functools.partial(_pool_ln_kernel, tt=tt, sub=sub, pos0=pos0, alpha=alpha, with_next=w_next is not None),
        out_shape=out_shape,
        grid=(B, nt),
        in_specs=in_specs,
        out_specs=out_specs,
        scratch_shapes=[pltpu.VMEM((halo + tt, D), F32), pltpu.VMEM((tt, D), F32)],
        compiler_params=_params("arbitrary", "arbitrary"),
        name="pool_res_ln",
    )(*operands)
    return out[0] if w_next is None else out


def _rope_tables(pos, rope_d, heads):
    half = rope_d // 2
    inv = ROPE_THETA ** (-jnp.arange(half, dtype=F32) / half)
    ang = pos.astype(F32)[:, None] * inv[None, :]
    cos, sin = jnp.cos(ang), jnp.sin(ang)
    cos_t = jnp.tile(jnp.concatenate([cos, cos], axis=1), (1, heads))
    sin_t = jnp.tile(jnp.concatenate([-sin, sin], axis=1), (1, heads))
    return cos_t, sin_t


def _prep_weights(w_in, q_norm_g, w_uq, kv_norm_g, w_uk, w_uv, conv_w, conv_b,
                  gate_a_w, gate_a_b, gate_x_w, gate_x_b, lru_lambda, w_out,
                  pool_w, pool_scale, xa_wq, xa_wo, ln_g, ln_b):
    n_even = w_in.shape[0]
    d_model = ln_g.shape[2]
    q_lora = q_norm_g.shape[1]
    kv_lora = kv_norm_g.shape[1]
    heads, nope = w_uk.shape[2], w_uk.shape[3]
    rope_d = w_uq.shape[2] // heads - nope
    wuq = w_uq.reshape(n_even, q_lora, heads, nope + rope_d)
    wuq = jnp.concatenate([wuq[..., :nope].reshape(n_even, q_lora, heads * nope),
                           wuq[..., nope:].reshape(n_even, q_lora, heads * rope_d)], axis=2).astype(BF16)
    wukt = jnp.transpose(w_uk, (0, 2, 3, 1)).astype(BF16)
    wuv = jnp.transpose(w_uv, (0, 2, 1, 3)).astype(BF16)
    row = lambda a: a.reshape(a.shape[0], 1, -1)
    return dict(
        w_in=w_in.astype(BF16), qg=row(q_norm_g), kg=row(kv_norm_g), wuq=wuq, wukt=wukt, wuv=wuv,
        cw=conv_w, cb=row(conv_b), gaw=gate_a_w.astype(BF16), gab=row(gate_a_b),
        gxw=gate_x_w.astype(BF16), gxb=row(gate_x_b), lam=row(lru_lambda),
        w_out=w_out.astype(BF16), pw=pool_w.astype(BF16), ps=row(pool_scale),
        wq=xa_wq.astype(BF16), wo=xa_wo.astype(BF16),
        ln_g=ln_g.reshape(-1, 1, d_model), ln_b=ln_b.reshape(-1, 1, d_model),
        q_lora=q_lora, kv_lora=kv_lora, heads=heads, nope=nope, rope_d=rope_d)


N_NORMS = 3


class _Group:
    def __init__(self, x3, t_past, mem_k, mem_v, conv_buf, h0, ckv_past, kpe_past, pool_buf, P):
        self.B, self.T, self.D = x3.shape
        B, T = self.B, self.T
        self.x = x3.reshape(B * T, self.D)
        self.t_past, self.mem_k, self.mem_v = t_past, mem_k, mem_v
        self.conv_buf, self.h0, self.ckv_past, self.kpe_past, self.pool_buf = (
            conv_buf, h0, ckv_past, kpe_past, pool_buf)
        self.cos_t, self.sin_t = _rope_tables(t_past + jnp.arange(T), P['rope_d'], P['heads'])
        self.tm_in = _tile(T, IN_PROJ_ROWS) if T >= IN_PROJ_ROWS else B * T
        reps = max(1, self.tm_in // T)
        self.cos_rows = jnp.tile(self.cos_t, (reps, 1))[:, :LANES]
        self.sin_rows = jnp.tile(self.sin_t, (reps, 1))[:, :LANES]
        self.convs, self.hs, self.ckvs, self.kpes, self.pools = [], [], [], [], []

    def pre_mlp(self, l, P, alpha):
        B, T, D = self.B, self.T, self.D
        x, t_past, mem_k, mem_v = self.x, self.t_past, self.mem_k, self.mem_v
        conv_buf, h0, ckv_past, kpe_past, pool_buf = (
            self.conv_buf, self.h0, self.ckv_past, self.kpe_past, self.pool_buf)
        cos_t, sin_t = self.cos_t, self.sin_t
        convs, hs, ckvs, kpes, pools = self.convs, self.hs, self.ckvs, self.kpes, self.pools
        n_norms = N_NORMS
        ln_g, ln_b = P['ln_g'], P['ln_b']
        j = l // 2
        if l % 2 == 0:
            ug, cqn, ckv, kpe = _in_proj(x, P, j, self.cos_rows, self.sin_rows, self.tm_in)
            lru_w = ug.shape[1] // 2
            rec, h_last = _lru(ug, conv_buf[j], h0[j][:, None, :], P, j, B, T)
            ckv3 = ckv.reshape(B, T, -1)
            kpe3 = kpe.reshape(B, T, -1)
            n_keys = ckv_past[j].shape[1] + T
            tk = 512 if T >= 512 and n_keys % 512 == 0 else n_keys + (-n_keys) % 256
            padk = (-n_keys) % tk
            ckv_all = jnp.concatenate([ckv_past[j].astype(BF16), ckv3.astype(BF16),
                                       jnp.zeros((B, padk, ckv3.shape[2]), BF16)], axis=1)
            kpe_all = jnp.concatenate([kpe_past[j].astype(BF16), kpe3.astype(BF16),
                                       jnp.zeros((B, padk, kpe3.shape[2]), BF16)], axis=1)
            attn = _mla_attention(cqn, ckv_all, kpe_all, P, j, cos_t, sin_t, B, T, t_past, n_keys,
                                  tk_pref=tk)
            x, q = _mm_ln([rec, attn], P['w_out'], j, x, ln_g, ln_b, n_norms * l, alpha,
                          w_next=P['wq'], next_l=l)
            tail = P['cw'].shape[1] - 1
            convs.append(ug.reshape(B, T, -1)[:, T - tail:, :lru_w])
            hs.append(h_last[:, 0, :])
            ckvs.append(ckv3)
            kpes.append(kpe3)
        else:
            nbuf = pool_buf[j].shape[1]
            pools.append(x.reshape(B, T, D)[:, T - nbuf:])
            pbuf16 = jnp.pad(pool_buf[j], ((0, 0), (2 * SUBLANES - nbuf, 0), (0, 0)))
            if T >= XATTN_FUSE_ROWS:
                x, q = _pool_ln(x, pbuf16, P, j, n_norms * l, alpha, B, T, t_past, P['wq'], l)
            else:
                x = _pool_ln(x, pbuf16, P, j, n_norms * l, alpha, B, T, t_past)
                q = _rows_matmul(x, P['wq'], BF16, g0=l, n_g=1, tn_pref=512)[0]
        if T >= XATTN_FUSE_ROWS:
            x = _xattn_ln(q, mem_k, mem_v, l, P['wo'], x, ln_g, ln_b, n_norms * l + 1, alpha, B, T)
        else:
            o = _xattn(q, mem_k, mem_v, l, B, T)
            x = _mm_ln([o], P['wo'], l, x, ln_g, ln_b, n_norms * l + 1, alpha)
        self.x = x

    def results(self):
        return (self.x.reshape(self.B, self.T, self.D), jnp.stack(self.convs), jnp.stack(self.hs),
                jnp.stack(self.ckvs), jnp.stack(self.kpes), jnp.stack(self.pools))


def kernel(x_prompt, x_sample, mem_prompt, cache_mem_k, cache_mem_v, cache_mla_ckv, cache_mla_kpe,
           state_rglru_h, state_rglru_conv, state_pool,
           w_in, q_norm_g, w_uq, kv_norm_g, w_uk, w_uv, conv_w, conv_b,
           gate_a_w, gate_a_b, gate_x_w, gate_x_b, lru_lambda, w_out,
           pool_w, pool_scale, xa_wq, xa_wk, xa_wv, xa_wo, mlp_up, mlp_down, ln_g, ln_b):
    P = _prep_weights(w_in, q_norm_g, w_uq, kv_norm_g, w_uk, w_uv, conv_w, conv_b,
                      gate_a_w, gate_a_b, gate_x_w, gate_x_b, lru_lambda, w_out,
                      pool_w, pool_scale, xa_wq, xa_wo, ln_g, ln_b)
    depth = ln_g.shape[0]
    n_even, n_odd = w_in.shape[0], pool_w.shape[0]
    bp, _, d_model = x_prompt.shape
    n_mem = mem_prompt.shape[1]
    dt = x_prompt.dtype
    lru_w = conv_w.shape[2]
    mem_rows = mem_prompt.reshape(bp * n_mem, d_model)
    p_mem_k = _rows_matmul(mem_rows, xa_wk, F32, tm_pref=1024, tn_pref=512).reshape(depth, bp, n_mem, d_model)
    p_mem_v = _rows_matmul(mem_rows, xa_wv, F32, tm_pref=1024, tn_pref=512).reshape(depth, bp, n_mem, d_model)
    prompt = _Group(
        x_prompt, 0, p_mem_k, p_mem_v,
        jnp.zeros((n_even, bp, conv_w.shape[1] - 1, lru_w), dt),
        jnp.zeros((n_even, bp, lru_w), dt),
        jnp.zeros((n_even, bp, 0, kv_norm_g.shape[1]), dt),
        jnp.zeros((n_even, bp, 0, cache_mla_kpe.shape[3]), dt),
        jnp.zeros((n_odd, bp, state_pool.shape[2], d_model), dt), P)
    sample = _Group(
        x_sample, cache_mla_ckv.shape[2], cache_mem_k, cache_mem_v,
        state_rglru_conv, state_rglru_h, cache_mla_ckv, cache_mla_kpe, state_pool, P)
    alpha = (2.0 * depth) ** 0.25
    ln_g3, ln_b3 = P['ln_g'], P['ln_b']
    for l in range(depth):
        sample.pre_mlp(l, P, alpha)
        prompt.pre_mlp(l, P, alpha)
        ln_i = N_NORMS * l + 2
        sample.x, up_b, down_b = _mlp_ln(sample.x, mlp_up, mlp_down, l, ln_g3, ln_b3, ln_i, alpha,
                                         emit=True, tf_pref=512)
        prompt.x = _mlp_ln(prompt.x, up_b, down_b, None, ln_g3, ln_b3, ln_i, alpha)
    y_prompt, p_conv, p_h, p_ckv, p_kpe, p_pool = prompt.results()
    y_sample, s_conv, s_h, s_ckv, s_kpe, s_pool = sample.results()
    return (y_prompt, y_sample, p_conv, p_h, p_ckv, p_kpe, p_pool, p_mem_k, p_mem_v,
            s_conv, s_h, s_ckv, s_kpe, s_pool)
```

```python
import functools

import jax
import jax.numpy as jnp
from jax import lax
from jax.experimental import pallas as pl
from jax.experimental.pallas import tpu as pltpu

F32 = jnp.float32
BF16 = jnp.bfloat16

CHUNK = 64
CHUNK_SHIFT = 6
assert 1 << CHUNK_SHIFT == CHUNK
MEM_HEADS = 4
POOL_WINDOWS = (2, 4, 8, 16)
LRU_C = 8.0
ROPE_THETA = 10000.0
LN_EPS = 1e-5
RMS_EPS = 1e-6
NEG_BIG = -1e30
LOG2_E = 1.4426950408889634

V7X_VMEM_BYTES = 64 * 1024 * 1024
VMEM_LIMIT = V7X_VMEM_BYTES - 8 * 1024 * 1024
LANES = 128
SUBLANES = 8
IN_PROJ_ROWS = 512
MLA_STACK_ROWS = 512
XATTN_FUSE_ROWS = 256
NT_DIMS = (((1,), (1,)), ((), ()))


def _params(*sem):
    return pltpu.CompilerParams(dimension_semantics=sem, vmem_limit_bytes=VMEM_LIMIT)


def _tile(n, pref):
    if n <= pref:
        return n
    t = pref
    while n % t:
        t -= SUBLANES
    assert t > 0
    return t


def _sel(arr, *lead, single=False):
    n_lead = len(lead)
    rest = arr.shape[n_lead:]
    idx = tuple(lead) + (0,) * len(rest)
    mode = dict(pipeline_mode=pl.Buffered(1)) if single else {}
    return pl.BlockSpec((None,) * n_lead + tuple(rest), lambda *_: idx, **mode)


def _layer_norm(y, g, b):
    mu = jnp.mean(y, axis=-1, keepdims=True)
    d = y - mu
    var = jnp.mean(d * d, axis=-1, keepdims=True)
    return d * lax.rsqrt(var + LN_EPS) * g + b


def _rms_norm(x, g):
    return x * lax.rsqrt(jnp.mean(x * x, axis=-1, keepdims=True) + RMS_EPS) * g


def _rope_lanes(x, cos_t, sin_t):
    n = x.shape[-1]
    half = 32
    lane = lax.broadcasted_iota(jnp.int32, x.shape, x.ndim - 1)
    first = (lane % (2 * half)) < half
    swapped = jnp.where(first, pltpu.roll(x, n - half, x.ndim - 1), pltpu.roll(x, half, x.ndim - 1))
    return x * cos_t + swapped * sin_t


def _rows_matmul_kernel(a_ref, w_ref, o_ref, *scratch, one_tile):
    if one_tile:
        ab_ref, = scratch

        @pl.when((pl.program_id(0) == 0) & (pl.program_id(1) == 0))
        def _():
            ab_ref[...] = a_ref[...].astype(BF16)

        a = ab_ref[...]
    else:
        a = a_ref[...].astype(BF16)
    w = w_ref[...].astype(BF16)
    o_ref[...] = jnp.dot(a, w, preferred_element_type=F32).astype(o_ref.dtype)


def _rows_matmul(a, w, out_dtype, g0=0, n_g=None, tm_pref=512, tn_pref=None):
    M, K = a.shape
    N = w.shape[2]
    n_g = w.shape[0] if n_g is None else n_g
    tm = _tile(M, tm_pref)
    tn = N if tn_pref is None else _tile(N, tn_pref)
    one_tile = M == tm and a.dtype != BF16 and n_g * (N // tn) > 1
    return pl.pallas_call(
        functools.partial(_rows_matmul_kernel, one_tile=one_tile),
        out_shape=jax.ShapeDtypeStruct((n_g, M, N), out_dtype),
        grid=(n_g, N // tn, M // tm),
        in_specs=[pl.BlockSpec((tm, K), lambda g, n, i: (i, 0)),
                  pl.BlockSpec((None, K, tn), lambda g, n, i: (g0 + g, 0, n))],
        out_specs=pl.BlockSpec((None, tm, tn), lambda g, n, i: (g, i, n)),
        scratch_shapes=[pltpu.VMEM((tm, K), BF16)] if one_tile else [],
        compiler_params=_params("arbitrary", "arbitrary", "arbitrary"),
        name="rows_matmul",
    )(a, w)


def _mm_ln_kernel(*refs, n_a, alpha, sub, with_next):
    a_refs = refs[:n_a]
    if with_next:
        w_ref, res_ref, g_ref, b_ref, wn_ref, o_ref, on_ref = refs[n_a:]
    else:
        w_ref, res_ref, g_ref, b_ref, o_ref = refs[n_a:]
    normed = []
    for r0 in range(0, o_ref.shape[0], sub):
        rows = slice(r0, r0 + sub)
        y = None
        off = 0
        for a_ref in a_refs:
            k = a_ref.shape[1]
            part = jnp.dot(a_ref[rows, :].astype(BF16), w_ref[off:off + k, :], preferred_element_type=F32)
            y = part if y is None else y + part
            off += k
        xo = _layer_norm(alpha * res_ref[rows, :] + y, g_ref[...], b_ref[...])
        o_ref[rows, :] = xo
        normed.append((rows, xo))
    if with_next:
        for rows, xo in normed:
            on_ref[rows, :] = jnp.dot(xo.astype(BF16), wn_ref[...], preferred_element_type=F32).astype(on_ref.dtype)


def _mm_ln(a_list, w, wl, res, ln_g, ln_b, ln_i, alpha, w_next=None, next_l=None, tm_pref=512, sub_pref=256):
    M, N = res.shape
    tm = _tile(M, tm_pref)
    sub = _tile(tm, sub_pref)
    row = lambda i: (i, 0)
    in_specs = [pl.BlockSpec((tm, a.shape[1]), row) for a in a_list]
    in_specs += [_sel(w, wl, single=True), pl.BlockSpec((tm, N), row), _sel(ln_g, ln_i), _sel(ln_b, ln_i)]
    out_shape = [jax.ShapeDtypeStruct((M, N), F32)]
    out_specs = [pl.BlockSpec((tm, N), row)]
    operands = list(a_list) + [w, res, ln_g, ln_b]
    if w_next is not None:
        n_next = w_next.shape[2]
        in_specs.append(_sel(w_next, next_l, single=True))
        out_shape.append(jax.ShapeDtypeStruct((M, n_next), BF16))
        out_specs.append(pl.BlockSpec((tm, n_next), row))
        operands.append(w_next)
    out = pl.pallas_call(
        functools.partial(_mm_ln_kernel, n_a=len(a_list), alpha=alpha, sub=sub, with_next=w_next is not None),
        out_shape=out_shape,
        grid=(M // tm,),
        in_specs=in_specs,
        out_specs=out_specs,
        compiler_params=_params("arbitrary"),
        name="matmul_res_ln",
    )(*operands)
    return out[0] if w_next is None else out


def _mlp_ln_kernel(x_ref, wu_ref, wd_ref, g_ref, b_ref, o_ref, *rest, alpha, sub, emit, n_steps):
    if emit:
        wub_ref, wdb_ref, xb_ref, acc_ref = rest
    else:
        xb_ref, acc_ref = rest
    j = pl.program_id(1)
    last = n_steps - 1

    def step(first, final):
        wu = wu_ref[...].astype(BF16)
        wd = wd_ref[...].astype(BF16)
        if emit:
            wub_ref[...] = wu
            wdb_ref[...] = wd
        if first:
            xb = x_ref[...].astype(BF16)
            xb_ref[...] = xb
        else:
            xb = xb_ref[...]
        h = jnp.dot(xb, wu, preferred_element_type=F32)
        h = jnp.square(jnp.maximum(h, 0.0)).astype(BF16)
        if not final:
            part = jnp.dot(h, wd, preferred_element_type=F32)
            if first:
                acc_ref[...] = part
            else:
                acc_ref[...] += part
            return
        for r0 in range(0, o_ref.shape[0], sub):
            rows = slice(r0, r0 + sub)
            y = jnp.dot(h[rows, :], wd, preferred_element_type=F32)
            if not first:
                y = acc_ref[rows, :] + y
            o_ref[rows, :] = _layer_norm(alpha * x_ref[rows, :] + y, g_ref[...], b_ref[...])

    if n_steps == 1:
        step(True, True)
    else:
        pl.when(j == 0)(functools.partial(step, True, False))
        if n_steps > 2:
            pl.when((j > 0) & (j < last))(functools.partial(step, False, False))
        pl.when(j == last)(functools.partial(step, False, True))


def _mlp_ln(x, w_up, w_down, l, ln_g, ln_b, ln_i, alpha, emit=False, tm_pref=512, tf_pref=1024,
            sub_pref=256):
    M, D = x.shape
    FF = w_up.shape[-1]
    tm = _tile(M, tm_pref)
    tf = _tile(FF, tf_pref)
    sub = _tile(tm, sub_pref)
    out_shape = [jax.ShapeDtypeStruct((M, D), F32)]
    out_specs = [pl.BlockSpec((tm, D), lambda i, j: (i, 0))]
    if emit:
        assert M == tm
        w_specs = [pl.BlockSpec((None, D, tf), lambda i, j: (l, 0, j)),
                   pl.BlockSpec((None, tf, D), lambda i, j: (l, j, 0))]
        out_shape += [jax.ShapeDtypeStruct((D, FF), BF16), jax.ShapeDtypeStruct((FF, D), BF16)]
        out_specs += [pl.BlockSpec((D, tf), lambda i, j: (0, j)), pl.BlockSpec((tf, D), lambda i, j: (j, 0))]
    else:
        w_specs = [pl.BlockSpec((D, tf), lambda i, j: (0, j)), pl.BlockSpec((tf, D), lambda i, j: (j, 0))]
    out = pl.pallas_call(
        functools.partial(_mlp_ln_kernel, alpha=alpha, sub=sub, emit=emit, n_steps=FF // tf),
        out_shape=out_shape,
        grid=(M // tm, FF // tf),
        in_specs=[pl.BlockSpec((tm, D), lambda i, j: (i, 0))] + w_specs + [_sel(ln_g, ln_i), _sel(ln_b, ln_i)],
        out_specs=out_specs,
        scratch_shapes=[pltpu.VMEM((tm, D), BF16), pltpu.VMEM((tm, D), F32)],
        compiler_params=_params("arbitrary", "arbitrary"),
        name="mlp_res_ln_round" if emit else "mlp_res_ln",
    )(x, w_up, w_down, ln_g, ln_b)
    return out if emit else out[0]


def _in_proj_kernel(x_ref, w_ref, qg_ref, kg_ref, cos_ref, sin_ref,
                    ug_ref, cqn_ref, ckv_ref, kpe_ref, *, ug_cols, q_lora, kv_lora, qk_rope):
    xb = x_ref[...].astype(BF16)
    z = jnp.dot(xb, w_ref[:, ug_cols:], preferred_element_type=F32)
    cqn_ref[...] = _rms_norm(z[:, :q_lora], qg_ref[...]).astype(BF16)
    ckv_ref[...] = _rms_norm(z[:, q_lora:q_lora + kv_lora], kg_ref[...])
    pe = z[:, q_lora + kv_lora:]
    pad = cos_ref.shape[1] - qk_rope
    pe = jnp.concatenate([pe, jnp.zeros((pe.shape[0], pad), F32)], axis=1)
    kpe_ref[...] = _rope_lanes(pe, cos_ref[...], sin_ref[...])[:, :qk_rope]
    ug_ref[...] = jnp.dot(xb, w_ref[:, :ug_cols], preferred_element_type=F32)


def _in_proj(x, P, j, cos_t, sin_t, tm):
    M, D = x.shape
    w = P['w_in']
    q_lora, kv_lora, qk_rope = P['q_lora'], P['kv_lora'], P['rope_d']
    ug_cols = w.shape[2] - q_lora - kv_lora - qk_rope
    assert M % tm == 0 and cos_t.shape[0] % tm == 0 and ug_cols % LANES == 0
    n_pos_blocks = cos_t.shape[0] // tm
    pe_w = cos_t.shape[1]
    row = lambda i: (i, 0)
    pos = lambda i: (i % n_pos_blocks, 0)
    return pl.pallas_call(
        functools.partial(_in_proj_kernel, ug_cols=ug_cols, q_lora=q_lora, kv_lora=kv_lora, qk_rope=qk_rope),
        out_shape=(jax.ShapeDtypeStruct((M, ug_cols), F32),
                   jax.ShapeDtypeStruct((M, q_lora), BF16),
                   jax.ShapeDtypeStruct((M, kv_lora), F32),
                   jax.ShapeDtypeStruct((M, qk_rope), F32)),
        grid=(M // tm,),
        in_specs=[pl.BlockSpec((tm, D), row), _sel(w, j, single=True),
                  _sel(P['qg'], j), _sel(P['kg'], j),
                  pl.BlockSpec((tm, pe_w), pos), pl.BlockSpec((tm, pe_w), pos)],
        out_specs=(pl.BlockSpec((tm, ug_cols), row), pl.BlockSpec((tm, q_lora), row),
                   pl.BlockSpec((tm, kv_lora), row), pl.BlockSpec((tm, qk_rope), row)),
        compiler_params=_params("arbitrary"),
        name="in_proj",
    )(x, w, P['qg'], P['kg'], cos_t, sin_t)


def _lru_kernel(u_ref, gate_ref, cbuf_ref, h0_ref, cw_ref, cb_ref, gaw_ref, gab_ref,
                gxw_ref, gxb_ref, lam_ref, out_ref, hlast_ref,
                ubuf_ref, hcar_ref, a_ref, b_ref, *, tt, cw, heads):
    t = pl.program_id(1)
    pad = SUBLANES
    tail = cw - 1

    @pl.when(t == 0)
    def _():
        ubuf_ref[pad - tail:pad, :] = cbuf_ref[0]
        hcar_ref[...] = h0_ref[0]

    ubuf_ref[pad:pad + tt, :] = u_ref[...]
    rows_all = ubuf_ref[...]
    uc = cb_ref[...] + cw_ref[tail:cw, :] * rows_all[pad:, :]
    for k in range(tail):
        uc = uc + cw_ref[k:k + 1, :] * pltpu.roll(rows_all, tail - k, 0)[pad:, :]
    ubuf_ref[pad - tail:pad, :] = ubuf_ref[pad + tt - tail:pad + tt, :]

    width = uc.shape[1]
    blk = width // heads
    ucb = uc.astype(BF16)
    rs, igs = [], []
    for h in range(heads):
        uh = ucb[:, h * blk:(h + 1) * blk]
        rs.append(jnp.dot(uh, gaw_ref[h], preferred_element_type=F32))
        igs.append(jnp.dot(uh, gxw_ref[h], preferred_element_type=F32))
    r = jax.nn.sigmoid(jnp.concatenate(rs, axis=1) + gab_ref[...])
    ig = jax.nn.sigmoid(jnp.concatenate(igs, axis=1) + gxb_ref[...])
    nlam = -lam_ref[...]
    softplus = jnp.maximum(nlam, 0.0) + jnp.log1p(jnp.exp(-jnp.abs(nlam)))
    log_a = -LRU_C * r * softplus
    a = jnp.exp(log_a)
    a_ref[...] = a
    b_ref[...] = jnp.sqrt(-jnp.tanh(log_a) * (a * a + 1.0)) * (ig * uc)

    def step(i, h):
        h = a_ref[pl.ds(i, 1), :] * h + b_ref[pl.ds(i, 1), :]
        b_ref[pl.ds(i, 1), :] = h
        return h

    h_end = lax.fori_loop(0, tt, step, hcar_ref[...], unroll=8)
    hcar_ref[...] = h_end
    out_ref[...] = (b_ref[...] * jax.nn.gelu(gate_ref[...])).astype(out_ref.dtype)

    @pl.when(t == pl.num_programs(1) - 1)
    def _():
        hlast_ref[0] = h_end


def _lru(ug, conv_buf, h0, P, j, B, T, tt_pref=256):
    M = ug.shape[0]
    W = ug.shape[1] // 2
    heads = P['gaw'].shape[1]
    width = P['cw'].shape[1]
    tt = _tile(T, tt_pref)
    assert tt % SUBLANES == 0
    nt = T // tt
    row_u = lambda b, t: (b * nt + t, 0)
    row_g = lambda b, t: (b * nt + t, 1)
    names = ('cw', 'cb', 'gaw', 'gab', 'gxw', 'gxb', 'lam')
    return pl.pallas_call(
        functools.partial(_lru_kernel, tt=tt, cw=width, heads=heads),
        out_shape=(jax.ShapeDtypeStruct((M, W), BF16), jax.ShapeDtypeStruct((B, 1, W), F32)),
        grid=(B, nt),
        in_specs=[pl.BlockSpec((tt, W), row_u), pl.BlockSpec((tt, W), row_g),
                  pl.BlockSpec((1, width - 1, W), lambda b, t: (b, 0, 0)),
                  pl.BlockSpec((1, 1, W), lambda b, t: (b, 0, 0))] + [_sel(P[n], j) for n in names],
        out_specs=(pl.BlockSpec((tt, W), row_u), pl.BlockSpec((1, 1, W), lambda b, t: (b, 0, 0))),
        scratch_shapes=[pltpu.VMEM((SUBLANES + tt, W), F32), pltpu.VMEM((1, W), F32),
                        pltpu.VMEM((tt, W), F32), pltpu.VMEM((tt, W), F32)],
        compiler_params=_params("arbitrary", "arbitrary"),
        name="rg_lru",
    )(ug, ug, conv_buf, h0, *[P[n] for n in names])


def _mla_kernel(cqn_ref, wuq_ref, wukt_ref, cos_ref, sin_ref, ckv_ref, kpe_ref, wuv_ref, o_ref,
                qlat_ref, qpe_ref, m_ref, l_ref, acc_ref, sa_ref, sb_ref,
                *, tq, tk, hg, q_pos0, n_keys, scale):
    heads, nope, lat = wukt_ref.shape
    rope_d = qpe_ref.shape[2]
    vdim = wuv_ref.shape[2]
    rows = hg * tq
    n_groups = heads // hg
    assert n_groups == 1 or n_groups % 2 == 0

    q = jnp.dot(cqn_ref[...], wuq_ref[...], preferred_element_type=F32)
    qn = q[:, :heads * nope].astype(BF16)
    pe = _rope_lanes(q[:, heads * nope:], cos_ref[...], sin_ref[...])
    for h in range(heads):
        qlat_ref[h] = jnp.dot(qn[:, h * nope:(h + 1) * nope], wukt_ref[h],
                              preferred_element_type=F32).astype(BF16)
        qpe_ref[h] = pe[:, h * rope_d:(h + 1) * rope_d].astype(BF16)
    qi = pl.program_id(1)
    q_lo = q_pos0 + qi * tq
    vis_all = jnp.minimum(((q_lo >> CHUNK_SHIFT) + 1) * CHUNK, n_keys)
    vis_any = jnp.minimum((((q_lo + tq - 1) >> CHUNK_SHIFT) + 1) * CHUNK, n_keys)
    n_full = vis_all // tk
    n_tot = (vis_any + tk - 1) // tk

    m_ref[...] = jnp.full_like(m_ref, NEG_BIG)
    l_ref[...] = jnp.zeros_like(l_ref)
    acc_ref[...] = jnp.zeros_like(acc_ref)

    def block(kj, carry, masked):
        k0 = pl.multiple_of(kj * tk, tk)
        kc = ckv_ref[0, pl.ds(k0, tk), :]
        kp = kpe_ref[0, pl.ds(k0, tk), :]
        if masked:
            kpos = k0 + lax.broadcasted_iota(jnp.int32, (rows, tk), 1)
            qpos = q_lo + (lax.broadcasted_iota(jnp.int32, (rows, tk), 0) & (tq - 1))
            ok = ((kpos >> CHUNK_SHIFT) <= (qpos >> CHUNK_SHIFT)) & (kpos < n_keys)
        def scores(g):
            hs = pl.ds(g * hg, hg)
            ql = qlat_ref[hs].reshape(rows, lat)
            qp = qpe_ref[hs].reshape(rows, rope_d)
            s = lax.dot_general(ql, kc, NT_DIMS, preferred_element_type=F32)
            return s + lax.dot_general(qp, kp, NT_DIMS, preferred_element_type=F32)

        def update(g, s):
            if masked:
                s = jnp.where(ok, s, NEG_BIG)
            m_old = m_ref[g]
            m_new = jnp.maximum(m_old, jnp.max(s, axis=1, keepdims=True))
            corr = jnp.exp2((m_old - m_new) * (scale * LOG2_E))
            p = jnp.exp2((s - m_new) * (scale * LOG2_E))
            l_ref[g] = corr * l_ref[g] + jnp.sum(p, axis=1, keepdims=True)
            acc_ref[g] = corr * acc_ref[g] + jnp.dot(p.astype(BF16), kc, preferred_element_type=F32)
            m_ref[g] = m_new

        if n_groups == 1:
            update(0, scores(0))
            return carry

        sa_ref[...] = scores(0)

        def pair(i, c):
            g = 2 * i
            sb_ref[...] = scores(g + 1)
            update(g, sa_ref[...])
            sa_ref[...] = scores(jnp.minimum(g + 2, n_groups - 1))
            update(g + 1, sb_ref[...])
            return c

        lax.fori_loop(0, n_groups // 2, pair, 0)
        return carry

    lax.fori_loop(0, n_full, functools.partial(block, masked=False), 0)
    lax.fori_loop(n_full, n_tot, functools.partial(block, masked=True), 0)
    for h in range(heads):
        g, r0 = h // hg, (h % hg) * tq
        o = (acc_ref[g, r0:r0 + tq, :] / l_ref[g, r0:r0 + tq, :]).astype(BF16)
        o_ref[:, h * vdim:(h + 1) * vdim] = jnp.dot(
            o, wuv_ref[h], preferred_element_type=F32).astype(o_ref.dtype)


def _mla_attention(cqn, ckv_all, kpe_all, P, j, cos_t, sin_t, B, T, q_pos0, n_keys, tq_pref=512, tk_pref=512):
    wuq, wukt, wuv = P['wuq'], P['wukt'], P['wuv']
    _, heads, nope, lat = wukt.shape
    rope_d = P['rope_d']
    vdim = wuv.shape[3]
    q_lora = cqn.shape[1]
    Tk = ckv_all.shape[1]
    tq = _tile(T, tq_pref)
    tk = _tile(Tk, tk_pref)
    nt = T // tq
    hg = max(1, min(heads, MLA_STACK_ROWS // tq))
    assert heads % hg == 0 and tq & (tq - 1) == 0 and cos_t.shape[0] == T
    scale = (nope + rope_d) ** -0.5
    return pl.pallas_call(
        functools.partial(_mla_kernel, tq=tq, tk=tk, hg=hg, q_pos0=q_pos0, n_keys=n_keys, scale=scale),
        out_shape=jax.ShapeDtypeStruct((B * T, heads * vdim), BF16),
        grid=(B, nt),
        in_specs=[pl.BlockSpec((tq, q_lora), lambda b, t: (b * nt + t, 0)),
                  _sel(wuq, j), _sel(wukt, j),
                  pl.BlockSpec((tq, heads * rope_d), lambda b, t: (t, 0)),
                  pl.BlockSpec((tq, heads * rope_d), lambda b, t: (t, 0)),
                  pl.BlockSpec((1, Tk, lat), lambda b, t: (b, 0, 0)),
                  pl.BlockSpec((1, Tk, rope_d), lambda b, t: (b, 0, 0)),
                  _sel(wuv, j)],
        out_specs=pl.BlockSpec((tq, heads * vdim), lambda b, t: (b * nt + t, 0)),
        scratch_shapes=[pltpu.VMEM((heads, tq, lat), BF16), pltpu.VMEM((heads, tq, rope_d), BF16),
                        pltpu.VMEM((heads // hg, hg * tq, 1), F32), pltpu.VMEM((heads // hg, hg * tq, 1), F32),
                        pltpu.VMEM((heads // hg, hg * tq, lat), F32),
                        pltpu.VMEM((hg * tq, tk), F32), pltpu.VMEM((hg * tq, tk), F32)],
        compiler_params=_params("arbitrary", "arbitrary"),
        name="mla_attention",
    )(cqn, wuq, wukt, cos_t, sin_t, ckv_all, kpe_all, wuv)


def _xattn_kernel(q_ref, k_ref, v_ref, o_ref, *, heads, scale):
    hd = q_ref.shape[1] // heads
    for h in range(heads):
        sl = slice(h * hd, (h + 1) * hd)
        q = q_ref[:, sl]
        k = k_ref[:, sl].astype(BF16)
        v = v_ref[:, sl].astype(BF16)
        s = lax.dot_general(q, k, NT_DIMS, preferred_element_type=F32) * scale
        p = jnp.exp(s - jnp.max(s, axis=1, keepdims=True))
        l = jnp.sum(p, axis=1, keepdims=True)
        o = jnp.dot(p.astype(BF16), v, preferred_element_type=F32) / l
        o_ref[:, sl] = o.astype(o_ref.dtype)


def _xattn(q, mem_k, mem_v, l, B, T, tq_pref=512):
    M, D = q.shape
    n_mem = mem_k.shape[2]
    tq = _tile(T, tq_pref)
    nt = T // tq
    scale = (D // MEM_HEADS) ** -0.5
    mem_spec = pl.BlockSpec((None, None, n_mem, D), lambda b, t: (l, b, 0, 0))
    return pl.pallas_call(
        functools.partial(_xattn_kernel, heads=MEM_HEADS, scale=scale),
        out_shape=jax.ShapeDtypeStruct((M, D), BF16),
        grid=(B, nt),
        in_specs=[pl.BlockSpec((tq, D), lambda b, t: (b * nt + t, 0)), mem_spec, mem_spec],
        out_specs=pl.BlockSpec((tq, D), lambda b, t: (b * nt + t, 0)),
        compiler_params=_params("arbitrary", "arbitrary"),
        name="mem_xattn",
    )(q, mem_k, mem_v)


def _xattn_ln_kernel(q_ref, k_ref, v_ref, wo_ref, res_ref, g_ref, b_ref, o_ref, *, heads, scale, alpha, sub):
    hd = q_ref.shape[1] // heads
    kb = k_ref[...].astype(BF16)
    vb = v_ref[...].astype(BF16)
    cols = [slice(h * hd, (h + 1) * hd) for h in range(heads)]
    attended = []
    for r0 in range(0, o_ref.shape[0], sub):
        rows = slice(r0, r0 + sub)
        scores = [lax.dot_general(q_ref[rows, sl], kb[:, sl], NT_DIMS, preferred_element_type=F32) for sl in cols]
        outs = []
        for s, sl in zip(scores, cols):
            p = jnp.exp((s - jnp.max(s, axis=1, keepdims=True)) * scale)
            l = jnp.sum(p, axis=1, keepdims=True)
            outs.append((jnp.dot(p.astype(BF16), vb[:, sl], preferred_element_type=F32) / l).astype(BF16))
        attended.append((rows, jnp.concatenate(outs, axis=1)))
    for rows, o in attended:
        y = jnp.dot(o, wo_ref[...], preferred_element_type=F32)
        o_ref[rows, :] = _layer_norm(alpha * res_ref[rows, :] + y, g_ref[...], b_ref[...])


def _xattn_ln(q, mem_k, mem_v, l, wo, res, ln_g, ln_b, ln_i, alpha, B, T, tq_pref=512, sub_pref=256):
    M, D = q.shape
    n_mem = mem_k.shape[2]
    tq = _tile(T, tq_pref)
    sub = _tile(tq, sub_pref)
    nt = T // tq
    scale = (D // MEM_HEADS) ** -0.5
    row = lambda b, t: (b * nt + t, 0)
    mem_spec = pl.BlockSpec((None, None, n_mem, D), lambda b, t: (l, b, 0, 0))
    return pl.pallas_call(
        functools.partial(_xattn_ln_kernel, heads=MEM_HEADS, scale=scale, alpha=alpha, sub=sub),
        out_shape=jax.ShapeDtypeStruct((M, D), F32),
        grid=(B, nt),
        in_specs=[pl.BlockSpec((tq, D), row), mem_spec, mem_spec, _sel(wo, l, single=True),
                  pl.BlockSpec((tq, D), row), _sel(ln_g, ln_i), _sel(ln_b, ln_i)],
        out_specs=pl.BlockSpec((tq, D), row),
        compiler_params=_params("arbitrary", "arbitrary"),
        name="mem_xattn_out_ln",
    )(q, mem_k, mem_v, wo, res, ln_g, ln_b)


def _pool_ln_kernel(x_ref, pbuf_ref, pw_ref, ps_ref, g_ref, b_ref, *rest, tt, sub, pos0, alpha, with_next):
    if with_next:
        wn_ref, o_ref, on_ref, full_ref, y_ref = rest
    else:
        o_ref, full_ref, y_ref = rest
    t = pl.program_id(1)
    halo = 2 * SUBLANES
    groups = pw_ref.shape[0]
    gw = pw_ref.shape[1]

    @pl.when(t == 0)
    def _():
        full_ref[0:halo, :] = pbuf_ref[0]

    full_ref[halo:halo + tt, :] = x_ref[...]
    normed = []
    for r0 in range(0, tt, sub):
        rows = slice(r0, r0 + sub)
        pos = pos0 + t * tt + r0 + lax.broadcasted_iota(jnp.int32, (sub, 1), 0)
        for g in range(groups):
            w = POOL_WINDOWS[g]
            sl = slice(g * gw, (g + 1) * gw)
            f = full_ref[r0:r0 + halo + sub, sl]
            s = f
            d = 1
            while d < w:
                s = s + pltpu.roll(s, d, 0)
                d *= 2
            cnt = jnp.minimum(pos + 1, w).astype(F32)
            dlt = (s[halo:] / cnt - f[halo:]).astype(BF16)
            y_ref[rows, sl] = jnp.dot(dlt, pw_ref[g], preferred_element_type=F32)
        xo = _layer_norm(alpha * x_ref[rows, :] + y_ref[rows, :] * ps_ref[...], g_ref[...], b_ref[...])
        o_ref[rows, :] = xo
        normed.append((rows, xo))
    full_ref[0:halo, :] = full_ref[tt:tt + halo, :]
    if with_next:
        for rows, xo in normed:
            on_ref[rows, :] = jnp.dot(xo.astype(BF16), wn_ref[...], preferred_element_type=F32).astype(on_ref.dtype)


def _pool_ln(x, pbuf16, P, j, ln_i, alpha, B, T, pos0, w_next=None, next_l=None, tt_pref=512, sub_pref=256):
    M, D = x.shape
    tt = _tile(T, tt_pref)
    sub = _tile(tt, sub_pref)
    nt = T // tt
    halo = 2 * SUBLANES
    assert max(POOL_WINDOWS) <= halo and tt >= halo
    row = lambda bb, t: (bb * nt + t, 0)
    in_specs = [pl.BlockSpec((tt, D), row),
                pl.BlockSpec((1, halo, D), lambda bb, t: (bb, 0, 0)),
                _sel(P['pw'], j), _sel(P['ps'], j), _sel(P['ln_g'], ln_i), _sel(P['ln_b'], ln_i)]
    out_shape = [jax.ShapeDtypeStruct((M, D), F32)]
    out_specs = [pl.BlockSpec((tt, D), row)]
    operands = [x, pbuf16, P['pw'], P['ps'], P['ln_g'], P['ln_b']]
    if w_next is not None:
        n_next = w_next.shape[2]
        in_specs.append(_sel(w_next, next_l, single=True))
        out_shape.append(jax.ShapeDtypeStruct((M, n_next), BF16))
        out_specs.append(pl.BlockSpec((tt, n_next), row))
        operands.append(w_next)
    out = pl.pallas_call(
        functools.partial(_pool_ln_kernel, tt=tt, sub=sub, pos0=pos0, alpha=alpha, with_next=w_next is not None),
        out_shape=out_shape,
        grid=(B, nt),
        in_specs=in_specs,
        out_specs=out_specs,
        scratch_shapes=[pltpu.VMEM((halo + tt, D), F32), pltpu.VMEM((tt, D), F32)],
        compiler_params=_params("arbitrary", "arbitrary"),
        name="pool_res_ln",
    )(*operands)
    return out[0] if w_next is None else out


def _rope_tables(pos, rope_d, heads):
    half = rope_d // 2
    inv = ROPE_THETA ** (-jnp.arange(half, dtype=F32) / half)
    ang = pos.astype(F32)[:, None] * inv[None, :]
    cos, sin = jnp.cos(ang), jnp.sin(ang)
    cos_t = jnp.tile(jnp.concatenate([cos, cos], axis=1), (1, heads))
    sin_t = jnp.tile(jnp.concatenate([-sin, sin], axis=1), (1, heads))
    return cos_t, sin_t


def _prep_weights(w_in, q_norm_g, w_uq, kv_norm_g, w_uk, w_uv, conv_w, conv_b,
                  gate_a_w, gate_a_b, gate_x_w, gate_x_b, lru_lambda, w_out,
                  pool_w, pool_scale, xa_wq, xa_wo, ln_g, ln_b):
    n_even = w_in.shape[0]
    d_model = ln_g.shape[2]
    q_lora = q_norm_g.shape[1]
    kv_lora = kv_norm_g.shape[1]
    heads, nope = w_uk.shape[2], w_uk.shape[3]
    rope_d = w_uq.shape[2] // heads - nope
    wuq = w_uq.reshape(n_even, q_lora, heads, nope + rope_d)
    wuq = jnp.concatenate([wuq[..., :nope].reshape(n_even, q_lora, heads * nope),
                           wuq[..., nope:].reshape(n_even, q_lora, heads * rope_d)], axis=2).astype(BF16)
    wukt = jnp.transpose(w_uk, (0, 2, 3, 1)).astype(BF16)
    wuv = jnp.transpose(w_uv, (0, 2, 1, 3)).astype(BF16)
    row = lambda a: a.reshape(a.shape[0], 1, -1)
    return dict(
        w_in=w_in.astype(BF16), qg=row(q_norm_g), kg=row(kv_norm_g), wuq=wuq, wukt=wukt, wuv=wuv,
        cw=conv_w, cb=row(conv_b), gaw=gate_a_w.astype(BF16), gab=row(gate_a_b),
        gxw=gate_x_w.astype(BF16), gxb=row(gate_x_b), lam=row(lru_lambda),
        w_out=w_out.astype(BF16), pw=pool_w.astype(BF16), ps=row(pool_scale),
        wq=xa_wq.astype(BF16), wo=xa_wo.astype(BF16),
        ln_g=ln_g.reshape(-1, 1, d_model), ln_b=ln_b.reshape(-1, 1, d_model),
        q_lora=q_lora, kv_lora=kv_lora, heads=heads, nope=nope, rope_d=rope_d)


N_NORMS = 3


class _Group:
    def __init__(self, x3, t_past, mem_k, mem_v, conv_buf, h0, ckv_past, kpe_past, pool_buf, P):
        self.B, self.T, self.D = x3.shape
        B, T = self.B, self.T
        self.x = x3.reshape(B * T, self.D)
        self.t_past, self.mem_k, self.mem_v = t_past, mem_k, mem_v
        self.conv_buf, self.h0, self.ckv_past, self.kpe_past, self.pool_buf = (
            conv_buf, h0, ckv_past, kpe_past, pool_buf)
        self.cos_t, self.sin_t = _rope_tables(t_past + jnp.arange(T), P['rope_d'], P['heads'])
        self.tm_in = _tile(T, IN_PROJ_ROWS) if T >= IN_PROJ_ROWS else B * T
        reps = max(1, self.tm_in // T)
        self.cos_rows = jnp.tile(self.cos_t, (reps, 1))[:, :LANES]
        self.sin_rows = jnp.tile(self.sin_t, (reps, 1))[:, :LANES]
        self.convs, self.hs, self.ckvs, self.kpes, self.pools = [], [], [], [], []

    def pre_mlp(self, l, P, alpha):
        B, T, D = self.B, self.T, self.D
        x, t_past, mem_k, mem_v = self.x, self.t_past, self.mem_k, self.mem_v
        conv_buf, h0, ckv_past, kpe_past, pool_buf = (
            self.conv_buf, self.h0, self.ckv_past, self.kpe_past, self.pool_buf)
        cos_t, sin_t = self.cos_t, self.sin_t
        convs, hs, ckvs, kpes, pools = self.convs, self.hs, self.ckvs, self.kpes, self.pools
        n_norms = N_NORMS
        ln_g, ln_b = P['ln_g'], P['ln_b']
        j = l // 2
        if l % 2 == 0:
            ug, cqn, ckv, kpe = _in_proj(x, P, j, self.cos_rows, self.sin_rows, self.tm_in)
            lru_w = ug.shape[1] // 2
            rec, h_last = _lru(ug, conv_buf[j], h0[j][:, None, :], P, j, B, T)
            ckv3 = ckv.reshape(B, T, -1)
            kpe3 = kpe.reshape(B, T, -1)
            n_keys = ckv_past[j].shape[1] + T
            tk = 512 if T >= 512 and n_keys % 512 == 0 else n_keys + (-n_keys) % 256
            padk = (-n_keys) % tk
            ckv_all = jnp.concatenate([ckv_past[j].astype(BF16), ckv3.astype(BF16),
                                       jnp.zeros((B, padk, ckv3.shape[2]), BF16)], axis=1)
            kpe_all = jnp.concatenate([kpe_past[j].astype(BF16), kpe3.astype(BF16),
                                       jnp.zeros((B, padk, kpe3.shape[2]), BF16)], axis=1)
            attn = _mla_attention(cqn, ckv_all, kpe_all, P, j, cos_t, sin_t, B, T, t_past, n_keys,
                                  tk_pref=tk)
            x, q = _mm_ln([rec, attn], P['w_out'], j, x, ln_g, ln_b, n_norms * l, alpha,
                          w_next=P['wq'], next_l=l)
            tail = P['cw'].shape[1] - 1
            convs.append(ug.reshape(B, T, -1)[:, T - tail:, :lru_w])
            hs.append(h_last[:, 0, :])
            ckvs.append(ckv3)
            kpes.append(kpe3)
        else:
            nbuf = pool_buf[j].shape[1]
            pools.append(x.reshape(B, T, D)[:, T - nbuf:])
            pbuf16 = jnp.pad(pool_buf[j], ((0, 0), (2 * SUBLANES - nbuf, 0), (0, 0)))
            if T >= XATTN_FUSE_ROWS:
                x, q = _pool_ln(x, pbuf16, P, j, n_norms * l, alpha, B, T, t_past, P['wq'], l)
            else:
                x = _pool_ln(x, pbuf16, P, j, n_norms * l, alpha, B, T, t_past)
                q = _rows_matmul(x, P['wq'], BF16, g0=l, n_g=1, tn_pref=512)[0]
        if T >= XATTN_FUSE_ROWS:
            x = _xattn_ln(q, mem_k, mem_v, l, P['wo'], x, ln_g, ln_b, n_norms * l + 1, alpha, B, T)
        else:
            o = _xattn(q, mem_k, mem_v, l, B, T)
            x = _mm_ln([o], P['wo'], l, x, ln_g, ln_b, n_norms * l + 1, alpha)
        self.x = x

    def results(self):
        return (self.x.reshape(self.B, self.T, self.D), jnp.stack(self.convs), jnp.stack(self.hs),
                jnp.stack(self.ckvs), jnp.stack(self.kpes), jnp.stack(self.pools))


def kernel(x_prompt, x_sample, mem_prompt, cache_mem_k, cache_mem_v, cache_mla_ckv, cache_mla_kpe,
           state_rglru_h, state_rglru_conv, state_pool,
           w_in, q_norm_g, w_uq, kv_norm_g, w_uk, w_uv, conv_w, conv_b,
           gate_a_w, gate_a_b, gate_x_w, gate_x_b, lru_lambda, w_out,
           pool_w, pool_scale, xa_wq, xa_wk, xa_wv, xa_wo, mlp_up, mlp_down, ln_g, ln_b):
    P = _prep_weights(w_in, q_norm_g, w_uq, kv_norm_g, w_uk, w_uv, conv_w, conv_b,
                      gate_a_w, gate_a_b, gate_x_w, gate_x_b, lru_lambda, w_out,
                      pool_w, pool_scale, xa_wq, xa_wo, ln_g, ln_b)
    depth = ln_g.shape[0]
    n_even, n_odd = w_in.shape[0], pool_w.shape[0]
    bp, _, d_model = x_prompt.shape
    n_mem = mem_prompt.shape[1]
    dt = x_prompt.dtype
    lru_w = conv_w.shape[2]
    mem_rows = mem_prompt.reshape(bp * n_mem, d_model)
    p_mem_k = _rows_matmul(mem_rows, xa_wk, F32, tm_pref=1024, tn_pref=512).reshape(depth, bp, n_mem, d_model)
    p_mem_v = _rows_matmul(mem_rows, xa_wv, F32, tm_pref=1024, tn_pref=512).reshape(depth, bp, n_mem, d_model)
    prompt = _Group(
        x_prompt, 0, p_mem_k, p_mem_v,
        jnp.zeros((n_even, bp, conv_w.shape[1] - 1, lru_w), dt),
        jnp.zeros((n_even, bp, lru_w), dt),
        jnp.zeros((n_even, bp, 0, kv_norm_g.shape[1]), dt),
        jnp.zeros((n_even, bp, 0, cache_mla_kpe.shape[3]), dt),
        jnp.zeros((n_odd, bp, state_pool.shape[2], d_model), dt), P)
    sample = _Group(
        x_sample, cache_mla_ckv.shape[2], cache_mem_k, cache_mem_v,
        state_rglru_conv, state_rglru_h, cache_mla_ckv, cache_mla_kpe, state_pool, P)
    alpha = (2.0 * depth) ** 0.25
    ln_g3, ln_b3 = P['ln_g'], P['ln_b']
    for l in range(depth):
        sample.pre_mlp(l, P, alpha)
        prompt.pre_mlp(l, P, alpha)
        ln_i = N_NORMS * l + 2
        sample.x, up_b, down_b = _mlp_ln(sample.x, mlp_up, mlp_down, l, ln_g3, ln_b3, ln_i, alpha,
                                         emit=True, tf_pref=512)
        prompt.x = _mlp_ln(prompt.x, up_b, down_b, None, ln_g3, ln_b3, ln_i, alpha)
    y_prompt, p_conv, p_h, p_ckv, p_kpe, p_pool = prompt.results()
    y_sample, s_conv, s_h, s_ckv, s_kpe, s_pool = sample.results()
    return (y_prompt, y_sample, p_conv, p_h, p_ckv, p_kpe, p_pool, p_mem_k, p_mem_v,
            s_conv, s_h, s_ckv, s_kpe, s_pool)
```
